```python
import math
import jax, jax.numpy as jnp
from jax import lax
import numpy as np

D_MODEL = 2048
BATCH = 1
SEQ = 8192
DEPTH = 2

F32 = jnp.float32
N_BRANCH = 4
BRANCH_WIDTH = D_MODEL // N_BRANCH
NORM_EPS = 1e-6
GN_EPS = 1e-5
ROPE_BASE = 10000.0
LRU_WIDTH = BRANCH_WIDTH
LRU_BLOCKS = 8
LRU_BLOCK_DIM = LRU_WIDTH // LRU_BLOCKS
CONV_WIDTH = 4
LRU_C = 8.0
MLA_HEADS = 4
MLA_NOPE_DIM = 128
MLA_ROPE_DIM = 64
MLA_V_DIM = 128
MLA_QK_DIM = MLA_NOPE_DIM + MLA_ROPE_DIM
MLA_Q_LORA = 384
MLA_KV_LORA = 256
Q_BLOCK = 128
RET_HEADS = 4
RET_HEAD_DIM = BRANCH_WIDTH // RET_HEADS
RET_WIDTH = RET_HEADS * RET_HEAD_DIM
RET_CHUNK = 128
RWKV_HEAD_DIM = 64
RWKV_HEADS = BRANCH_WIDTH // RWKV_HEAD_DIM
RWKV_WIDTH = RWKV_HEADS * RWKV_HEAD_DIM
RWKV_DECAY_LORA = 64
RWKV_AAA_LORA = 64
RWKV_GATE_LORA = 128
RWKV_COLS = 3 * RWKV_WIDTH + RWKV_DECAY_LORA + RWKV_AAA_LORA + RWKV_GATE_LORA
FFN_HIDDEN = -(-8 * D_MODEL // (3 * 256)) * 256
GATE_COLS = N_BRANCH * D_MODEL
IN_SPLITS = (GATE_COLS, LRU_WIDTH, LRU_WIDTH, MLA_Q_LORA, MLA_KV_LORA, MLA_ROPE_DIM,
             4 * RET_WIDTH, RWKV_COLS)
N_IN = sum(IN_SPLITS)

kernel_name = 'hybrid_lru_mla_retnet_rwkv7_adaln_block'


def split_cols(p, sizes):
    idx = np.cumsum(sizes)[:-1].tolist()
    return jnp.split(p, idx, axis=-1)


def rms_norm(x, g, eps=NORM_EPS):
    xf = x.astype(F32)
    y = xf * lax.rsqrt(jnp.mean(xf * xf, axis=-1, keepdims=True) + eps)
    return (y * g.astype(F32)).astype(x.dtype)


def head_group_norm(x, gain, eps=GN_EPS):
    xf = x.astype(F32)
    xc = xf - jnp.mean(xf, axis=-1, keepdims=True)
    return xc * lax.rsqrt(jnp.mean(xc * xc, axis=-1, keepdims=True) + eps) * gain.astype(F32)


def rope_tables(positions, dim):
    inv = 1.0 / (ROPE_BASE ** (jnp.arange(0, dim, 2, dtype=F32) / dim))
    ang = positions.astype(F32)[..., None] * inv
    return jnp.cos(ang), jnp.sin(ang)


def apply_rope(x, cos, sin):
    x1, x2 = jnp.split(x, 2, axis=-1)
    c = cos[:, :, None, :]
    s = sin[:, :, None, :]
    return jnp.concatenate([x1 * c - x2 * s, x1 * s + x2 * c], axis=-1).astype(x.dtype)


def causal_depthwise_conv(x, w, b):
    y = lax.conv_general_dilated(
        x, w[:, None, :].astype(x.dtype), window_strides=(1,),
        padding=[(CONV_WIDTH - 1, 0)], dimension_numbers=('NWC', 'WIO', 'NWC'),
        feature_group_count=x.shape[-1])
    return y + b.astype(x.dtype)


def rg_lru(x, wr, br, wi, bi, lam):
    B, T, _ = x.shape
    xf = x.astype(F32)
    xb = xf.reshape(B, T, LRU_BLOCKS, LRU_BLOCK_DIM)
    r = jax.nn.sigmoid(jnp.einsum('btni,nij->btnj', xb, wr.astype(F32)).reshape(B, T, LRU_WIDTH) + br)
    i = jax.nn.sigmoid(jnp.einsum('btni,nij->btnj', xb, wi.astype(F32)).reshape(B, T, LRU_WIDTH) + bi)
    log_a = -LRU_C * jax.nn.softplus(-lam.astype(F32)) * r
    a = jnp.exp(log_a)
    u = jnp.sqrt(-jnp.expm1(2.0 * log_a)) * (i * xf)

    def combine(left, right):
        a1, u1 = left
        a2, u2 = right
        return a1 * a2, a2 * u1 + u2

    _, h = lax.associative_scan(combine, (a, u), axis=1)
    return h


def causal_block_attention(q, k, v, scale):
    B, T, H, Dk = q.shape
    n_blk = T // Q_BLOCK
    qb = jnp.moveaxis(q.reshape(B, n_blk, Q_BLOCK, H, Dk), 1, 0)
    kpos = jnp.arange(T)

    def one_block(args):
        qi, bi = args
        s = jnp.einsum('bqhd,bkhd->bhqk', qi, k, preferred_element_type=F32) * scale
        qpos = bi * Q_BLOCK + jnp.arange(Q_BLOCK)
        s = jnp.where(kpos[None, :] <= qpos[:, None], s, -jnp.inf)
        p = jax.nn.softmax(s, axis=-1)
        return jnp.einsum('bhqk,bkhd->bqhd', p.astype(v.dtype), v)

    o = lax.map(one_block, (qb, jnp.arange(n_blk)))
    return jnp.moveaxis(o, 0, 1).reshape(B, T, H, v.shape[-1])


def mla_branch(cq, ckv, k_rope, cos, sin, g_cq, g_ckv, w_uq, w_ukv, g_qn, g_kn):
    B, T, _ = cq.shape
    q = (rms_norm(cq, g_cq) @ w_uq).reshape(B, T, MLA_HEADS, MLA_QK_DIM)
    kv = (rms_norm(ckv, g_ckv) @ w_ukv).reshape(B, T, MLA_HEADS, MLA_NOPE_DIM + MLA_V_DIM)
    k_nope, v = jnp.split(kv, [MLA_NOPE_DIM], axis=-1)
    k_r = jnp.broadcast_to(k_rope[:, :, None, :], (B, T, MLA_HEADS, MLA_ROPE_DIM)).astype(k_nope.dtype)
    k = jnp.concatenate([k_nope, k_r], axis=-1)
    q = rms_norm(q, g_qn)
    k = rms_norm(k, g_kn)
    q = jnp.concatenate([q[..., :MLA_NOPE_DIM], apply_rope(q[..., MLA_NOPE_DIM:], cos, sin)], axis=-1)
    k = jnp.concatenate([k[..., :MLA_NOPE_DIM], apply_rope(k[..., MLA_NOPE_DIM:], cos, sin)], axis=-1)
    o = causal_block_attention(q, k, v, MLA_QK_DIM ** -0.5)
    return o.reshape(B, T, MLA_HEADS * MLA_V_DIM)


def retention_branch(ret_p, cos, sin, g_norm):
    B, T, _ = ret_p.shape
    H, Dh, C = RET_HEADS, RET_HEAD_DIM, RET_CHUNK
    q, k, v, g = jnp.split(ret_p, 4, axis=-1)
    q = apply_rope(q.reshape(B, T, H, Dh), cos, sin).astype(F32)
    k = apply_rope(k.reshape(B, T, H, Dh), cos, sin).astype(F32) * (Dh ** -0.5)
    v = v.reshape(B, T, H, Dh).astype(F32)
    log_gamma = jnp.log1p(-jnp.exp(jnp.linspace(math.log(1.0 / 32), math.log(1.0 / 512), H, dtype=F32)))
    idx = jnp.arange(C, dtype=F32)
    rel = idx[:, None] - idx[None, :]
    decay_in = jnp.where(rel >= 0, jnp.exp(log_gamma[:, None, None] * jnp.maximum(rel, 0.0)), 0.0)
    q_dec = jnp.exp(log_gamma[:, None] * (idx + 1.0))[None, :, :, None]
    k_dec = jnp.exp(log_gamma[:, None] * (C - 1.0 - idx))[None, :, :, None]
    chunk_dec = jnp.exp(log_gamma * C)[None, :, None, None]
    n = T // C

    def to_chunks(t):
        return t.reshape(B, n, C, H, Dh).transpose(1, 0, 3, 2, 4)

    def step(state, inp):
        qi, ki, vi = inp
        inner = jnp.einsum('bhqd,bhkd->bhqk', qi, ki) * decay_in
        o = jnp.einsum('bhqk,bhkv->bhqv', inner, vi) + jnp.einsum('bhqd,bhdv->bhqv', qi * q_dec, state)
        state = chunk_dec * state + jnp.einsum('bhkd,bhkv->bhdv', ki * k_dec, vi)
        return state, o

    state0 = jnp.zeros((B, H, Dh, Dh), F32)
    _, o = lax.scan(step, state0, (to_chunks(q), to_chunks(k), to_chunks(v)))
    o = o.transpose(1, 0, 3, 2, 4).reshape(B, T, H, Dh)
    o = head_group_norm(o, g_norm.reshape(H, Dh)).reshape(B, T, H * Dh)
    return jax.nn.silu(g.astype(F32)) * o


def rwkv7_branch(p, mu, w0, w_up, a0, a_up, g_up, k_k, k_a, r_k, g_norm):
    B, T, _ = p.shape
    H, N = RWKV_HEADS, RWKV_HEAD_DIM
    pf = p.astype(F32)
    prev = jnp.pad(pf, ((0, 0), (1, 0), (0, 0)))[:, :-1]
    xs = pf + (prev - pf) * mu.astype(F32)
    r, k, v, wd, ad, gd = split_cols(xs, (RWKV_WIDTH, RWKV_WIDTH, RWKV_WIDTH,
                                          RWKV_DECAY_LORA, RWKV_AAA_LORA, RWKV_GATE_LORA))
    w_log = -jax.nn.softplus(-(w0 + jnp.tanh(wd) @ w_up)) - 0.5
    decay = jnp.exp(-jnp.exp(w_log))
    a = jax.nn.sigmoid(a0 + ad @ a_up)
    g = jax.nn.sigmoid(gd) @ g_up

    def heads(t):
        return t.reshape(B, T, H, N).astype(F32)

    kk = heads(k * k_k)
    kk = kk / jnp.maximum(jnp.sqrt(jnp.sum(kk * kk, axis=-1, keepdims=True)), 1e-12)
    k = heads(k * (1.0 + (a - 1.0) * k_a))
    r, v, w, a = heads(r), heads(v), heads(decay), heads(a)

    def step(S, inp):
        r_t, w_t, k_t, v_t, neg_kk_t, akk_t = inp
        sa = jnp.einsum('bhvk,bhk->bhv', S, neg_kk_t)
        S = S * w_t[:, :, None, :] + sa[..., None] * akk_t[:, :, None, :] + v_t[..., None] * k_t[:, :, None, :]
        return S, jnp.einsum('bhvk,bhk->bhv', S, r_t)

    def tm(t):
        return jnp.moveaxis(t, 1, 0)

    S0 = jnp.zeros((B, H, N, N), F32)
    _, o = lax.scan(step, S0, (tm(r), tm(w), tm(k), tm(v), tm(-kk), tm(kk * a)))
    o = jnp.moveaxis(o, 0, 1)
    o = head_group_norm(o, g_norm.reshape(H, N))
    o = o + jnp.sum(r * k * r_k.reshape(H, N).astype(F32), axis=-1, keepdims=True) * v
    return o.reshape(B, T, H * N) * g


def setup_inputs(seed: int = 0) -> dict:
    key = jax.random.key(seed)
    ks = list(jax.random.split(key, 40))
    L = DEPTH

    def nrm(i, shape, scale):
        return jax.random.normal(ks[i], shape, F32) * scale

    def unif(i, shape, lo, hi):
        return jax.random.uniform(ks[i], shape, F32, lo, hi)

    x = nrm(0, (BATCH, SEQ, D_MODEL), 1.0)
    c = nrm(1, (BATCH, D_MODEL), 1.0)
    start = jax.random.randint(ks[2], (BATCH, 1), 0, 4096, dtype=jnp.int32)
    positions = (start + jnp.arange(SEQ, dtype=jnp.int32)[None, :]).astype(jnp.int32)
    u = unif(14, (L, LRU_WIDTH), 0.9, 0.999)
    log_a = jnp.log(u) / LRU_C
    lru_lam = log_a - jnp.log(-jnp.expm1(log_a))
    return {
        'x': x, 'c': c, 'positions': positions,
        'ada_w': nrm(3, (L, D_MODEL, 6 * D_MODEL), 0.25 * D_MODEL ** -0.5),
        'ada_b': nrm(4, (L, 6 * D_MODEL), 0.02),
        'norm_mix': 1.0 + nrm(5, (L, D_MODEL), 0.02),
        'norm_ffn': 1.0 + nrm(6, (L, D_MODEL), 0.02),
        'w_in': nrm(7, (L, D_MODEL, N_IN), D_MODEL ** -0.5),
        'conv_w': nrm(8, (L, CONV_WIDTH, LRU_WIDTH), CONV_WIDTH ** -0.5),
        'conv_b': nrm(9, (L, LRU_WIDTH), 0.02),
        'lru_wr': nrm(10, (L, LRU_BLOCKS, LRU_BLOCK_DIM, LRU_BLOCK_DIM), LRU_BLOCK_DIM ** -0.5),
        'lru_br': nrm(11, (L, LRU_WIDTH), 0.02),
        'lru_wi': nrm(12, (L, LRU_BLOCKS, LRU_BLOCK_DIM, LRU_BLOCK_DIM), LRU_BLOCK_DIM ** -0.5),
        'lru_bi': nrm(13, (L, LRU_WIDTH), 0.02),
        'lru_lam': lru_lam,
        'mla_g_cq': 1.0 + nrm(15, (L, MLA_Q_LORA), 0.02),
        'mla_g_ckv': 1.0 + nrm(16, (L, MLA_KV_LORA), 0.02),
        'mla_w_uq': nrm(17, (L, MLA_Q_LORA, MLA_HEADS * MLA_QK_DIM), MLA_Q_LORA ** -0.5),
        'mla_w_ukv': nrm(18, (L, MLA_KV_LORA, MLA_HEADS * (MLA_NOPE_DIM + MLA_V_DIM)), MLA_KV_LORA ** -0.5),
        'mla_g_qn': 1.0 + nrm(19, (L, MLA_QK_DIM), 0.02),
        'mla_g_kn': 1.0 + nrm(20, (L, MLA_QK_DIM), 0.02),
        'ret_g_norm': 1.0 + nrm(21, (L, RET_WIDTH), 0.02),
        'rwkv_mu': unif(22, (L, RWKV_COLS), 0.0, 1.0),
        'rwkv_w0': unif(23, (L, RWKV_WIDTH), -6.5, -1.5),
        'rwkv_w_up': nrm(24, (L, RWKV_DECAY_LORA, RWKV_WIDTH), 0.1),
        'rwkv_a0': nrm(25, (L, RWKV_WIDTH), 0.1),
        'rwkv_a_up': nrm(26, (L, RWKV_AAA_LORA, RWKV_WIDTH), 0.1),
        'rwkv_g_up': nrm(27, (L, RWKV_GATE_LORA, RWKV_WIDTH), RWKV_GATE_LORA ** -0.5),
        'rwkv_k_k': 0.85 + nrm(28, (L, RWKV_WIDTH), 0.02),
        'rwkv_k_a': 1.0 + nrm(29, (L, RWKV_WIDTH), 0.02),
        'rwkv_r_k': nrm(30, (L, RWKV_WIDTH), 0.1),
        'rwkv_g_norm': 1.0 + nrm(31, (L, RWKV_WIDTH), 0.02),
        'w_branch': nrm(32, (L, N_BRANCH, BRANCH_WIDTH, D_MODEL), BRANCH_WIDTH ** -0.5),
        'w_out': nrm(33, (L, D_MODEL, D_MODEL), D_MODEL ** -0.5),
        'ffn_w_in': nrm(34, (L, D_MODEL, 2 * FFN_HIDDEN), D_MODEL ** -0.5),
        'ffn_w_out': nrm(35, (L, FFN_HIDDEN, D_MODEL), FFN_HIDDEN ** -0.5),
    }


def reference(x, c, positions, ada_w, ada_b, norm_mix, norm_ffn, w_in, conv_w, conv_b,
              lru_wr, lru_br, lru_wi, lru_bi, lru_lam, mla_g_cq, mla_g_ckv, mla_w_uq, mla_w_ukv,
              mla_g_qn, mla_g_kn, ret_g_norm, rwkv_mu, rwkv_w0, rwkv_w_up, rwkv_a0, rwkv_a_up,
              rwkv_g_up, rwkv_k_k, rwkv_k_a, rwkv_r_k, rwkv_g_norm, w_branch, w_out,
              ffn_w_in, ffn_w_out):
    B, T, _ = x.shape
    dt = x.dtype
    cos_m, sin_m = rope_tables(positions, MLA_ROPE_DIM)
    cos_r, sin_r = rope_tables(positions, RET_HEAD_DIM)
    c_act = jax.nn.silu(c)
    for l in range(DEPTH):
        mod = c_act @ ada_w[l] + ada_b[l]
        shift_m, scale_m, gate_m, shift_f, scale_f, gate_f = [m[:, None, :] for m in jnp.split(mod, 6, axis=-1)]

        h = (rms_norm(x, norm_mix[l]) * (1.0 + scale_m) + shift_m).astype(dt)
        p = h @ w_in[l]
        gate_logits, a_x, a_gate, cq, ckv, k_rope, ret_p, rwkv_p = split_cols(p, IN_SPLITS)

        xa = causal_depthwise_conv(a_x, conv_w[l], conv_b[l])
        y_a = (rg_lru(xa, lru_wr[l], lru_br[l], lru_wi[l], lru_bi[l], lru_lam[l])
               * jax.nn.gelu(a_gate.astype(F32))).astype(dt)
        y_b = mla_branch(cq, ckv, k_rope, cos_m, sin_m, mla_g_cq[l], mla_g_ckv[l], mla_w_uq[l],
                         mla_w_ukv[l], mla_g_qn[l], mla_g_kn[l]).astype(dt)
        y_c = retention_branch(ret_p, cos_r, sin_r, ret_g_norm[l]).astype(dt)
        y_d = rwkv7_branch(rwkv_p, rwkv_mu[l], rwkv_w0[l], rwkv_w_up[l], rwkv_a0[l], rwkv_a_up[l],
                           rwkv_g_up[l], rwkv_k_k[l], rwkv_k_a[l], rwkv_r_k[l], rwkv_g_norm[l]).astype(dt)

        ys = jnp.stack([y_a, y_b, y_c, y_d], axis=2)
        branch = jnp.einsum('btni,nid->btnd', ys, w_branch[l])
        gates = jax.nn.sigmoid(gate_logits.astype(F32)).reshape(B, T, N_BRANCH, D_MODEL)
        merged = jnp.sum(gates * branch, axis=2).astype(dt)
        x = x + (gate_m * (merged @ w_out[l])).astype(dt)

        h2 = (rms_norm(x, norm_ffn[l]) * (1.0 + scale_f) + shift_f).astype(dt)
        u_gate, u_val = jnp.split(h2 @ ffn_w_in[l], 2, axis=-1)
        ff = (jax.nn.silu(u_gate) * u_val) @ ffn_w_out[l]
        x = x + (gate_f * ff).astype(dt)
    return x
```

```python
import functools
import math

import numpy as np
import jax
import jax.numpy as jnp
from jax import lax
from jax.experimental import pallas as pl
from jax.experimental.pallas import tpu as pltpu

F32 = jnp.float32
BF16 = jnp.bfloat16
HIGHEST = lax.Precision.HIGHEST

D_MODEL = 2048
N_BRANCH = 4
BRANCH_WIDTH = D_MODEL // N_BRANCH
NORM_EPS = 1e-6
GN_EPS = 1e-5
ROPE_BASE = 10000.0
LRU_WIDTH = BRANCH_WIDTH
LRU_C = 8.0
CONV_WIDTH = 4
MLA_HEADS = 4
MLA_NOPE_DIM = 128
MLA_ROPE_DIM = 64
MLA_V_DIM = 128
MLA_QK_DIM = MLA_NOPE_DIM + MLA_ROPE_DIM
MLA_QK_PAD = 256
MLA_Q_LORA = 384
MLA_KV_LORA = 256
RET_HEADS = 4
RET_HEAD_DIM = 128
RET_WIDTH = RET_HEADS * RET_HEAD_DIM
RET_CHUNK = 128
RWKV_HEAD_DIM = 64
RWKV_HEADS = 8
RWKV_WIDTH = RWKV_HEADS * RWKV_HEAD_DIM
RWKV_DECAY_LORA = 64
RWKV_AAA_LORA = 64
RWKV_GATE_LORA = 128
RWKV_CHUNK = 64
FFN_HIDDEN = 5632
GATE_COLS = N_BRANCH * D_MODEL
LANE = 128
SUBLANE = 8
VMEM_LIMIT_BYTES = 56 * 1024 * 1024


def _params(*sem):
    return pltpu.CompilerParams(dimension_semantics=sem, vmem_limit_bytes=VMEM_LIMIT_BYTES)


def _sigmoid(x):
    return 1.0 / (1.0 + jnp.exp(-x))


def _softplus(x):
    return jnp.maximum(x, 0.0) + jnp.log(1.0 + jnp.exp(-jnp.abs(x)))


def _bdot(a, b):
    return jnp.dot(a.astype(BF16), b.astype(BF16), preferred_element_type=F32)


def _fdot(a, b):
    return jnp.dot(a, b, preferred_element_type=F32, precision=HIGHEST)


def _mod_kernel(c_ref, w_ref, b_ref, o_ref):
    c = c_ref[...]
    ca = c * _sigmoid(c)
    o_ref[0] = jnp.sum(ca * w_ref[0], axis=0, keepdims=True) + b_ref[0]


def _modulation(c, ada_w, ada_b):
    L, D, N = ada_w.shape
    tn = 1024
    return pl.pallas_call(
        _mod_kernel,
        grid=(L, N // tn),
        in_specs=[pl.BlockSpec((D, 1), lambda l, j: (0, 0)),
                  pl.BlockSpec((1, D, tn), lambda l, j: (l, 0, j)),
                  pl.BlockSpec((1, 1, tn), lambda l, j: (l, 0, j))],
        out_specs=pl.BlockSpec((1, 1, tn), lambda l, j: (l, 0, j)),
        out_shape=jax.ShapeDtypeStruct((L, 1, N), F32),
        compiler_params=_params("arbitrary", "arbitrary"),
        name="adaln_mod",
    )(c.reshape(D, 1), ada_w, ada_b.reshape(L, 1, N))


def _norm_kernel(x_ref, g_ref, sh_ref, sc_ref, o_ref):
    x = x_ref[...]
    ms = jnp.mean(x * x, axis=-1, keepdims=True)
    y = x * lax.rsqrt(ms + NORM_EPS) * g_ref[...]
    o_ref[...] = (y * (1.0 + sc_ref[...]) + sh_ref[...]).astype(o_ref.dtype)


def _mod_norm(x, g, mod, shift_idx):
    T, D = x.shape
    tm = 512
    return pl.pallas_call(
        _norm_kernel,
        grid=(T // tm,),
        in_specs=[pl.BlockSpec((tm, D), lambda i: (i, 0)),
                  pl.BlockSpec((1, D), lambda i: (0, 0)),
                  pl.BlockSpec((1, D), lambda i: (0, shift_idx)),
                  pl.BlockSpec((1, D), lambda i: (0, shift_idx + 1))],
        out_specs=pl.BlockSpec((tm, D), lambda i: (i, 0)),
        out_shape=jax.ShapeDtypeStruct((T, D), BF16),
        compiler_params=_params("arbitrary"),
        name="mod_norm",
    )(x, g, mod, mod)


def _mm_kernel(a_ref, b_ref, o_ref):
    o_ref[...] = jnp.dot(a_ref[...], b_ref[...], preferred_element_type=F32).astype(o_ref.dtype)


def _matmul(a, b, out_dtype, name):
    M, K = a.shape
    N = b.shape[1]
    tm = 512
    tn = N if N <= 2048 else 512
    return pl.pallas_call(
        _mm_kernel,
        grid=(M // tm, N // tn),
        in_specs=[pl.BlockSpec((tm, K), lambda i, j: (i, 0)),
                  pl.BlockSpec((K, tn), lambda i, j: (0, j))],
        out_specs=pl.BlockSpec((tm, tn), lambda i, j: (i, j)),
        out_shape=jax.ShapeDtypeStruct((M, N), out_dtype),
        compiler_params=_params("arbitrary", "arbitrary"),
        name=name,
    )(a, b)


def _mm_res_kernel(a_ref, b_ref, x_ref, g_ref, o_ref):
    acc = jnp.dot(a_ref[...], b_ref[...], preferred_element_type=F32)
    o_ref[...] = x_ref[...] + g_ref[...] * acc


def _matmul_gated_residual(a, b, x, mod, gate_idx, name):
    M, K = a.shape
    N = b.shape[1]
    tm, tn = 512, 512
    nj = N // tn
    return pl.pallas_call(
        _mm_res_kernel,
        grid=(M // tm, nj),
        in_specs=[pl.BlockSpec((tm, K), lambda i, j: (i, 0)),
                  pl.BlockSpec((K, tn), lambda i, j: (0, j)),
                  pl.BlockSpec((tm, tn), lambda i, j: (i, j)),
                  pl.BlockSpec((1, tn), lambda i, j: (0, gate_idx * nj + j))],
        out_specs=pl.BlockSpec((tm, tn), lambda i, j: (i, j)),
        out_shape=jax.ShapeDtypeStruct((M, N), F32),
        compiler_params=_params("arbitrary", "arbitrary"),
        name=name,
    )(a, b, x, mod)


def _swiglu_kernel(a_ref, bg_ref, bv_ref, o_ref):
    a = a_ref[...]
    ug = jnp.dot(a, bg_ref[...], preferred_element_type=F32)
    uv = jnp.dot(a, bv_ref[...], preferred_element_type=F32)
    o_ref[...] = (ug * _sigmoid(ug) * uv).astype(o_ref.dtype)


def _swiglu_in(a, w):
    M, K = a.shape
    H = w.shape[1] // 2
    tm, tn = 512, 512
    nj = H // tn
    return pl.pallas_call(
        _swiglu_kernel,
        grid=(M // tm, nj),
        in_specs=[pl.BlockSpec((tm, K), lambda i, j: (i, 0)),
                  pl.BlockSpec((K, tn), lambda i, j: (0, j)),
                  pl.BlockSpec((K, tn), lambda i, j: (0, nj + j))],
        out_specs=pl.BlockSpec((tm, tn), lambda i, j: (i, j)),
        out_shape=jax.ShapeDtypeStruct((M, H), BF16),
        compiler_params=_params("arbitrary", "arbitrary"),
        name="ffn_in_swiglu",
    )(a, w, w)


def _merge_kernel(h_ref, g0_ref, g1_ref, g2_ref, g3_ref, y0_ref, y1_ref, y2_ref, y3_ref,
                  wb_ref, o_ref):
    h = h_ref[...]
    acc = None
    for n, (g_ref, y_ref) in enumerate(((g0_ref, y0_ref), (g1_ref, y1_ref),
                                        (g2_ref, y2_ref), (g3_ref, y3_ref))):
        logits = jnp.dot(h, g_ref[...], preferred_element_type=F32)
        branch = jnp.dot(y_ref[...], wb_ref[n], preferred_element_type=F32)
        term = _sigmoid(logits) * branch
        acc = term if acc is None else acc + term
    o_ref[...] = acc.astype(o_ref.dtype)


def _gated_merge(h, w_gate, ys, w_branch):
    M, K = h.shape
    D = w_branch.shape[2]
    BW = w_branch.shape[1]
    tm, tn = 512, 512
    nj = D // tn
    gate_specs = [pl.BlockSpec((K, tn), functools.partial(lambda i, j, n: (0, n * nj + j), n=n))
                  for n in range(N_BRANCH)]
    y_specs = [pl.BlockSpec((tm, BW), lambda i, j: (i, 0)) for _ in range(N_BRANCH)]
    return pl.pallas_call(
        _merge_kernel,
        grid=(M // tm, nj),
        in_specs=[pl.BlockSpec((tm, K), lambda i, j: (i, 0))] + gate_specs + y_specs
                 + [pl.BlockSpec((N_BRANCH, BW, tn), lambda i, j: (0, 0, j))],
        out_specs=pl.BlockSpec((tm, tn), lambda i, j: (i, j)),
        out_shape=jax.ShapeDtypeStruct((M, D), BF16),
        compiler_params=_params("arbitrary", "arbitrary"),
        name="gated_merge",
    )(h, w_gate, w_gate, w_gate, w_gate, *ys, w_branch)


def _lru_kernel(ax_ref, ag_ref, cw_ref, cb_ref, wr_ref, br_ref, wi_ref, bi_ref, lam_ref,
                o_ref, xbuf, hcar):
    i = pl.program_id(0)
    tt = ax_ref.shape[0]

    @pl.when(i == 0)
    def _():
        xbuf[0:SUBLANE, :] = jnp.zeros((SUBLANE, LRU_WIDTH), F32)
        hcar[...] = jnp.zeros_like(hcar)

    xbuf[SUBLANE:SUBLANE + tt, :] = ax_ref[...]
    xa = cb_ref[...]
    for j in range(CONV_WIDTH):
        xa = xa + cw_ref[j:j + 1, :] * xbuf[pl.ds(SUBLANE - (CONV_WIDTH - 1) + j, tt), :]
    xbuf[0:SUBLANE, :] = ax_ref[tt - SUBLANE:tt, :]

    r = _sigmoid(_bdot(xa, wr_ref[...]) + br_ref[...])
    ig = _sigmoid(_bdot(xa, wi_ref[...]) + bi_ref[...])
    log_a = (-LRU_C * _softplus(-lam_ref[...])) * r
    a = jnp.exp(log_a)
    th = jnp.tanh(log_a)
    u = jnp.sqrt(-2.0 * th / (1.0 - th)) * (ig * xa)

    row = lax.broadcasted_iota(jnp.int32, (tt, LRU_WIDTH), 0)
    d = 1
    while d < tt:
        keep = row >= d
        u_s = jnp.where(keep, pltpu.roll(u, d, 0), 0.0)
        a_s = jnp.where(keep, pltpu.roll(a, d, 0), 1.0)
        u = u + a * u_s
        a = a * a_s
        d *= 2
    h = u + a * hcar[...]
    hcar[...] = h[tt - 1:tt, :]

    g = ag_ref[...]
    gelu = 0.5 * g * (1.0 + jnp.tanh(math.sqrt(2.0 / math.pi) * (g + 0.044715 * (g * g * g))))
    o_ref[...] = (h * gelu).astype(o_ref.dtype)


def _lru_branch(p_a, conv_w, conv_b, wr_bd, br, wi_bd, bi, lam):
    T = p_a.shape[0]
    W = LRU_WIDTH
    tt = 256
    row = lambda i: (0, 0)
    return pl.pallas_call(
        _lru_kernel,
        grid=(T // tt,),
        in_specs=[pl.BlockSpec((tt, W), lambda i: (i, 0)),
                  pl.BlockSpec((tt, W), lambda i: (i, 1)),
                  pl.BlockSpec((CONV_WIDTH, W), row),
                  pl.BlockSpec((1, W), row),
                  pl.BlockSpec((W, W), row),
                  pl.BlockSpec((1, W), row),
                  pl.BlockSpec((W, W), row),
                  pl.BlockSpec((1, W), row),
                  pl.BlockSpec((1, W), row)],
        out_specs=pl.BlockSpec((tt, W), lambda i: (i, 0)),
        out_shape=jax.ShapeDtypeStruct((T, W), BF16),
        scratch_shapes=[pltpu.VMEM((tt + SUBLANE, W), F32), pltpu.VMEM((1, W), F32)],
        compiler_params=_params("arbitrary"),
        name="rg_lru",
    )(p_a, p_a, conv_w, conv_b, wr_bd, br, wi_bd, bi, lam)


def _mla_prep_kernel(p_ref, gcq_ref, gckv_ref, wuq_ref, wuk_ref, wuv_ref, gq_ref, gk_ref,
                     cc_ref, sa_ref, sb_ref, q_ref, k_ref, v_ref):
    p = p_ref[...]
    cq = p[:, :MLA_Q_LORA]
    ckv = p[:, MLA_Q_LORA:MLA_Q_LORA + MLA_KV_LORA]
    kr = p[:, MLA_Q_LORA + MLA_KV_LORA:]

    def rms(x, g, n):
        return x * lax.rsqrt(jnp.sum(x * x, axis=-1, keepdims=True) * (1.0 / n) + NORM_EPS) * g

    cqn = rms(cq, gcq_ref[...], MLA_Q_LORA)
    ckvn = rms(ckv, gckv_ref[...], MLA_KV_LORA)
    q = _bdot(cqn, wuq_ref[...])
    kn = _bdot(ckvn, wuk_ref[...])
    v_ref[...] = _bdot(ckvn, wuv_ref[...]).astype(v_ref.dtype)
    cc, sa, sb = cc_ref[...], sa_ref[...], sb_ref[...]
    half = MLA_ROPE_DIM // 2

    def rope(x):
        x2 = x[:, LANE:]
        x2 = x2 * cc + pltpu.roll(x2, LANE - half, 1) * sa + pltpu.roll(x2, half, 1) * sb
        return jnp.concatenate([x[:, :LANE], x2], axis=-1)

    scale = MLA_QK_DIM ** -0.5
    for h in range(MLA_HEADS):
        qh = rms(q[:, h * MLA_QK_PAD:(h + 1) * MLA_QK_PAD], gq_ref[...], MLA_QK_DIM)
        q_ref[:, h * MLA_QK_PAD:(h + 1) * MLA_QK_PAD] = (rope(qh) * scale).astype(q_ref.dtype)
        kh = jnp.concatenate([kn[:, h * MLA_NOPE_DIM:(h + 1) * MLA_NOPE_DIM], kr], axis=-1)
        kh = rms(kh, gk_ref[...], MLA_QK_DIM)
        k_ref[:, h * MLA_QK_PAD:(h + 1) * MLA_QK_PAD] = rope(kh).astype(k_ref.dtype)


def _mla_prep(p_mla, g_cq, g_ckv, wuq, wuk, wuv, gq, gk, cc, sa, sb):
    T, PW = p_mla.shape
    tt = 512
    QW = MLA_HEADS * MLA_QK_PAD
    VW = MLA_HEADS * MLA_V_DIM
    row = lambda i: (0, 0)
    tab = pl.BlockSpec((tt, LANE), lambda i: (i, 0))
    return pl.pallas_call(
        _mla_prep_kernel,
        grid=(T // tt,),
        in_specs=[pl.BlockSpec((tt, PW), lambda i: (i, 0)),
                  pl.BlockSpec((1, MLA_Q_LORA), row),
                  pl.BlockSpec((1, MLA_KV_LORA), row),
                  pl.BlockSpec((MLA_Q_LORA, QW), row),
                  pl.BlockSpec((MLA_KV_LORA, MLA_HEADS * MLA_NOPE_DIM), row),
                  pl.BlockSpec((MLA_KV_LORA, VW), row),
                  pl.BlockSpec((1, MLA_QK_PAD), row),
                  pl.BlockSpec((1, MLA_QK_PAD), row),
                  tab, tab, tab],
        out_specs=[pl.BlockSpec((tt, QW), lambda i: (i, 0)),
                   pl.BlockSpec((tt, QW), lambda i: (i, 0)),
                   pl.BlockSpec((tt, VW), lambda i: (i, 0))],
        out_shape=[jax.ShapeDtypeStruct((T, QW), BF16),
                   jax.ShapeDtypeStruct((T, QW), BF16),
                   jax.ShapeDtypeStruct((T, VW), BF16)],
        compiler_params=_params("arbitrary"),
        name="mla_prep",
    )(p_mla, g_cq, g_ckv, wuq, wuk, wuv, gq, gk, cc, sa, sb)


def _attn_kernel(q_ref, k_ref, v_ref, o_ref, *, blk):
    i = pl.program_id(1)
    q = q_ref[...]

    def step(j, carry, masked):
        m, l, acc = carry
        start = pl.multiple_of(j * blk, blk)
        kb = k_ref[pl.ds(start, blk), :]
        vb = v_ref[pl.ds(start, blk), :]
        s = lax.dot_general(q, kb, (((1,), (1,)), ((), ())), preferred_element_type=F32)
        if masked:
            row = lax.broadcasted_iota(jnp.int32, (blk, blk), 0)
            col = lax.broadcasted_iota(jnp.int32, (blk, blk), 1)
            s = jnp.where(col <= row, s, -jnp.inf)
        m_new = jnp.maximum(m, jnp.max(s, axis=-1, keepdims=True))
        alpha = jnp.exp(m - m_new)
        p = jnp.exp(s - m_new)
        l = alpha * l + jnp.sum(p, axis=-1, keepdims=True)
        acc = alpha * acc + jnp.dot(p.astype(BF16), vb, preferred_element_type=F32)
        return m_new, l, acc

    init = (jnp.full((blk, 1), -jnp.inf, F32), jnp.zeros((blk, 1), F32),
            jnp.zeros((blk, MLA_V_DIM), F32))
    carry = lax.fori_loop(0, i, lambda j, c: step(j, c, False), init)
    _, l, acc = step(i, carry, True)
    o_ref[...] = (acc / l).astype(o_ref.dtype)


def _causal_attention(q, k, v):
    T = q.shape[0]
    blk = 512
    return pl.pallas_call(
        functools.partial(_attn_kernel, blk=blk),
        grid=(MLA_HEADS, T // blk),
        in_specs=[pl.BlockSpec((blk, MLA_QK_PAD), lambda h, i: (i, h)),
                  pl.BlockSpec((T, MLA_QK_PAD), lambda h, i: (0, h)),
                  pl.BlockSpec((T, MLA_V_DIM), lambda h, i: (0, h))],
        out_specs=pl.BlockSpec((blk, MLA_V_DIM), lambda h, i: (i, h)),
        out_shape=jax.ShapeDtypeStruct((T, MLA_HEADS * MLA_V_DIM), BF16),
        compiler_params=_params("arbitrary", "arbitrary"),
        name="mla_attention",
    )(q, k, v)


_RET_LOG_GAMMA = np.log1p(-np.exp(np.linspace(math.log(1.0 / 32), math.log(1.0 / 512),
                                              RET_HEADS, dtype=np.float32))).astype(np.float32)


def _ret_kernel(q_ref, k_ref, v_ref, g_ref, cc_ref, ss_ref, gn_ref, o_ref, st_ref):
    i = pl.program_id(0)
    C, Dh = RET_CHUNK, RET_HEAD_DIM

    @pl.when(i == 0)
    def _():
        st_ref[...] = jnp.zeros_like(st_ref)

    cc, ss = cc_ref[...], ss_ref[...]
    row = lax.broadcasted_iota(jnp.int32, (C, C), 0)
    col = lax.broadcasted_iota(jnp.int32, (C, C), 1)
    rel = (row - col).astype(F32)
    idx = lax.broadcasted_iota(jnp.int32, (C, 1), 0).astype(F32)
    for h in range(RET_HEADS):
        lg = float(_RET_LOG_GAMMA[h])
        sl = slice(h * Dh, (h + 1) * Dh)
        q = q_ref[:, sl]
        k = k_ref[:, sl]
        v = v_ref[:, sl]
        q = q * cc + pltpu.roll(q, Dh // 2, 1) * ss
        k = (k * cc + pltpu.roll(k, Dh // 2, 1) * ss) * (Dh ** -0.5)
        decay_in = jnp.where(rel >= 0, jnp.exp(lg * jnp.maximum(rel, 0.0)), 0.0)
        q_dec = jnp.exp(lg * (idx + 1.0))
        k_dec = jnp.exp(lg * (C - 1.0 - idx))
        chunk_dec = math.exp(lg * C)
        inner = lax.dot_general(q.astype(BF16), k.astype(BF16), (((1,), (1,)), ((), ())),
                                preferred_element_type=F32) * decay_in
        st = st_ref[h]
        o = _bdot(inner, v) + _bdot(q * q_dec, st)
        st_ref[h] = chunk_dec * st + lax.dot_general(
            (k * k_dec).astype(BF16), v.astype(BF16), (((0,), (0,)), ((), ())),
            preferred_element_type=F32)
        oc = o - jnp.mean(o, axis=-1, keepdims=True)
        on = oc * lax.rsqrt(jnp.mean(oc * oc, axis=-1, keepdims=True) + GN_EPS) * gn_ref[:, sl]
        g = g_ref[:, sl]
        o_ref[:, sl] = (g * _sigmoid(g) * on).astype(o_ref.dtype)


def _retention_branch(p_ret, cc, ss, g_norm):
    T = p_ret.shape[0]
    C, W = RET_CHUNK, RET_WIDTH
    blk = lambda n: pl.BlockSpec((C, W), functools.partial(lambda i, n: (i, n), n=n))
    return pl.pallas_call(
        _ret_kernel,
        grid=(T // C,),
        in_specs=[blk(0), blk(1), blk(2), blk(3),
                  pl.BlockSpec((C, LANE), lambda i: (i, 0)),
                  pl.BlockSpec((C, LANE), lambda i: (i, 0)),
                  pl.BlockSpec((1, W), lambda i: (0, 0))],
        out_specs=pl.BlockSpec((C, W), lambda i: (i, 0)),
        out_shape=jax.ShapeDtypeStruct((T, W), BF16),
        scratch_shapes=[pltpu.VMEM((RET_HEADS, RET_HEAD_DIM, RET_HEAD_DIM), F32)],
        compiler_params=_params("arbitrary"),
        name="retention",
    )(p_ret, p_ret, p_ret, p_ret, cc, ss, g_norm)


RWKV_PAD_COLS = 3 * RWKV_WIDTH + 3 * LANE


def _rwkv_prep_kernel(p_ref, mu_ref, w0_ref, a0_ref, wup_ref, aup_ref, gup_ref, kk_ref, ka_ref,
                      rk_ref, ones_ref, tri_ref,
                      kp_ref, rp_ref, kn_ref, bn_ref, v_ref, g_ref, bonus_ref, pc_ref, buf):
    i = pl.program_id(0)
    tt = p_ref.shape[0]
    W = RWKV_WIDTH
    C = RWKV_CHUNK

    @pl.when(i == 0)
    def _():
        buf[0:SUBLANE, :] = jnp.zeros((SUBLANE, RWKV_PAD_COLS), F32)

    p = p_ref[...]
    buf[SUBLANE:SUBLANE + tt, :] = p
    prev = buf[pl.ds(SUBLANE - 1, tt), :]
    buf[0:SUBLANE, :] = p_ref[tt - SUBLANE:tt, :]
    xs = p + (prev - p) * mu_ref[...]
    r = xs[:, 0:W]
    k = xs[:, W:2 * W]
    v = xs[:, 2 * W:3 * W]
    wd = xs[:, 3 * W:3 * W + LANE]
    ad = xs[:, 3 * W + LANE:3 * W + 2 * LANE]
    gd = xs[:, 3 * W + 2 * LANE:3 * W + 3 * LANE]

    w_log = -_softplus(-(w0_ref[...] + _fdot(jnp.tanh(wd), wup_ref[...]))) - 0.5
    logw = -jnp.exp(w_log)
    a = _sigmoid(a0_ref[...] + _fdot(ad, aup_ref[...]))
    g_ref[...] = _fdot(_sigmoid(gd), gup_ref[...])

    ones_bd = ones_ref[...]
    kk = k * kk_ref[...]
    kk = kk / jnp.maximum(jnp.sqrt(_fdot(kk * kk, ones_bd)), 1e-12)
    kt = k * (1.0 + (a - 1.0) * ka_ref[...])
    bonus_ref[...] = _fdot(r * kt * rk_ref[...], ones_bd) * v
    v_ref[...] = v

    cum = _fdot(tri_ref[...], logw)
    e_neg = jnp.exp(-cum)
    kp_ref[...] = kk * jnp.exp(cum - logw)
    rp_ref[...] = r * jnp.exp(cum)
    kn_ref[...] = kt * e_neg
    bn_ref[...] = (kk * a) * e_neg
    pc_ref[...] = jnp.exp(jnp.sum(logw.reshape(tt // C, C, W), axis=1))


def _rwkv_prep(p_rw, mu, w0, a0, wup, aup, gup, k_k, k_a, r_k):
    T = p_rw.shape[0]
    W, C = RWKV_WIDTH, RWKV_CHUNK
    tt = 512
    hid = np.arange(W) // RWKV_HEAD_DIM
    ones_bd = jnp.asarray((hid[:, None] == hid[None, :]).astype(np.float32))
    tid = np.arange(tt)
    tri = jnp.asarray(((tid[:, None] // C == tid[None, :] // C)
                       & (tid[:, None] >= tid[None, :])).astype(np.float32))
    row = lambda i: (0, 0)
    vec = pl.BlockSpec((1, W), row)
    lora = pl.BlockSpec((LANE, W), row)
    out = pl.BlockSpec((tt, W), lambda i: (i, 0))
    shp = jax.ShapeDtypeStruct((T, W), F32)
    return pl.pallas_call(
        _rwkv_prep_kernel,
        grid=(T // tt,),
        in_specs=[pl.BlockSpec((tt, RWKV_PAD_COLS), lambda i: (i, 0)),
                  pl.BlockSpec((1, RWKV_PAD_COLS), row),
                  vec, vec, lora, lora, lora, vec, vec, vec,
                  pl.BlockSpec((W, W), row),
                  pl.BlockSpec((tt, tt), row)],
        out_specs=[out] * 7 + [pl.BlockSpec((tt // C, W), lambda i: (i, 0))],
        out_shape=[shp] * 7 + [jax.ShapeDtypeStruct((T // C, W), F32)],
        scratch_shapes=[pltpu.VMEM((tt + SUBLANE, RWKV_PAD_COLS), F32)],
        compiler_params=_params("arbitrary"),
        name="rwkv_prep",
    )(p_rw, mu, w0, a0, wup, aup, gup, k_k, k_a, r_k, ones_bd, tri)


def _bmm(a, b, dims):
    return lax.dot_general(a.astype(BF16), b.astype(BF16), dims, preferred_element_type=F32)


_NN = (((2,), (1,)), ((0,), (0,)))
_NT = (((2,), (2,)), ((0,), (0,)))
_TN = (((1,), (1,)), ((0,), (0,)))


def _rwkv_rec_kernel(kp_ref, rp_ref, kn_ref, bn_ref, v_ref, pc_ref, gn_ref, o_ref, s_ref,
                     *, nchunk):
    C = RWKV_CHUNK

    @pl.when(pl.program_id(0) == 0)
    def _():
        s_ref[...] = jnp.zeros_like(s_ref)

    row = lax.broadcasted_iota(jnp.int32, (C, C), 0)
    col = lax.broadcasted_iota(jnp.int32, (C, C), 1)
    strict = row > col
    incl = row >= col
    same16 = (row // 16) == (col // 16)
    same32 = (row // 32) == (col // 32)
    m_base = jnp.where(strict, jnp.where(same16, 1.0, 0.0), 0.0)
    m_l32 = jnp.where(strict, jnp.where(same32, 1.0, 0.0), 0.0) - m_base
    m_l64 = jnp.where(strict, 1.0, 0.0) - m_base - m_l32
    eye = jnp.where(row == col, 1.0, 0.0)

    def chunk(c, carry):
        sl = pl.ds(pl.multiple_of(c * C, C), C)
        kp = kp_ref[:, sl, :]
        rp = rp_ref[:, sl, :]
        kn = kn_ref[:, sl, :]
        bn = bn_ref[:, sl, :]
        v = v_ref[:, sl, :]
        pc = pc_ref[:, pl.ds(c, 1), :]
        s0 = s_ref[...]
        q2 = jnp.concatenate([kp, rp], axis=1)
        a_kn = _bmm(q2, kn, _NT)
        a_bn = _bmm(q2, bn, _NT)
        a_k = jnp.where(strict, a_kn[:, :C], 0.0)
        a_rk = jnp.where(incl, a_kn[:, C:], 0.0)
        a_b = a_bn[:, :C]
        a_rb = jnp.where(incl, a_bn[:, C:], 0.0)

        n1 = -(a_b * m_base)
        inv = eye + n1
        n2 = _bmm(n1, n1, _NN)
        inv = inv + _bmm(inv, n2, _NN)
        n4 = _bmm(n2, n2, _NN)
        inv = inv + _bmm(inv, n4, _NN)
        n8 = _bmm(n4, n4, _NN)
        inv = inv + _bmm(inv, n8, _NN)
        for msk in (m_l32, m_l64):
            inv = inv - _bmm(inv, _bmm(a_b * msk, inv, _NN), _NN)

        av = _bmm(jnp.concatenate([a_k, a_rk], axis=1), v, _NN)
        x1 = _bmm(q2, s0, _NT)
        u = _bmm(inv, x1[:, :C] + av[:, :C], _NN)
        o = x1[:, C:] + av[:, C:] - _bmm(a_rb, u, _NN)
        k2 = jnp.concatenate([kn * pc, bn * pc], axis=1)
        vu = jnp.concatenate([v, -u], axis=1)
        s_ref[...] = s0 * pc + _bmm(vu, k2, _TN)

        oc = o - jnp.mean(o, axis=-1, keepdims=True)
        o_ref[:, sl, :] = (oc * lax.rsqrt(jnp.mean(oc * oc, axis=-1, keepdims=True) + GN_EPS)
                           * gn_ref[...])
        return carry

    lax.fori_loop(0, nchunk, chunk, 0)


def _rwkv_recurrence(kp, rp, kn, bn, v, pc, gn):
    H, T, N = kp.shape
    C = RWKV_CHUNK
    tb = 512
    nchunk = tb // C
    blk = pl.BlockSpec((H, tb, N), lambda i: (0, i, 0))
    return pl.pallas_call(
        functools.partial(_rwkv_rec_kernel, nchunk=nchunk),
        grid=(T // tb,),
        in_specs=[blk, blk, blk, blk, blk,
                  pl.BlockSpec((H, nchunk, N), lambda i: (0, i, 0)),
                  pl.BlockSpec((H, 1, N), lambda i: (0, 0, 0))],
        out_specs=blk,
        out_shape=jax.ShapeDtypeStruct((H, T, N), F32),
        scratch_shapes=[pltpu.VMEM((H, N, N), F32)],
        compiler_params=_params("arbitrary"),
        name="rwkv_recurrence",
    )(kp, rp, kn, bn, v, pc, gn)


def _rwkv_post_kernel(o_ref, bonus_ref, g_ref, y_ref):
    y_ref[...] = ((o_ref[...] + bonus_ref[...]) * g_ref[...]).astype(y_ref.dtype)


def _rwkv_post(o, bonus, g):
    T, W = o.shape
    tt = 1024
    blk = pl.BlockSpec((tt, W), lambda i: (i, 0))
    return pl.pallas_call(
        _rwkv_post_kernel,
        grid=(T // tt,),
        in_specs=[blk, blk, blk],
        out_specs=blk,
        out_shape=jax.ShapeDtypeStruct((T, W), BF16),
        compiler_params=_params("arbitrary"),
        name="rwkv_post",
    )(o, bonus, g)


def _to_heads(x):
    T = x.shape[0]
    return x.reshape(T, RWKV_HEADS, RWKV_HEAD_DIM).transpose(1, 0, 2)


def _rwkv_branch(p_rw, mu, w0, a0, wup, aup, gup, k_k, k_a, r_k, g_norm):
    T = p_rw.shape[0]
    kp, rp, kn, bn, v, g, bonus, pc = _rwkv_prep(p_rw, mu, w0, a0, wup, aup, gup, k_k, k_a, r_k)
    o = _rwkv_recurrence(_to_heads(kp), _to_heads(rp), _to_heads(kn), _to_heads(bn), _to_heads(v),
                         _to_heads(pc), g_norm.reshape(RWKV_HEADS, 1, RWKV_HEAD_DIM))
    o = o.transpose(1, 0, 2).reshape(T, RWKV_WIDTH)
    return _rwkv_post(o, bonus, g)


def _pad_cols(w, n):
    return jnp.pad(w, ((0, 0), (0, n - w.shape[1])))


def _block_diag(w):
    n, bi, bj = w.shape
    eye = jnp.eye(n, dtype=w.dtype)
    return (eye[:, None, :, None] * w[:, :, None, :]).reshape(n * bi, n * bj)


def _rope_tables(positions):
    pos = positions.astype(F32).reshape(-1, 1)
    T = pos.shape[0]

    def cs(dim):
        inv = 1.0 / (ROPE_BASE ** (jnp.arange(0, dim, 2, dtype=F32) / dim))
        ang = pos * inv
        return jnp.cos(ang), jnp.sin(ang)

    cm, sm = cs(MLA_ROPE_DIM)
    z32 = jnp.zeros((T, MLA_ROPE_DIM // 2), F32)
    z64 = jnp.zeros((T, LANE - MLA_ROPE_DIM), F32)
    mla = (jnp.concatenate([cm, cm, z64], axis=1),
           jnp.concatenate([-sm, z32, z64], axis=1),
           jnp.concatenate([z32, sm, z64], axis=1))
    cr, sr = cs(RET_HEAD_DIM)
    ret = (jnp.concatenate([cr, cr], axis=1), jnp.concatenate([-sr, sr], axis=1))
    return mla, ret


def _mla_weights(w_uq, w_ukv, g_qn, g_kn):
    wq = w_uq.reshape(MLA_Q_LORA, MLA_HEADS, MLA_QK_DIM)
    wq = jnp.pad(wq, ((0, 0), (0, 0), (0, MLA_QK_PAD - MLA_QK_DIM)))
    wq = wq.reshape(MLA_Q_LORA, MLA_HEADS * MLA_QK_PAD)
    wkv = w_ukv.reshape(MLA_KV_LORA, MLA_HEADS, MLA_NOPE_DIM + MLA_V_DIM)
    wk = wkv[:, :, :MLA_NOPE_DIM].reshape(MLA_KV_LORA, MLA_HEADS * MLA_NOPE_DIM)
    wv = wkv[:, :, MLA_NOPE_DIM:].reshape(MLA_KV_LORA, MLA_HEADS * MLA_V_DIM)
    pad = MLA_QK_PAD - MLA_QK_DIM
    gq = jnp.pad(g_qn, (0, pad)).reshape(1, MLA_QK_PAD)
    gk = jnp.pad(g_kn, (0, pad)).reshape(1, MLA_QK_PAD)
    return wq.astype(BF16), wk.astype(BF16), wv.astype(BF16), gq, gk


def _pad_rows(w, n):
    return jnp.pad(w, ((0, n - w.shape[0]), (0, 0)))


def kernel(x, c, positions, ada_w, ada_b, norm_mix, norm_ffn, w_in, conv_w, conv_b, lru_wr, lru_br, lru_wi, lru_bi, lru_lam, mla_g_cq, mla_g_ckv, mla_w_uq, mla_w_ukv, mla_g_qn, mla_g_kn, ret_g_norm, rwkv_mu, rwkv_w0, rwkv_w_up, rwkv_a0, rwkv_a_up, rwkv_g_up, rwkv_k_k, rwkv_k_a, rwkv_r_k, rwkv_g_norm, w_branch, w_out, ffn_w_in, ffn_w_out):
    B, T, D = x.shape
    assert B == 1 and D == D_MODEL
    depth = ada_w.shape[0]
    xt = x.reshape(T, D)
    mod_all = _modulation(c, ada_w, ada_b)
    (cc_m, sa_m, sb_m), (cc_r, ss_r) = _rope_tables(positions)

    o_a = GATE_COLS
    o_cq = o_a + 2 * LRU_WIDTH
    o_ckv = o_cq + MLA_Q_LORA
    o_kr = o_ckv + MLA_KV_LORA
    o_ret = o_kr + MLA_ROPE_DIM
    o_rw = o_ret + 4 * RET_WIDTH
    o_lora = o_rw + 3 * RWKV_WIDTH
    W = RWKV_WIDTH

    for l in range(depth):
        mod = mod_all[l]
        wl = w_in[l]
        w_gate = wl[:, :GATE_COLS].astype(BF16)
        w_a = wl[:, o_a:o_cq].astype(BF16)
        w_mla = _pad_cols(wl[:, o_cq:o_ret], MLA_Q_LORA + MLA_KV_LORA + LANE).astype(BF16)
        w_ret = wl[:, o_ret:o_rw].astype(BF16)
        dl, al = RWKV_DECAY_LORA, RWKV_AAA_LORA
        w_rw = jnp.concatenate([wl[:, o_rw:o_lora],
                                _pad_cols(wl[:, o_lora:o_lora + dl], LANE),
                                _pad_cols(wl[:, o_lora + dl:o_lora + dl + al], LANE),
                                wl[:, o_lora + dl + al:]], axis=1).astype(BF16)
        mu = rwkv_mu[l]
        mu_p = jnp.concatenate([mu[:3 * W], jnp.pad(mu[3 * W:3 * W + dl], (0, LANE - dl)),
                                jnp.pad(mu[3 * W + dl:3 * W + dl + al], (0, LANE - al)),
                                mu[3 * W + dl + al:]]).reshape(1, RWKV_PAD_COLS)

        h = _mod_norm(xt, norm_mix[l].reshape(1, D), mod, 0)
        p_a = _matmul(h, w_a, F32, "in_proj_lru")
        p_mla = _matmul(h, w_mla, F32, "in_proj_mla")
        p_ret = _matmul(h, w_ret, F32, "in_proj_ret")
        p_rw = _matmul(h, w_rw, F32, "in_proj_rwkv")

        y_a = _lru_branch(p_a, conv_w[l], conv_b[l].reshape(1, -1),
                          _block_diag(lru_wr[l]).astype(BF16), lru_br[l].reshape(1, -1),
                          _block_diag(lru_wi[l]).astype(BF16), lru_bi[l].reshape(1, -1),
                          lru_lam[l].reshape(1, -1))

        wq, wk, wv, gq, gk = _mla_weights(mla_w_uq[l], mla_w_ukv[l], mla_g_qn[l], mla_g_kn[l])
        q, k, v = _mla_prep(p_mla, mla_g_cq[l].reshape(1, -1), mla_g_ckv[l].reshape(1, -1),
                            wq, wk, wv, gq, gk, cc_m, sa_m, sb_m)
        y_b = _causal_attention(q, k, v)

        y_c = _retention_branch(p_ret, cc_r, ss_r, ret_g_norm[l].reshape(1, -1))

        y_d = _rwkv_branch(p_rw, mu_p, rwkv_w0[l].reshape(1, -1), rwkv_a0[l].reshape(1, -1),
                           _pad_rows(rwkv_w_up[l], LANE), _pad_rows(rwkv_a_up[l], LANE),
                           rwkv_g_up[l], rwkv_k_k[l].reshape(1, -1), rwkv_k_a[l].reshape(1, -1),
                           rwkv_r_k[l].reshape(1, -1), rwkv_g_norm[l])

        merged = _gated_merge(h, w_gate, (y_a, y_b, y_c, y_d), w_branch[l].astype(BF16))
        xt = _matmul_gated_residual(merged, w_out[l].astype(BF16), xt, mod, 2, "out_proj")

        h2 = _mod_norm(xt, norm_ffn[l].reshape(1, D), mod, 3)
        act = _swiglu_in(h2, ffn_w_in[l].astype(BF16))
        xt = _matmul_gated_residual(act, ffn_w_out[l].astype(BF16), xt, mod, 5, "ffn_out")
    return xt.reshape(B, T, D)
```

```python
import functools
import math

import numpy as np
import jax
import jax.numpy as jnp
from jax import lax
from jax.experimental import pallas as pl
from jax.experimental.pallas import tpu as pltpu

F32 = jnp.float32
BF16 = jnp.bfloat16
HIGHEST = lax.Precision.HIGHEST

D_MODEL = 2048
N_BRANCH = 4
BRANCH_WIDTH = D_MODEL // N_BRANCH
NORM_EPS = 1e-6
GN_EPS = 1e-5
ROPE_BASE = 10000.0
LRU_WIDTH = BRANCH_WIDTH
LRU_C = 8.0
CONV_WIDTH = 4
MLA_HEADS = 4
MLA_NOPE_DIM = 128
MLA_ROPE_DIM = 64
MLA_V_DIM = 128
MLA_QK_DIM = MLA_NOPE_DIM + MLA_ROPE_DIM
MLA_QK_PAD = 256
MLA_Q_LORA = 384
MLA_KV_LORA = 256
RET_HEADS = 4
RET_HEAD_DIM = 128
RET_WIDTH = RET_HEADS * RET_HEAD_DIM
RET_CHUNK = 128
RWKV_HEAD_DIM = 64
RWKV_HEADS = 8
RWKV_WIDTH = RWKV_HEADS * RWKV_HEAD_DIM
RWKV_DECAY_LORA = 64
RWKV_AAA_LORA = 64
RWKV_GATE_LORA = 128
RWKV_CHUNK = 64
FFN_HIDDEN = 5632
GATE_COLS = N_BRANCH * D_MODEL
LANE = 128
SUBLANE = 8
VMEM_LIMIT_BYTES = 56 * 1024 * 1024


def _params(*sem):
    return pltpu.CompilerParams(dimension_semantics=sem, vmem_limit_bytes=VMEM_LIMIT_BYTES)


def _sigmoid(x):
    return 1.0 / (1.0 + jnp.exp(-x))


def _softplus(x):
    return jnp.maximum(x, 0.0) + jnp.log(1.0 + jnp.exp(-jnp.abs(x)))


def _bdot(a, b):
    return jnp.dot(a.astype(BF16), b.astype(BF16), preferred_element_type=F32)


def _fdot(a, b):
    return jnp.dot(a, b, preferred_element_type=F32, precision=HIGHEST)


def _mod_kernel(c_ref, w_ref, b_ref, o_ref):
    c = c_ref[...]
    ca = c * _sigmoid(c)
    o_ref[0] = jnp.sum(ca * w_ref[0], axis=0, keepdims=True) + b_ref[0]


def _modulation(c, ada_w, ada_b):
    L, D, N = ada_w.shape
    tn = 1024
    return pl.pallas_call(
        _mod_kernel,
        grid=(L, N // tn),
        in_specs=[pl.BlockSpec((D, 1), lambda l, j: (0, 0)),
                  pl.BlockSpec((1, D, tn), lambda l, j: (l, 0, j)),
                  pl.BlockSpec((1, 1, tn), lambda l, j: (l, 0, j))],
        out_specs=pl.BlockSpec((1, 1, tn), lambda l, j: (l, 0, j)),
        out_shape=jax.ShapeDtypeStruct((L, 1, N), F32),
        compiler_params=_params("arbitrary", "arbitrary"),
        name="adaln_mod",
    )(c.reshape(D, 1), ada_w, ada_b.reshape(L, 1, N))


def _norm_kernel(x_ref, g_ref, sh_ref, sc_ref, o_ref):
    x = x_ref[...]
    ms = jnp.mean(x * x, axis=-1, keepdims=True)
    y = x * lax.rsqrt(ms + NORM_EPS) * g_ref[...]
    o_ref[...] = (y * (1.0 + sc_ref[...]) + sh_ref[...]).astype(o_ref.dtype)


def _mod_norm(x, g, mod, shift_idx):
    T, D = x.shape
    tm = 512
    return pl.pallas_call(
        _norm_kernel,
        grid=(T // tm,),
        in_specs=[pl.BlockSpec((tm, D), lambda i: (i, 0)),
                  pl.BlockSpec((1, D), lambda i: (0, 0)),
                  pl.BlockSpec((1, D), lambda i: (0, shift_idx)),
                  pl.BlockSpec((1, D), lambda i: (0, shift_idx + 1))],
        out_specs=pl.BlockSpec((tm, D), lambda i: (i, 0)),
        out_shape=jax.ShapeDtypeStruct((T, D), BF16),
        compiler_params=_params("arbitrary"),
        name="mod_norm",
    )(x, g, mod, mod)


def _mm_kernel(a_ref, b_ref, o_ref):
    o_ref[...] = jnp.dot(a_ref[...], b_ref[...], preferred_element_type=F32).astype(o_ref.dtype)


def _matmul(a, b, out_dtype, name):
    M, K = a.shape
    N = b.shape[1]
    tm = 512
    tn = N if N <= 2048 else 512
    return pl.pallas_call(
        _mm_kernel,
        grid=(M // tm, N // tn),
        in_specs=[pl.BlockSpec((tm, K), lambda i, j: (i, 0)),
                  pl.BlockSpec((K, tn), lambda i, j: (0, j))],
        out_specs=pl.BlockSpec((tm, tn), lambda i, j: (i, j)),
        out_shape=jax.ShapeDtypeStruct((M, N), out_dtype),
        compiler_params=_params("arbitrary", "arbitrary"),
        name=name,
    )(a, b)


def _first_row_tile():
    return pl.program_id(1) == 0


def _ws_mm_kernel(a_ref, w_ref, o_ref, wb_ref):
    @pl.when(_first_row_tile())
    def _():
        wb_ref[...] = w_ref[...].astype(BF16)

    o_ref[...] = jnp.dot(a_ref[...], wb_ref[...],
                         preferred_element_type=F32).astype(o_ref.dtype)


def _ws_matmul(a, w, l, col0, ncols, tn, name):
    M, K = a.shape
    tm = 1024
    jb = col0 // tn
    assert col0 % tn == 0 and ncols % tn == 0
    return pl.pallas_call(
        _ws_mm_kernel,
        grid=(ncols // tn, M // tm),
        in_specs=[pl.BlockSpec((tm, K), lambda j, i: (i, 0)),
                  pl.BlockSpec((None, K, tn), lambda j, i: (l, 0, jb + j))],
        out_specs=pl.BlockSpec((tm, tn), lambda j, i: (i, j)),
        out_shape=jax.ShapeDtypeStruct((M, ncols), F32),
        scratch_shapes=[pltpu.VMEM((K, tn), BF16)],
        compiler_params=_params("arbitrary", "arbitrary"),
        name=name,
    )(a, w)


def _ws_mm_res_kernel(a_ref, w_ref, x_ref, g_ref, o_ref, wb_ref):
    @pl.when(_first_row_tile())
    def _():
        wb_ref[...] = w_ref[...].astype(BF16)

    acc = jnp.dot(a_ref[...], wb_ref[...], preferred_element_type=F32)
    o_ref[...] = x_ref[...] + g_ref[...] * acc


def _matmul_gated_residual(a, w, l, x, mod, gate_idx, tm, name):
    M, K = a.shape
    N = w.shape[2]
    tn = 512
    nj = N // tn
    return pl.pallas_call(
        _ws_mm_res_kernel,
        grid=(nj, M // tm),
        in_specs=[pl.BlockSpec((tm, K), lambda j, i: (i, 0)),
                  pl.BlockSpec((None, K, tn), lambda j, i: (l, 0, j)),
                  pl.BlockSpec((tm, tn), lambda j, i: (i, j)),
                  pl.BlockSpec((1, tn), lambda j, i: (0, gate_idx * nj + j))],
        out_specs=pl.BlockSpec((tm, tn), lambda j, i: (i, j)),
        out_shape=jax.ShapeDtypeStruct((M, N), F32),
        scratch_shapes=[pltpu.VMEM((K, tn), BF16)],
        compiler_params=_params("arbitrary", "arbitrary"),
        name=name,
    )(a, w, x, mod)


def _swiglu_kernel(a_ref, wg_ref, wv_ref, o_ref, wgb_ref, wvb_ref):
    @pl.when(_first_row_tile())
    def _():
        wgb_ref[...] = wg_ref[...].astype(BF16)
        wvb_ref[...] = wv_ref[...].astype(BF16)

    a = a_ref[...]
    ug = jnp.dot(a, wgb_ref[...], preferred_element_type=F32)
    uv = jnp.dot(a, wvb_ref[...], preferred_element_type=F32)
    o_ref[...] = (ug * _sigmoid(ug) * uv).astype(o_ref.dtype)


def _swiglu_in(a, w, l):
    M, K = a.shape
    H = w.shape[2] // 2
    tm, tn = 1024, 512
    nj = H // tn
    return pl.pallas_call(
        _swiglu_kernel,
        grid=(nj, M // tm),
        in_specs=[pl.BlockSpec((tm, K), lambda j, i: (i, 0)),
                  pl.BlockSpec((None, K, tn), lambda j, i: (l, 0, j)),
                  pl.BlockSpec((None, K, tn), lambda j, i: (l, 0, nj + j))],
        out_specs=pl.BlockSpec((tm, tn), lambda j, i: (i, j)),
        out_shape=jax.ShapeDtypeStruct((M, H), BF16),
        scratch_shapes=[pltpu.VMEM((K, tn), BF16), pltpu.VMEM((K, tn), BF16)],
        compiler_params=_params("arbitrary", "arbitrary"),
        name="ffn_in_swiglu",
    )(a, w, w)


def _merge_kernel(h_ref, g0_ref, g1_ref, g2_ref, g3_ref, y0_ref, y1_ref, y2_ref, y3_ref,
                  wb_ref, o_ref, gb_ref, wbb_ref):
    g_refs = (g0_ref, g1_ref, g2_ref, g3_ref)

    @pl.when(_first_row_tile())
    def _():
        for n in range(N_BRANCH):
            gb_ref[n] = g_refs[n][...].astype(BF16)
        wbb_ref[...] = wb_ref[...].astype(BF16)

    h = h_ref[...]
    acc = None
    for n, y_ref in enumerate((y0_ref, y1_ref, y2_ref, y3_ref)):
        logits = jnp.dot(h, gb_ref[n], preferred_element_type=F32)
        branch = jnp.dot(y_ref[...], wbb_ref[n], preferred_element_type=F32)
        term = _sigmoid(logits) * branch
        acc = term if acc is None else acc + term
    o_ref[...] = acc.astype(o_ref.dtype)


def _gated_merge(h, w_in, ys, w_branch, l):
    M, K = h.shape
    D = w_branch.shape[3]
    BW = w_branch.shape[2]
    tm, tn = 1024, 256
    nj = D // tn
    gate_specs = [pl.BlockSpec((None, K, tn),
                               functools.partial(lambda j, i, n: (l, 0, n * nj + j), n=n))
                  for n in range(N_BRANCH)]
    y_specs = [pl.BlockSpec((tm, BW), lambda j, i: (i, 0)) for _ in range(N_BRANCH)]
    return pl.pallas_call(
        _merge_kernel,
        grid=(nj, M // tm),
        in_specs=[pl.BlockSpec((tm, K), lambda j, i: (i, 0))] + gate_specs + y_specs
                 + [pl.BlockSpec((None, N_BRANCH, BW, tn), lambda j, i: (l, 0, 0, j))],
        out_specs=pl.BlockSpec((tm, tn), lambda j, i: (i, j)),
        out_shape=jax.ShapeDtypeStruct((M, D), BF16),
        scratch_shapes=[pltpu.VMEM((N_BRANCH, K, tn), BF16),
                        pltpu.VMEM((N_BRANCH, BW, tn), BF16)],
        compiler_params=_params("arbitrary", "arbitrary"),
        name="gated_merge",
    )(h, w_in, w_in, w_in, w_in, *ys, w_branch)


def _lru_kernel(ax_ref, ag_ref, cw_ref, cb_ref, wr_ref, br_ref, wi_ref, bi_ref, lam_ref,
                o_ref, xbuf, hcar):
    i = pl.program_id(0)
    tt = ax_ref.shape[0]

    @pl.when(i == 0)
    def _():
        xbuf[0:SUBLANE, :] = jnp.zeros((SUBLANE, LRU_WIDTH), F32)
        hcar[...] = jnp.zeros_like(hcar)

    xbuf[SUBLANE:SUBLANE + tt, :] = ax_ref[...]
    xa = cb_ref[...]
    for j in range(CONV_WIDTH):
        xa = xa + cw_ref[j:j + 1, :] * xbuf[pl.ds(SUBLANE - (CONV_WIDTH - 1) + j, tt), :]
    xbuf[0:SUBLANE, :] = ax_ref[tt - SUBLANE:tt, :]

    r = _sigmoid(_bdot(xa, wr_ref[...]) + br_ref[...])
    ig = _sigmoid(_bdot(xa, wi_ref[...]) + bi_ref[...])
    log_a = (-LRU_C * _softplus(-lam_ref[...])) * r
    a = jnp.exp(log_a)
    th = jnp.tanh(log_a)
    u = jnp.sqrt(-2.0 * th / (1.0 - th)) * (ig * xa)

    row = lax.broadcasted_iota(jnp.int32, (tt, LRU_WIDTH), 0)
    d = 1
    while d < tt:
        keep = row >= d
        u_s = jnp.where(keep, pltpu.roll(u, d, 0), 0.0)
        a_s = jnp.where(keep, pltpu.roll(a, d, 0), 1.0)
        u = u + a * u_s
        a = a * a_s
        d *= 2
    h = u + a * hcar[...]
    hcar[...] = h[tt - 1:tt, :]

    g = ag_ref[...]
    gelu = 0.5 * g * (1.0 + jnp.tanh(math.sqrt(2.0 / math.pi) * (g + 0.044715 * (g * g * g))))
    o_ref[...] = (h * gelu).astype(o_ref.dtype)


def _lru_branch(p_a, conv_w, conv_b, wr_bd, br, wi_bd, bi, lam):
    T = p_a.shape[0]
    W = LRU_WIDTH
    tt = 256
    row = lambda i: (0, 0)
    return pl.pallas_call(
        _lru_kernel,
        grid=(T // tt,),
        in_specs=[pl.BlockSpec((tt, W), lambda i: (i, 0)),
                  pl.BlockSpec((tt, W), lambda i: (i, 1)),
                  pl.BlockSpec((CONV_WIDTH, W), row),
                  pl.BlockSpec((1, W), row),
                  pl.BlockSpec((W, W), row),
                  pl.BlockSpec((1, W), row),
                  pl.BlockSpec((W, W), row),
                  pl.BlockSpec((1, W), row),
                  pl.BlockSpec((1, W), row)],
        out_specs=pl.BlockSpec((tt, W), lambda i: (i, 0)),
        out_shape=jax.ShapeDtypeStruct((T, W), BF16),
        scratch_shapes=[pltpu.VMEM((tt + SUBLANE, W), F32), pltpu.VMEM((1, W), F32)],
        compiler_params=_params("arbitrary"),
        name="rg_lru",
    )(p_a, p_a, conv_w, conv_b, wr_bd, br, wi_bd, bi, lam)


def _mla_prep_kernel(p_ref, gcq_ref, gckv_ref, wuq_ref, wuk_ref, wuv_ref, gq_ref, gk_ref,
                     cc_ref, sa_ref, sb_ref, q_ref, k_ref, v_ref):
    p = p_ref[...]
    cq = p[:, :MLA_Q_LORA]
    ckv = p[:, MLA_Q_LORA:MLA_Q_LORA + MLA_KV_LORA]
    kr = p[:, MLA_Q_LORA + MLA_KV_LORA:]
    lane = lax.broadcasted_iota(jnp.int32, kr.shape, 1)
    kr = jnp.where(lane < MLA_ROPE_DIM, kr, 0.0)

    def rms(x, g, n):
        return x * lax.rsqrt(jnp.sum(x * x, axis=-1, keepdims=True) * (1.0 / n) + NORM_EPS) * g

    cqn = rms(cq, gcq_ref[...], MLA_Q_LORA)
    ckvn = rms(ckv, gckv_ref[...], MLA_KV_LORA)
    q = _bdot(cqn, wuq_ref[...])
    kn = _bdot(ckvn, wuk_ref[...])
    v_ref[...] = _bdot(ckvn, wuv_ref[...]).astype(v_ref.dtype)
    cc, sa, sb = cc_ref[...], sa_ref[...], sb_ref[...]
    half = MLA_ROPE_DIM // 2

    def rope(x):
        x2 = x[:, LANE:]
        x2 = x2 * cc + pltpu.roll(x2, LANE - half, 1) * sa + pltpu.roll(x2, half, 1) * sb
        return jnp.concatenate([x[:, :LANE], x2], axis=-1)

    scale = MLA_QK_DIM ** -0.5 * math.log2(math.e)
    for h in range(MLA_HEADS):
        qh = rms(q[:, h * MLA_QK_PAD:(h + 1) * MLA_QK_PAD], gq_ref[...], MLA_QK_DIM)
        q_ref[:, h * MLA_QK_PAD:(h + 1) * MLA_QK_PAD] = (rope(qh) * scale).astype(q_ref.dtype)
        kh = jnp.concatenate([kn[:, h * MLA_NOPE_DIM:(h + 1) * MLA_NOPE_DIM], kr], axis=-1)
        kh = rms(kh, gk_ref[...], MLA_QK_DIM)
        k_ref[:, h * MLA_QK_PAD:(h + 1) * MLA_QK_PAD] = rope(kh).astype(k_ref.dtype)


def _mla_prep(p_mla, g_cq, g_ckv, wuq, wuk, wuv, gq, gk, cc, sa, sb):
    T, PW = p_mla.shape
    tt = 512
    QW = MLA_HEADS * MLA_QK_PAD
    VW = MLA_HEADS * MLA_V_DIM
    row = lambda i: (0, 0)
    tab = pl.BlockSpec((tt, LANE), lambda i: (i, 0))
    return pl.pallas_call(
        _mla_prep_kernel,
        grid=(T // tt,),
        in_specs=[pl.BlockSpec((tt, PW), lambda i: (i, 0)),
                  pl.BlockSpec((1, MLA_Q_LORA), row),
                  pl.BlockSpec((1, MLA_KV_LORA), row),
                  pl.BlockSpec((MLA_Q_LORA, QW), row),
                  pl.BlockSpec((MLA_KV_LORA, MLA_HEADS * MLA_NOPE_DIM), row),
                  pl.BlockSpec((MLA_KV_LORA, VW), row),
                  pl.BlockSpec((1, MLA_QK_PAD), row),
                  pl.BlockSpec((1, MLA_QK_PAD), row),
                  tab, tab, tab],
        out_specs=[pl.BlockSpec((tt, QW), lambda i: (i, 0)),
                   pl.BlockSpec((tt, QW), lambda i: (i, 0)),
                   pl.BlockSpec((tt, VW), lambda i: (i, 0))],
        out_shape=[jax.ShapeDtypeStruct((T, QW), BF16),
                   jax.ShapeDtypeStruct((T, QW), BF16),
                   jax.ShapeDtypeStruct((T, VW), BF16)],
        compiler_params=_params("arbitrary"),
        name="mla_prep",
    )(p_mla, g_cq, g_ckv, wuq, wuk, wuv, gq, gk, cc, sa, sb)


def _attn_kernel(q_ref, k_ref, v_ref, o_ref, *, blk):
    i = pl.program_id(1)
    q = q_ref[...]

    def step(j, carry, masked):
        m, l, acc = carry
        start = pl.multiple_of(j * blk, blk)
        kb = k_ref[pl.ds(start, blk), :]
        vb = v_ref[pl.ds(start, blk), :]
        s = lax.dot_general(q, kb, (((1,), (1,)), ((), ())), preferred_element_type=F32)
        if masked:
            row = lax.broadcasted_iota(jnp.int32, (blk, blk), 0)
            col = lax.broadcasted_iota(jnp.int32, (blk, blk), 1)
            s = jnp.where(col <= row, s, -jnp.inf)
        m_new = jnp.maximum(m, jnp.max(s, axis=-1, keepdims=True))
        alpha = jnp.exp2(m - m_new)
        p = jnp.exp2(s - m_new)
        l = alpha * l + jnp.sum(p, axis=-1, keepdims=True)
        acc = alpha * acc + jnp.dot(p.astype(BF16), vb, preferred_element_type=F32)
        return m_new, l, acc

    init = (jnp.full((blk, 1), -jnp.inf, F32), jnp.zeros((blk, 1), F32),
            jnp.zeros((blk, MLA_V_DIM), F32))
    carry = lax.fori_loop(0, i, lambda j, c: step(j, c, False), init)
    _, l, acc = step(i, carry, True)
    o_ref[...] = (acc / l).astype(o_ref.dtype)


def _causal_attention(q, k, v):
    T = q.shape[0]
    blk = 512
    return pl.pallas_call(
        functools.partial(_attn_kernel, blk=blk),
        grid=(MLA_HEADS, T // blk),
        in_specs=[pl.BlockSpec((blk, MLA_QK_PAD), lambda h, i: (i, h)),
                  pl.BlockSpec((T, MLA_QK_PAD), lambda h, i: (0, h)),
                  pl.BlockSpec((T, MLA_V_DIM), lambda h, i: (0, h))],
        out_specs=pl.BlockSpec((blk, MLA_V_DIM), lambda h, i: (i, h)),
        out_shape=jax.ShapeDtypeStruct((T, MLA_HEADS * MLA_V_DIM), BF16),
        compiler_params=_params("arbitrary", "arbitrary"),
        name="mla_attention",
    )(q, k, v)


_RET_LOG_GAMMA = np.log1p(-np.exp(np.linspace(math.log(1.0 / 32), math.log(1.0 / 512),
                                              RET_HEADS, dtype=np.float32))).astype(np.float32)


def _ret_kernel(q_ref, k_ref, v_ref, g_ref, cc_ref, ss_ref, gn_ref, o_ref, st_ref):
    i = pl.program_id(0)
    C, Dh = RET_CHUNK, RET_HEAD_DIM

    @pl.when(i == 0)
    def _():
        st_ref[...] = jnp.zeros_like(st_ref)

    cc, ss = cc_ref[...], ss_ref[...]
    row = lax.broadcasted_iota(jnp.int32, (C, C), 0)
    col = lax.broadcasted_iota(jnp.int32, (C, C), 1)
    rel = (row - col).astype(F32)
    idx = lax.broadcasted_iota(jnp.int32, (C, 1), 0).astype(F32)
    for h in range(RET_HEADS):
        lg = float(_RET_LOG_GAMMA[h])
        sl = slice(h * Dh, (h + 1) * Dh)
        q = q_ref[:, sl]
        k = k_ref[:, sl]
        v = v_ref[:, sl]
        q = q * cc + pltpu.roll(q, Dh // 2, 1) * ss
        k = (k * cc + pltpu.roll(k, Dh // 2, 1) * ss) * (Dh ** -0.5)
        decay_in = jnp.where(rel >= 0, jnp.exp(lg * jnp.maximum(rel, 0.0)), 0.0)
        q_dec = jnp.exp(lg * (idx + 1.0))
        k_dec = jnp.exp(lg * (C - 1.0 - idx))
        chunk_dec = math.exp(lg * C)
        inner = lax.dot_general(q.astype(BF16), k.astype(BF16), (((1,), (1,)), ((), ())),
                                preferred_element_type=F32) * decay_in
        st = st_ref[h]
        o = _bdot(inner, v) + _bdot(q * q_dec, st)
        st_ref[h] = chunk_dec * st + lax.dot_general(
            (k * k_dec).astype(BF16), v.astype(BF16), (((0,), (0,)), ((), ())),
            preferred_element_type=F32)
        oc = o - jnp.mean(o, axis=-1, keepdims=True)
        on = oc * lax.rsqrt(jnp.mean(oc * oc, axis=-1, keepdims=True) + GN_EPS) * gn_ref[:, sl]
        g = g_ref[:, sl]
        o_ref[:, sl] = (g * _sigmoid(g) * on).astype(o_ref.dtype)


def _retention_branch(p_ret, cc, ss, g_norm):
    T = p_ret.shape[0]
    C, W = RET_CHUNK, RET_WIDTH
    blk = lambda n: pl.BlockSpec((C, W), functools.partial(lambda i, n: (i, n), n=n))
    return pl.pallas_call(
        _ret_kernel,
        grid=(T // C,),
        in_specs=[blk(0), blk(1), blk(2), blk(3),
                  pl.BlockSpec((C, LANE), lambda i: (i, 0)),
                  pl.BlockSpec((C, LANE), lambda i: (i, 0)),
                  pl.BlockSpec((1, W), lambda i: (0, 0))],
        out_specs=pl.BlockSpec((C, W), lambda i: (i, 0)),
        out_shape=jax.ShapeDtypeStruct((T, W), BF16),
        scratch_shapes=[pltpu.VMEM((RET_HEADS, RET_HEAD_DIM, RET_HEAD_DIM), F32)],
        compiler_params=_params("arbitrary"),
        name="retention",
    )(p_ret, p_ret, p_ret, p_ret, cc, ss, g_norm)


RWKV_PAD_COLS = 3 * RWKV_WIDTH + 3 * LANE


def _split_dot(a, b_exact, terms):
    acc = None
    rem = a
    for _ in range(terms):
        piece = rem.astype(BF16)
        rem = rem - piece.astype(F32)
        d = jnp.dot(piece, b_exact, preferred_element_type=F32)
        acc = d if acc is None else acc + d
    return acc


def _rwkv_prep_kernel(p_ref, mu_ref, w0_ref, a0_ref, wup_ref, aup_ref, gup_ref, kk_ref, ka_ref,
                      rk_ref, ones_ref, tri_ref,
                      kp_ref, rp_ref, kn_ref, bn_ref, knp_ref, bnp_ref, v_ref, g_ref, bonus_ref,
                      pc_ref, buf):
    i = pl.program_id(0)
    tt = p_ref.shape[0]
    W = RWKV_WIDTH
    C = RWKV_CHUNK

    @pl.when(i == 0)
    def _():
        buf[0:SUBLANE, :] = jnp.zeros((SUBLANE, RWKV_PAD_COLS), F32)

    p = p_ref[...]
    buf[SUBLANE:SUBLANE + tt, :] = p
    prev = buf[pl.ds(SUBLANE - 1, tt), :]
    buf[0:SUBLANE, :] = p_ref[tt - SUBLANE:tt, :]
    xs = p + (prev - p) * mu_ref[...]
    r = xs[:, 0:W]
    k = xs[:, W:2 * W]
    v = xs[:, 2 * W:3 * W]
    wd = xs[:, 3 * W:3 * W + LANE]
    ad = xs[:, 3 * W + LANE:3 * W + 2 * LANE]
    gd = xs[:, 3 * W + 2 * LANE:3 * W + 3 * LANE]

    w_log = -_softplus(-(w0_ref[...] + _fdot(jnp.tanh(wd), wup_ref[...]))) - 0.5
    logw = -jnp.exp(w_log)
    a = _sigmoid(a0_ref[...] + _bdot(ad, aup_ref[...]))
    g_ref[...] = _bdot(_sigmoid(gd), gup_ref[...])

    ones_bd = ones_ref[...]
    kk = k * kk_ref[...]
    kk = kk / jnp.maximum(jnp.sqrt(_split_dot(kk * kk, ones_bd, 2)), 1e-12)
    kt = k * (1.0 + (a - 1.0) * ka_ref[...])
    bonus_ref[...] = _split_dot(r * kt * rk_ref[...], ones_bd, 2) * v
    v_ref[...] = v.astype(v_ref.dtype)

    tri = tri_ref[...]
    cum = jnp.concatenate([_split_dot_lhs_exact(tri, logw[c * C:(c + 1) * C], 3)
                           for c in range(tt // C)], axis=0)
    tot = jnp.sum(logw.reshape(tt // C, C, W), axis=1)
    pc_ref[...] = jnp.exp(tot)
    to_end = jnp.exp((tot[:, None, :] - cum.reshape(tt // C, C, W)).reshape(tt, W))
    e_neg = jnp.exp(-cum)
    beta = kk * a
    kp_ref[...] = (kk * jnp.exp(cum - logw)).astype(kp_ref.dtype)
    rp_ref[...] = (r * jnp.exp(cum)).astype(rp_ref.dtype)
    kn_ref[...] = (kt * e_neg).astype(kn_ref.dtype)
    bn_ref[...] = (beta * e_neg).astype(bn_ref.dtype)
    knp_ref[...] = (kt * to_end).astype(knp_ref.dtype)
    bnp_ref[...] = (beta * to_end).astype(bnp_ref.dtype)


def _split_dot_lhs_exact(a_exact, b, terms):
    acc = None
    rem = b
    for _ in range(terms):
        piece = rem.astype(BF16)
        rem = rem - piece.astype(F32)
        d = jnp.dot(a_exact, piece, preferred_element_type=F32)
        acc = d if acc is None else acc + d
    return acc


def _rwkv_prep(p_rw, mu, w0, a0, wup, aup, gup, k_k, k_a, r_k):
    T = p_rw.shape[0]
    W, C = RWKV_WIDTH, RWKV_CHUNK
    tt = 512
    hid = np.arange(W) // RWKV_HEAD_DIM
    ones_bd = jnp.asarray((hid[:, None] == hid[None, :]).astype(np.float32), dtype=BF16)
    tid = np.arange(C)
    tri = jnp.asarray((tid[:, None] >= tid[None, :]).astype(np.float32), dtype=BF16)
    row = lambda i: (0, 0)
    vec = pl.BlockSpec((1, W), row)
    lora = pl.BlockSpec((LANE, W), row)
    out = pl.BlockSpec((tt, W), lambda i: (i, 0))
    lo = jax.ShapeDtypeStruct((T, W), BF16)
    hi = jax.ShapeDtypeStruct((T, W), F32)
    return pl.pallas_call(
        _rwkv_prep_kernel,
        grid=(T // tt,),
        in_specs=[pl.BlockSpec((tt, RWKV_PAD_COLS), lambda i: (i, 0)),
                  pl.BlockSpec((1, RWKV_PAD_COLS), row),
                  vec, vec, lora, lora, lora, vec, vec, vec,
                  pl.BlockSpec((W, W), row),
                  pl.BlockSpec((C, C), row)],
        out_specs=[out] * 9 + [pl.BlockSpec((tt // C, W), lambda i: (i, 0))],
        out_shape=[lo] * 7 + [hi, hi, jax.ShapeDtypeStruct((T // C, W), F32)],
        scratch_shapes=[pltpu.VMEM((tt + SUBLANE, RWKV_PAD_COLS), F32)],
        compiler_params=_params("arbitrary"),
        name="rwkv_prep",
    )(p_rw, mu, w0, a0, wup, aup, gup, k_k, k_a, r_k, ones_bd, tri)


def _mm(a, b, dims):
    return lax.dot_general(a.astype(BF16), b.astype(BF16), dims, preferred_element_type=F32)


_NN = (((1,), (0,)), ((), ()))
_NT = (((1,), (1,)), ((), ()))
_TN = (((0,), (0,)), ((), ()))


def _rwkv_rec_kernel(kp_ref, rp_ref, kn_ref, bn_ref, knp_ref, bnp_ref, v_ref, pc_ref, gn_ref,
                     bonus_ref, g_ref, y_ref, s_ref, *, nchunk):
    C = RWKV_CHUNK
    P = 2 * C
    assert P == LANE

    @pl.when(pl.program_id(0) == 0)
    def _():
        s_ref[...] = jnp.zeros_like(s_ref)

    row = lax.broadcasted_iota(jnp.int32, (P, P), 0)
    col = lax.broadcasted_iota(jnp.int32, (P, P), 1)

    def blocks(n):
        return jnp.where((row // n) == (col // n), 1.0, 0.0)

    own = blocks(C)
    own_bf = own.astype(BF16)
    lower = jnp.where(row > col, 1.0, 0.0)
    m_strict = own * lower
    m_incl = own * jnp.where(row >= col, 1.0, 0.0)
    m_base = blocks(16) * lower
    m_l32 = blocks(32) * lower - m_base
    m_l64 = m_strict - m_base - m_l32
    eye = jnp.where(row == col, 1.0, 0.0)

    def stack(ref, sl, ln):
        x = ref[sl, ln]
        return jnp.concatenate([x, x], axis=0) * own_bf

    def chunk(c, carry):
        sl = pl.ds(pl.multiple_of(c * C, C), C)
        for p in range(RWKV_HEADS // 2):
            ln = slice(p * LANE, (p + 1) * LANE)
            kp, rp, kn, bn = (stack(r, sl, ln) for r in (kp_ref, rp_ref, kn_ref, bn_ref))
            knp, bnp, v = (stack(r, sl, ln) for r in (knp_ref, bnp_ref, v_ref))
            s0 = s_ref[p]
            q2 = jnp.concatenate([kp, rp], axis=0)
            a_kn = _mm(q2, kn, _NT)
            a_bn = _mm(q2, bn, _NT)
            a_k = a_kn[:P] * m_strict
            a_rk = a_kn[P:] * m_incl
            a_b = a_bn[:P]
            a_rb = a_bn[P:] * m_incl

            n1 = -(a_b * m_base)
            inv = eye + n1
            n2 = _mm(n1, n1, _NN)
            inv = inv + _mm(inv, n2, _NN)
            n4 = _mm(n2, n2, _NN)
            inv = inv + _mm(inv, n4, _NN)
            n8 = _mm(n4, n4, _NN)
            inv = inv + _mm(inv, n8, _NN)
            for msk in (m_l32, m_l64):
                inv = inv - _mm(inv, _mm(a_b * msk, inv, _NN), _NN)

            av = _mm(jnp.concatenate([a_k, a_rk], axis=0), v, _NN)
            x1 = _mm(q2, s0, _NT)
            u = _mm(inv, x1[:P] + av[:P], _NN)
            o = x1[P:] + av[P:] - _mm(a_rb, u, _NN)
            vu = jnp.concatenate([v, (-u).astype(BF16)], axis=0)
            k2 = jnp.concatenate([knp, bnp], axis=0)
            s_ref[p] = s0 * pc_ref[c, :, ln] + _mm(vu, k2, _TN)

            inv_n = 1.0 / RWKV_HEAD_DIM
            oc = (o - jnp.sum(o, axis=-1, keepdims=True) * inv_n) * own
            on = oc * lax.rsqrt(jnp.sum(oc * oc, axis=-1, keepdims=True) * inv_n + GN_EPS)
            on = on[:C] + on[C:]
            y = (on * gn_ref[:, ln] + bonus_ref[sl, ln]) * g_ref[sl, ln]
            y_ref[sl, ln] = y.astype(y_ref.dtype)
        return carry

    lax.fori_loop(0, nchunk, chunk, 0)


def _rwkv_recurrence(kp, rp, kn, bn, knp, bnp, v, pc, gn, bonus, g):
    T, W = kp.shape
    C = RWKV_CHUNK
    tb = 512
    nchunk = tb // C
    blk = pl.BlockSpec((tb, W), lambda i: (i, 0))
    return pl.pallas_call(
        functools.partial(_rwkv_rec_kernel, nchunk=nchunk),
        grid=(T // tb,),
        in_specs=[blk] * 7 + [pl.BlockSpec((nchunk, 1, W), lambda i: (i, 0, 0)),
                              pl.BlockSpec((1, W), lambda i: (0, 0)), blk, blk],
        out_specs=blk,
        out_shape=jax.ShapeDtypeStruct((T, W), BF16),
        scratch_shapes=[pltpu.VMEM((RWKV_HEADS // 2, LANE, LANE), F32)],
        compiler_params=_params("arbitrary"),
        name="rwkv_recurrence",
    )(kp, rp, kn, bn, knp, bnp, v, pc, gn, bonus, g)


def _rwkv_branch(p_rw, mu, w0, a0, wup, aup, gup, k_k, k_a, r_k, g_norm):
    kp, rp, kn, bn, knp, bnp, v, g, bonus, pc = _rwkv_prep(p_rw, mu, w0, a0, wup, aup, gup,
                                                          k_k, k_a, r_k)
    pc = pc.reshape(pc.shape[0], 1, pc.shape[1])
    return _rwkv_recurrence(kp, rp, kn, bn, knp, bnp, v, pc, g_norm.reshape(1, -1), bonus, g)


def _pad_cols(w, n):
    return jnp.pad(w, ((0, 0), (0, n - w.shape[1])))


def _block_diag(w):
    n, bi, bj = w.shape
    eye = jnp.eye(n, dtype=w.dtype)
    return (eye[:, None, :, None] * w[:, :, None, :]).reshape(n * bi, n * bj)


def _rope_tables(positions):
    pos = positions.astype(F32).reshape(-1, 1)
    T = pos.shape[0]

    def cs(dim):
        inv = 1.0 / (ROPE_BASE ** (jnp.arange(0, dim, 2, dtype=F32) / dim))
        ang = pos * inv
        return jnp.cos(ang), jnp.sin(ang)

    cm, sm = cs(MLA_ROPE_DIM)
    z32 = jnp.zeros((T, MLA_ROPE_DIM // 2), F32)
    z64 = jnp.zeros((T, LANE - MLA_ROPE_DIM), F32)
    mla = (jnp.concatenate([cm, cm, z64], axis=1),
           jnp.concatenate([-sm, z32, z64], axis=1),
           jnp.concatenate([z32, sm, z64], axis=1))
    cr, sr = cs(RET_HEAD_DIM)
    ret = (jnp.concatenate([cr, cr], axis=1), jnp.concatenate([-sr, sr], axis=1))
    return mla, ret


def _mla_weights(w_uq, w_ukv, g_qn, g_kn):
    wq = w_uq.reshape(MLA_Q_LORA, MLA_HEADS, MLA_QK_DIM)
    wq = jnp.pad(wq, ((0, 0), (0, 0), (0, MLA_QK_PAD - MLA_QK_DIM)))
    wq = wq.reshape(MLA_Q_LORA, MLA_HEADS * MLA_QK_PAD)
    wkv = w_ukv.reshape(MLA_KV_LORA, MLA_HEADS, MLA_NOPE_DIM + MLA_V_DIM)
    wk = wkv[:, :, :MLA_NOPE_DIM].reshape(MLA_KV_LORA, MLA_HEADS * MLA_NOPE_DIM)
    wv = wkv[:, :, MLA_NOPE_DIM:].reshape(MLA_KV_LORA, MLA_HEADS * MLA_V_DIM)
    pad = MLA_QK_PAD - MLA_QK_DIM
    gq = jnp.pad(g_qn, (0, pad)).reshape(1, MLA_QK_PAD)
    gk = jnp.pad(g_kn, (0, pad)).reshape(1, MLA_QK_PAD)
    return wq.astype(BF16), wk.astype(BF16), wv.astype(BF16), gq, gk


def _pad_rows(w, n):
    return jnp.pad(w, ((0, n - w.shape[0]), (0, 0)))


def kernel(x, c, positions, ada_w, ada_b, norm_mix, norm_ffn, w_in, conv_w, conv_b, lru_wr, lru_br, lru_wi, lru_bi, lru_lam, mla_g_cq, mla_g_ckv, mla_w_uq, mla_w_ukv, mla_g_qn, mla_g_kn, ret_g_norm, rwkv_mu, rwkv_w0, rwkv_w_up, rwkv_a0, rwkv_a_up, rwkv_g_up, rwkv_k_k, rwkv_k_a, rwkv_r_k, rwkv_g_norm, w_branch, w_out, ffn_w_in, ffn_w_out):
    B, T, D = x.shape
    assert B == 1 and D == D_MODEL
    depth = ada_w.shape[0]
    xt = x.reshape(T, D)
    mod_all = _modulation(c, ada_w, ada_b)
    (cc_m, sa_m, sb_m), (cc_r, ss_r) = _rope_tables(positions)

    o_a = GATE_COLS
    o_cq = o_a + 2 * LRU_WIDTH
    o_ckv = o_cq + MLA_Q_LORA
    o_kr = o_ckv + MLA_KV_LORA
    o_ret = o_kr + MLA_ROPE_DIM
    o_rw = o_ret + 4 * RET_WIDTH
    o_lora = o_rw + 3 * RWKV_WIDTH
    W = RWKV_WIDTH

    for l in range(depth):
        mod = mod_all[l]
        wl = w_in[l]
        w_ret = wl[:, o_ret:o_rw].astype(BF16)
        dl, al = RWKV_DECAY_LORA, RWKV_AAA_LORA
        w_rw = jnp.concatenate([wl[:, o_rw:o_lora],
                                _pad_cols(wl[:, o_lora:o_lora + dl], LANE),
                                _pad_cols(wl[:, o_lora + dl:o_lora + dl + al], LANE),
                                wl[:, o_lora + dl + al:]], axis=1).astype(BF16)
        mu = rwkv_mu[l]
        mu_p = jnp.concatenate([mu[:3 * W], jnp.pad(mu[3 * W:3 * W + dl], (0, LANE - dl)),
                                jnp.pad(mu[3 * W + dl:3 * W + dl + al], (0, LANE - al)),
                                mu[3 * W + dl + al:]]).reshape(1, RWKV_PAD_COLS)

        h = _mod_norm(xt, norm_mix[l].reshape(1, D), mod, 0)
        p_a = _ws_matmul(h, w_in, l, o_a, 2 * LRU_WIDTH, 512, "in_proj_lru")
        p_mla = _ws_matmul(h, w_in, l, o_cq, o_ret - o_cq + MLA_ROPE_DIM, 768, "in_proj_mla")
        p_ret = _matmul(h, w_ret, F32, "in_proj_ret")
        p_rw = _matmul(h, w_rw, F32, "in_proj_rwkv")

        y_a = _lru_branch(p_a, conv_w[l], conv_b[l].reshape(1, -1),
                          _block_diag(lru_wr[l]).astype(BF16), lru_br[l].reshape(1, -1),
                          _block_diag(lru_wi[l]).astype(BF16), lru_bi[l].reshape(1, -1),
                          lru_lam[l].reshape(1, -1))

        wq, wk, wv, gq, gk = _mla_weights(mla_w_uq[l], mla_w_ukv[l], mla_g_qn[l], mla_g_kn[l])
        q, k, v = _mla_prep(p_mla, mla_g_cq[l].reshape(1, -1), mla_g_ckv[l].reshape(1, -1),
                            wq, wk, wv, gq, gk, cc_m, sa_m, sb_m)
        y_b = _causal_attention(q, k, v)

        y_c = _retention_branch(p_ret, cc_r, ss_r, ret_g_norm[l].reshape(1, -1))

        y_d = _rwkv_branch(p_rw, mu_p, rwkv_w0[l].reshape(1, -1), rwkv_a0[l].reshape(1, -1),
                           _pad_rows(rwkv_w_up[l], LANE), _pad_rows(rwkv_a_up[l], LANE),
                           rwkv_g_up[l], rwkv_k_k[l].reshape(1, -1), rwkv_k_a[l].reshape(1, -1),
                           rwkv_r_k[l].reshape(1, -1), rwkv_g_norm[l])

        merged = _gated_merge(h, w_in, (y_a, y_b, y_c, y_d), w_branch, l)
        xt = _matmul_gated_residual(merged, w_out, l, xt, mod, 2, 1024, "out_proj")

        h2 = _mod_norm(xt, norm_ffn[l].reshape(1, D), mod, 3)
        act = _swiglu_in(h2, ffn_w_in, l)
        xt = _matmul_gated_residual(act, ffn_w_out, l, xt, mod, 5, 512, "ffn_out")
    return xt.reshape(B, T, D)
```

```python
import functools
import math

import numpy as np
import jax
import jax.numpy as jnp
from jax import lax
from jax.experimental import pallas as pl
from jax.experimental.pallas import tpu as pltpu

F32 = jnp.float32
BF16 = jnp.bfloat16
HIGHEST = lax.Precision.HIGHEST

D_MODEL = 2048
N_BRANCH = 4
BRANCH_WIDTH = D_MODEL // N_BRANCH
NORM_EPS = 1e-6
GN_EPS = 1e-5
ROPE_BASE = 10000.0
LRU_WIDTH = BRANCH_WIDTH
LRU_C = 8.0
CONV_WIDTH = 4
MLA_HEADS = 4
MLA_NOPE_DIM = 128
MLA_ROPE_DIM = 64
MLA_V_DIM = 128
MLA_QK_DIM = MLA_NOPE_DIM + MLA_ROPE_DIM
MLA_QK_PAD = 256
MLA_Q_LORA = 384
MLA_KV_LORA = 256
RET_HEADS = 4
RET_HEAD_DIM = 128
RET_WIDTH = RET_HEADS * RET_HEAD_DIM
RET_CHUNK = 128
RWKV_HEAD_DIM = 64
RWKV_HEADS = 8
RWKV_WIDTH = RWKV_HEADS * RWKV_HEAD_DIM
RWKV_DECAY_LORA = 64
RWKV_AAA_LORA = 64
RWKV_GATE_LORA = 128
RWKV_CHUNK = 64
FFN_HIDDEN = 5632
GATE_COLS = N_BRANCH * D_MODEL
LANE = 128
SUBLANE = 8
VMEM_LIMIT_BYTES = 56 * 1024 * 1024


def _params(*sem):
    return pltpu.CompilerParams(dimension_semantics=sem, vmem_limit_bytes=VMEM_LIMIT_BYTES)


def _sigmoid(x):
    return 1.0 / (1.0 + jnp.exp(-x))


def _softplus(x):
    return jnp.maximum(x, 0.0) + jnp.log(1.0 + jnp.exp(-jnp.abs(x)))


def _bdot(a, b):
    return jnp.dot(a.astype(BF16), b.astype(BF16), preferred_element_type=F32)


def _fdot(a, b):
    return jnp.dot(a, b, preferred_element_type=F32, precision=HIGHEST)


def _mod_kernel(c_ref, w_ref, b_ref, o_ref):
    c = c_ref[...]
    ca = c * _sigmoid(c)
    o_ref[0] = jnp.sum(ca * w_ref[0], axis=0, keepdims=True) + b_ref[0]


def _modulation(c, ada_w, ada_b):
    L, D, N = ada_w.shape
    tn = 1024
    return pl.pallas_call(
        _mod_kernel,
        grid=(L, N // tn),
        in_specs=[pl.BlockSpec((D, 1), lambda l, j: (0, 0)),
                  pl.BlockSpec((1, D, tn), lambda l, j: (l, 0, j)),
                  pl.BlockSpec((1, 1, tn), lambda l, j: (l, 0, j))],
        out_specs=pl.BlockSpec((1, 1, tn), lambda l, j: (l, 0, j)),
        out_shape=jax.ShapeDtypeStruct((L, 1, N), F32),
        compiler_params=_params("arbitrary", "arbitrary"),
        name="adaln_mod",
    )(c.reshape(D, 1), ada_w, ada_b.reshape(L, 1, N))


def _norm_kernel(x_ref, g_ref, sh_ref, sc_ref, o_ref):
    x = x_ref[...]
    ms = jnp.mean(x * x, axis=-1, keepdims=True)
    y = x * lax.rsqrt(ms + NORM_EPS) * g_ref[...]
    o_ref[...] = (y * (1.0 + sc_ref[...]) + sh_ref[...]).astype(o_ref.dtype)


def _mod_norm(x, g, mod, shift_idx):
    T, D = x.shape
    tm = 512
    return pl.pallas_call(
        _norm_kernel,
        grid=(T // tm,),
        in_specs=[pl.BlockSpec((tm, D), lambda i: (i, 0)),
                  pl.BlockSpec((1, D), lambda i: (0, 0)),
                  pl.BlockSpec((1, D), lambda i: (0, shift_idx)),
                  pl.BlockSpec((1, D), lambda i: (0, shift_idx + 1))],
        out_specs=pl.BlockSpec((tm, D), lambda i: (i, 0)),
        out_shape=jax.ShapeDtypeStruct((T, D), BF16),
        compiler_params=_params("arbitrary"),
        name="mod_norm",
    )(x, g, mod, mod)


def _mm_kernel(a_ref, b_ref, o_ref):
    o_ref[...] = jnp.dot(a_ref[...], b_ref[...], preferred_element_type=F32).astype(o_ref.dtype)


def _matmul(a, b, out_dtype, name):
    M, K = a.shape
    N = b.shape[1]
    tm = 512
    tn = N if N <= 2048 else 512
    return pl.pallas_call(
        _mm_kernel,
        grid=(M // tm, N // tn),
        in_specs=[pl.BlockSpec((tm, K), lambda i, j: (i, 0)),
                  pl.BlockSpec((K, tn), lambda i, j: (0, j))],
        out_specs=pl.BlockSpec((tm, tn), lambda i, j: (i, j)),
        out_shape=jax.ShapeDtypeStruct((M, N), out_dtype),
        compiler_params=_params("arbitrary", "arbitrary"),
        name=name,
    )(a, b)


def _first_row_tile():
    return pl.program_id(1) == 0


def _ws_mm_kernel(a_ref, w_ref, o_ref, wb_ref):
    @pl.when(_first_row_tile())
    def _():
        wb_ref[...] = w_ref[...].astype(BF16)

    o_ref[...] = jnp.dot(a_ref[...], wb_ref[...],
                         preferred_element_type=F32).astype(o_ref.dtype)


def _ws_matmul(a, w, l, col0, ncols, tn, name):
    M, K = a.shape
    tm = 1024
    jb = col0 // tn
    assert col0 % tn == 0 and ncols % tn == 0
    return pl.pallas_call(
        _ws_mm_kernel,
        grid=(ncols // tn, M // tm),
        in_specs=[pl.BlockSpec((tm, K), lambda j, i: (i, 0)),
                  pl.BlockSpec((None, K, tn), lambda j, i: (l, 0, jb + j))],
        out_specs=pl.BlockSpec((tm, tn), lambda j, i: (i, j)),
        out_shape=jax.ShapeDtypeStruct((M, ncols), F32),
        scratch_shapes=[pltpu.VMEM((K, tn), BF16)],
        compiler_params=_params("arbitrary", "arbitrary"),
        name=name,
    )(a, w)


def _ws_mm_res_kernel(a_ref, w_ref, x_ref, g_ref, o_ref, wb_ref):
    @pl.when(_first_row_tile())
    def _():
        wb_ref[...] = w_ref[...].astype(BF16)

    acc = jnp.dot(a_ref[...], wb_ref[...], preferred_element_type=F32)
    o_ref[...] = x_ref[...] + g_ref[...] * acc


def _matmul_gated_residual(a, w, l, x, mod, gate_idx, tm, name):
    M, K = a.shape
    N = w.shape[2]
    tn = 512
    nj = N // tn
    return pl.pallas_call(
        _ws_mm_res_kernel,
        grid=(nj, M // tm),
        in_specs=[pl.BlockSpec((tm, K), lambda j, i: (i, 0)),
                  pl.BlockSpec((None, K, tn), lambda j, i: (l, 0, j)),
                  pl.BlockSpec((tm, tn), lambda j, i: (i, j)),
                  pl.BlockSpec((1, tn), lambda j, i: (0, gate_idx * nj + j))],
        out_specs=pl.BlockSpec((tm, tn), lambda j, i: (i, j)),
        out_shape=jax.ShapeDtypeStruct((M, N), F32),
        scratch_shapes=[pltpu.VMEM((K, tn), BF16)],
        compiler_params=_params("arbitrary", "arbitrary"),
        name=name,
    )(a, w, x, mod)


def _swiglu_kernel(a_ref, wg_ref, wv_ref, o_ref, wgb_ref, wvb_ref):
    @pl.when(_first_row_tile())
    def _():
        wgb_ref[...] = wg_ref[...].astype(BF16)
        wvb_ref[...] = wv_ref[...].astype(BF16)

    a = a_ref[...]
    ug = jnp.dot(a, wgb_ref[...], preferred_element_type=F32)
    uv = jnp.dot(a, wvb_ref[...], preferred_element_type=F32)
    o_ref[...] = (ug * _sigmoid(ug) * uv).astype(o_ref.dtype)


def _swiglu_in(a, w, l):
    M, K = a.shape
    H = w.shape[2] // 2
    tm, tn = 1024, 512
    nj = H // tn
    return pl.pallas_call(
        _swiglu_kernel,
        grid=(nj, M // tm),
        in_specs=[pl.BlockSpec((tm, K), lambda j, i: (i, 0)),
                  pl.BlockSpec((None, K, tn), lambda j, i: (l, 0, j)),
                  pl.BlockSpec((None, K, tn), lambda j, i: (l, 0, nj + j))],
        out_specs=pl.BlockSpec((tm, tn), lambda j, i: (i, j)),
        out_shape=jax.ShapeDtypeStruct((M, H), BF16),
        scratch_shapes=[pltpu.VMEM((K, tn), BF16), pltpu.VMEM((K, tn), BF16)],
        compiler_params=_params("arbitrary", "arbitrary"),
        name="ffn_in_swiglu",
    )(a, w, w)


def _merge_kernel(h_ref, g0_ref, g1_ref, g2_ref, g3_ref, y0_ref, y1_ref, y2_ref, y3_ref,
                  wb_ref, o_ref, gb_ref, wbb_ref):
    g_refs = (g0_ref, g1_ref, g2_ref, g3_ref)

    @pl.when(_first_row_tile())
    def _():
        for n in range(N_BRANCH):
            gb_ref[n] = g_refs[n][...].astype(BF16)
        wbb_ref[...] = wb_ref[...].astype(BF16)

    h = h_ref[...]
    acc = None
    for n, y_ref in enumerate((y0_ref, y1_ref, y2_ref, y3_ref)):
        logits = jnp.dot(h, gb_ref[n], preferred_element_type=F32)
        branch = jnp.dot(y_ref[...], wbb_ref[n], preferred_element_type=F32)
        term = _sigmoid(logits) * branch
        acc = term if acc is None else acc + term
    o_ref[...] = acc.astype(o_ref.dtype)


def _gated_merge(h, w_in, ys, w_branch, l):
    M, K = h.shape
    D = w_branch.shape[3]
    BW = w_branch.shape[2]
    tm, tn = 1024, 256
    nj = D // tn
    gate_specs = [pl.BlockSpec((None, K, tn),
                               functools.partial(lambda j, i, n: (l, 0, n * nj + j), n=n))
                  for n in range(N_BRANCH)]
    y_specs = [pl.BlockSpec((tm, BW), lambda j, i: (i, 0)) for _ in range(N_BRANCH)]
    return pl.pallas_call(
        _merge_kernel,
        grid=(nj, M // tm),
        in_specs=[pl.BlockSpec((tm, K), lambda j, i: (i, 0))] + gate_specs + y_specs
                 + [pl.BlockSpec((None, N_BRANCH, BW, tn), lambda j, i: (l, 0, 0, j))],
        out_specs=pl.BlockSpec((tm, tn), lambda j, i: (i, j)),
        out_shape=jax.ShapeDtypeStruct((M, D), BF16),
        scratch_shapes=[pltpu.VMEM((N_BRANCH, K, tn), BF16),
                        pltpu.VMEM((N_BRANCH, BW, tn), BF16)],
        compiler_params=_params("arbitrary", "arbitrary"),
        name="gated_merge",
    )(h, w_in, w_in, w_in, w_in, *ys, w_branch)


def _lru_kernel(ax_ref, ag_ref, cw_ref, cb_ref, wr_ref, br_ref, wi_ref, bi_ref, lam_ref,
                o_ref, xbuf, hcar):
    i = pl.program_id(0)
    tt = ax_ref.shape[0]

    @pl.when(i == 0)
    def _():
        xbuf[0:SUBLANE, :] = jnp.zeros((SUBLANE, LRU_WIDTH), F32)
        hcar[...] = jnp.zeros_like(hcar)

    xbuf[SUBLANE:SUBLANE + tt, :] = ax_ref[...]
    xa = cb_ref[...]
    for j in range(CONV_WIDTH):
        xa = xa + cw_ref[j:j + 1, :] * xbuf[pl.ds(SUBLANE - (CONV_WIDTH - 1) + j, tt), :]
    xbuf[0:SUBLANE, :] = ax_ref[tt - SUBLANE:tt, :]

    r = _sigmoid(_bdot(xa, wr_ref[...]) + br_ref[...])
    ig = _sigmoid(_bdot(xa, wi_ref[...]) + bi_ref[...])
    log_a = (-LRU_C * _softplus(-lam_ref[...])) * r
    a = jnp.exp(log_a)
    th = jnp.tanh(log_a)
    u = jnp.sqrt(-2.0 * th / (1.0 - th)) * (ig * xa)

    row = lax.broadcasted_iota(jnp.int32, (tt, LRU_WIDTH), 0)
    d = 1
    while d < tt:
        keep = row >= d
        u_s = jnp.where(keep, pltpu.roll(u, d, 0), 0.0)
        a_s = jnp.where(keep, pltpu.roll(a, d, 0), 1.0)
        u = u + a * u_s
        a = a * a_s
        d *= 2
    h = u + a * hcar[...]
    hcar[...] = h[tt - 1:tt, :]

    g = ag_ref[...]
    gelu = 0.5 * g * (1.0 + jnp.tanh(math.sqrt(2.0 / math.pi) * (g + 0.044715 * (g * g * g))))
    o_ref[...] = (h * gelu).astype(o_ref.dtype)


def _lru_branch(p_a, conv_w, conv_b, wr_bd, br, wi_bd, bi, lam):
    T = p_a.shape[0]
    W = LRU_WIDTH
    tt = 256
    row = lambda i: (0, 0)
    return pl.pallas_call(
        _lru_kernel,
        grid=(T // tt,),
        in_specs=[pl.BlockSpec((tt, W), lambda i: (i, 0)),
                  pl.BlockSpec((tt, W), lambda i: (i, 1)),
                  pl.BlockSpec((CONV_WIDTH, W), row),
                  pl.BlockSpec((1, W), row),
                  pl.BlockSpec((W, W), row),
                  pl.BlockSpec((1, W), row),
                  pl.BlockSpec((W, W), row),
                  pl.BlockSpec((1, W), row),
                  pl.BlockSpec((1, W), row)],
        out_specs=pl.BlockSpec((tt, W), lambda i: (i, 0)),
        out_shape=jax.ShapeDtypeStruct((T, W), BF16),
        scratch_shapes=[pltpu.VMEM((tt + SUBLANE, W), F32), pltpu.VMEM((1, W), F32)],
        compiler_params=_params("arbitrary"),
        name="rg_lru",
    )(p_a, p_a, conv_w, conv_b, wr_bd, br, wi_bd, bi, lam)


def _mla_prep_kernel(p_ref, gcq_ref, gckv_ref, wuq_ref, wuk_ref, wuv_ref, gq_ref, gk_ref,
                     cc_ref, sa_ref, sb_ref, q_ref, k_ref, v_ref):
    p = p_ref[...]
    cq = p[:, :MLA_Q_LORA]
    ckv = p[:, MLA_Q_LORA:MLA_Q_LORA + MLA_KV_LORA]
    kr = p[:, MLA_Q_LORA + MLA_KV_LORA:]
    lane = lax.broadcasted_iota(jnp.int32, kr.shape, 1)
    kr = jnp.where(lane < MLA_ROPE_DIM, kr, 0.0)

    def rms(x, g, n):
        return x * lax.rsqrt(jnp.sum(x * x, axis=-1, keepdims=True) * (1.0 / n) + NORM_EPS) * g

    cqn = rms(cq, gcq_ref[...], MLA_Q_LORA)
    ckvn = rms(ckv, gckv_ref[...], MLA_KV_LORA)
    q = _bdot(cqn, wuq_ref[...])
    kn = _bdot(ckvn, wuk_ref[...])
    v_ref[...] = _bdot(ckvn, wuv_ref[...]).astype(v_ref.dtype)
    cc, sa, sb = cc_ref[...], sa_ref[...], sb_ref[...]
    half = MLA_ROPE_DIM // 2

    def rope(x):
        x2 = x[:, LANE:]
        x2 = x2 * cc + pltpu.roll(x2, LANE - half, 1) * sa + pltpu.roll(x2, half, 1) * sb
        return jnp.concatenate([x[:, :LANE], x2], axis=-1)

    scale = MLA_QK_DIM ** -0.5 * math.log2(math.e)
    for h in range(MLA_HEADS):
        qh = rms(q[:, h * MLA_QK_PAD:(h + 1) * MLA_QK_PAD], gq_ref[...], MLA_QK_DIM)
        q_ref[:, h * MLA_QK_PAD:(h + 1) * MLA_QK_PAD] = (rope(qh) * scale).astype(q_ref.dtype)
        kh = jnp.concatenate([kn[:, h * MLA_NOPE_DIM:(h + 1) * MLA_NOPE_DIM], kr], axis=-1)
        kh = rms(kh, gk_ref[...], MLA_QK_DIM)
        k_ref[:, h * MLA_QK_PAD:(h + 1) * MLA_QK_PAD] = rope(kh).astype(k_ref.dtype)


def _mla_prep(p_mla, g_cq, g_ckv, wuq, wuk, wuv, gq, gk, cc, sa, sb):
    T, PW = p_mla.shape
    tt = 512
    QW = MLA_HEADS * MLA_QK_PAD
    VW = MLA_HEADS * MLA_V_DIM
    row = lambda i: (0, 0)
    tab = pl.BlockSpec((tt, LANE), lambda i: (i, 0))
    return pl.pallas_call(
        _mla_prep_kernel,
        grid=(T // tt,),
        in_specs=[pl.BlockSpec((tt, PW), lambda i: (i, 0)),
                  pl.BlockSpec((1, MLA_Q_LORA), row),
                  pl.BlockSpec((1, MLA_KV_LORA), row),
                  pl.BlockSpec((MLA_Q_LORA, QW), row),
                  pl.BlockSpec((MLA_KV_LORA, MLA_HEADS * MLA_NOPE_DIM), row),
                  pl.BlockSpec((MLA_KV_LORA, VW), row),
                  pl.BlockSpec((1, MLA_QK_PAD), row),
                  pl.BlockSpec((1, MLA_QK_PAD), row),
                  tab, tab, tab],
        out_specs=[pl.BlockSpec((tt, QW), lambda i: (i, 0)),
                   pl.BlockSpec((tt, QW), lambda i: (i, 0)),
                   pl.BlockSpec((tt, VW), lambda i: (i, 0))],
        out_shape=[jax.ShapeDtypeStruct((T, QW), BF16),
                   jax.ShapeDtypeStruct((T, QW), BF16),
                   jax.ShapeDtypeStruct((T, VW), BF16)],
        compiler_params=_params("arbitrary"),
        name="mla_prep",
    )(p_mla, g_cq, g_ckv, wuq, wuk, wuv, gq, gk, cc, sa, sb)


ATTN_ROW_GROUP = 32


def _attn_kernel(q_ref, k_ref, v_ref, o_ref, s_scr, p_scr, m_scr, l_scr, a_scr, acc_scr, *, blk):
    i = pl.program_id(1)
    RG = ATTN_ROW_GROUP
    ntile = blk // LANE
    m_scr[...] = jnp.full(m_scr.shape, -jnp.inf, F32)
    l_scr[...] = jnp.zeros(l_scr.shape, F32)
    acc_scr[...] = jnp.zeros(acc_scr.shape, F32)

    def step(j, masked):
        start = pl.multiple_of(j * blk, blk)
        kb = k_ref[pl.ds(start, blk), :]
        s_scr[...] = lax.dot_general(q_ref[...], kb, (((1,), (1,)), ((), ())),
                                     preferred_element_type=F32)

        def pass_max(g, carry):
            r = pl.ds(pl.multiple_of(g * RG, RG), RG)
            s = s_scr[r, :]
            if masked:
                row = g * RG + lax.broadcasted_iota(jnp.int32, (RG, blk), 0)
                col = lax.broadcasted_iota(jnp.int32, (RG, blk), 1)
                s = jnp.where(col <= row, s, -jnp.inf)
                s_scr[r, :] = s
            m_old = m_scr[r, :]
            m_new = jnp.maximum(m_old, jnp.max(s, axis=-1, keepdims=True))
            a_scr[r, :] = jnp.exp2(m_old - m_new)
            m_scr[r, :] = m_new
            return carry

        lax.fori_loop(0, blk // RG, pass_max, 0, unroll=True)

        def pass_exp(g, carry):
            r = pl.ds(pl.multiple_of(g * RG, RG), RG)
            m_new = m_scr[r, :]
            p = jnp.exp2(s_scr[r, :] - jnp.concatenate([m_new] * ntile, axis=1))
            psum = p[:, :LANE]
            for t in range(1, ntile):
                psum = psum + p[:, t * LANE:(t + 1) * LANE]
            l_scr[r, :] = a_scr[r, :] * l_scr[r, :] + psum
            p_scr[r, :] = p.astype(BF16)
            return carry

        lax.fori_loop(0, blk // RG, pass_exp, 0, unroll=4)
        vb = v_ref[pl.ds(start, blk), :]
        acc_scr[...] = acc_scr[...] * a_scr[...] + jnp.dot(p_scr[...], vb,
                                                           preferred_element_type=F32)

    def body(j, carry):
        step(j, False)
        return carry

    lax.fori_loop(0, i, body, 0)
    step(i, True)
    l = jnp.sum(l_scr[...], axis=-1, keepdims=True)
    o_ref[...] = (acc_scr[...] / l).astype(o_ref.dtype)


def _causal_attention(q, k, v):
    T = q.shape[0]
    blk = 512
    assert MLA_V_DIM == LANE
    stat = pltpu.VMEM((blk, LANE), F32)
    return pl.pallas_call(
        functools.partial(_attn_kernel, blk=blk),
        grid=(MLA_HEADS, T // blk),
        in_specs=[pl.BlockSpec((blk, MLA_QK_PAD), lambda h, i: (i, h)),
                  pl.BlockSpec((T, MLA_QK_PAD), lambda h, i: (0, h)),
                  pl.BlockSpec((T, MLA_V_DIM), lambda h, i: (0, h))],
        out_specs=pl.BlockSpec((blk, MLA_V_DIM), lambda h, i: (i, h)),
        out_shape=jax.ShapeDtypeStruct((T, MLA_HEADS * MLA_V_DIM), BF16),
        scratch_shapes=[pltpu.VMEM((blk, blk), F32), pltpu.VMEM((blk, blk), BF16),
                        stat, stat, stat, stat],
        compiler_params=_params("arbitrary", "arbitrary"),
        name="mla_attention",
    )(q, k, v)


_RET_LOG_GAMMA = np.log1p(-np.exp(np.linspace(math.log(1.0 / 32), math.log(1.0 / 512),
                                              RET_HEADS, dtype=np.float32))).astype(np.float32)


def _ret_kernel(q_ref, k_ref, v_ref, g_ref, cc_ref, ss_ref, gn_ref, o_ref, st_ref):
    i = pl.program_id(0)
    C, Dh = RET_CHUNK, RET_HEAD_DIM

    @pl.when(i == 0)
    def _():
        st_ref[...] = jnp.zeros_like(st_ref)

    cc, ss = cc_ref[...], ss_ref[...]
    row = lax.broadcasted_iota(jnp.int32, (C, C), 0)
    col = lax.broadcasted_iota(jnp.int32, (C, C), 1)
    rel = (row - col).astype(F32)
    idx = lax.broadcasted_iota(jnp.int32, (C, 1), 0).astype(F32)
    for h in range(RET_HEADS):
        lg = float(_RET_LOG_GAMMA[h])
        sl = slice(h * Dh, (h + 1) * Dh)
        q = q_ref[:, sl]
        k = k_ref[:, sl]
        v = v_ref[:, sl]
        q = q * cc + pltpu.roll(q, Dh // 2, 1) * ss
        k = (k * cc + pltpu.roll(k, Dh // 2, 1) * ss) * (Dh ** -0.5)
        decay_in = jnp.where(rel >= 0, jnp.exp(lg * jnp.maximum(rel, 0.0)), 0.0)
        q_dec = jnp.exp(lg * (idx + 1.0))
        k_dec = jnp.exp(lg * (C - 1.0 - idx))
        chunk_dec = math.exp(lg * C)
        inner = lax.dot_general(q.astype(BF16), k.astype(BF16), (((1,), (1,)), ((), ())),
                                preferred_element_type=F32) * decay_in
        st = st_ref[h]
        o = _bdot(inner, v) + _bdot(q * q_dec, st)
        st_ref[h] = chunk_dec * st + lax.dot_general(
            (k * k_dec).astype(BF16), v.astype(BF16), (((0,), (0,)), ((), ())),
            preferred_element_type=F32)
        oc = o - jnp.mean(o, axis=-1, keepdims=True)
        on = oc * lax.rsqrt(jnp.mean(oc * oc, axis=-1, keepdims=True) + GN_EPS) * gn_ref[:, sl]
        g = g_ref[:, sl]
        o_ref[:, sl] = (g * _sigmoid(g) * on).astype(o_ref.dtype)


def _retention_branch(p_ret, cc, ss, g_norm):
    T = p_ret.shape[0]
    C, W = RET_CHUNK, RET_WIDTH
    blk = lambda n: pl.BlockSpec((C, W), functools.partial(lambda i, n: (i, n), n=n))
    return pl.pallas_call(
        _ret_kernel,
        grid=(T // C,),
        in_specs=[blk(0), blk(1), blk(2), blk(3),
                  pl.BlockSpec((C, LANE), lambda i: (i, 0)),
                  pl.BlockSpec((C, LANE), lambda i: (i, 0)),
                  pl.BlockSpec((1, W), lambda i: (0, 0))],
        out_specs=pl.BlockSpec((C, W), lambda i: (i, 0)),
        out_shape=jax.ShapeDtypeStruct((T, W), BF16),
        scratch_shapes=[pltpu.VMEM((RET_HEADS, RET_HEAD_DIM, RET_HEAD_DIM), F32)],
        compiler_params=_params("arbitrary"),
        name="retention",
    )(p_ret, p_ret, p_ret, p_ret, cc, ss, g_norm)


RWKV_PAD_COLS = 3 * RWKV_WIDTH + 3 * LANE


def _split_dot(a, b_exact, terms):
    acc = None
    rem = a
    for _ in range(terms):
        piece = rem.astype(BF16)
        rem = rem - piece.astype(F32)
        d = jnp.dot(piece, b_exact, preferred_element_type=F32)
        acc = d if acc is None else acc + d
    return acc


def _rwkv_prep_kernel(p_ref, mu_ref, w0_ref, a0_ref, wup_ref, aup_ref, gup_ref, kk_ref, ka_ref,
                      rk_ref, ones_ref, tri_ref,
                      kp_ref, rp_ref, kn_ref, bn_ref, knp_ref, bnp_ref, v_ref, g_ref, bonus_ref,
                      pc_ref, buf):
    i = pl.program_id(0)
    tt = p_ref.shape[0]
    W = RWKV_WIDTH
    C = RWKV_CHUNK

    @pl.when(i == 0)
    def _():
        buf[0:SUBLANE, :] = jnp.zeros((SUBLANE, RWKV_PAD_COLS), F32)

    p = p_ref[...]
    buf[SUBLANE:SUBLANE + tt, :] = p
    prev = buf[pl.ds(SUBLANE - 1, tt), :]
    buf[0:SUBLANE, :] = p_ref[tt - SUBLANE:tt, :]
    xs = p + (prev - p) * mu_ref[...]
    r = xs[:, 0:W]
    k = xs[:, W:2 * W]
    v = xs[:, 2 * W:3 * W]
    wd = xs[:, 3 * W:3 * W + LANE]
    ad = xs[:, 3 * W + LANE:3 * W + 2 * LANE]
    gd = xs[:, 3 * W + 2 * LANE:3 * W + 3 * LANE]

    w_log = -_softplus(-(w0_ref[...] + _fdot(jnp.tanh(wd), wup_ref[...]))) - 0.5
    logw = -jnp.exp(w_log)
    a = _sigmoid(a0_ref[...] + _bdot(ad, aup_ref[...]))
    g_ref[...] = _bdot(_sigmoid(gd), gup_ref[...])

    ones_bd = ones_ref[...]
    kk = k * kk_ref[...]
    kk = kk / jnp.maximum(jnp.sqrt(_split_dot(kk * kk, ones_bd, 2)), 1e-12)
    kt = k * (1.0 + (a - 1.0) * ka_ref[...])
    bonus_ref[...] = _split_dot(r * kt * rk_ref[...], ones_bd, 2) * v
    v_ref[...] = v.astype(v_ref.dtype)

    tri = tri_ref[...]
    cum = jnp.concatenate([_split_dot_lhs_exact(tri, logw[c * C:(c + 1) * C], 3)
                           for c in range(tt // C)], axis=0)
    tot = jnp.sum(logw.reshape(tt // C, C, W), axis=1)
    pc_ref[...] = jnp.exp(tot)
    to_end = jnp.exp((tot[:, None, :] - cum.reshape(tt // C, C, W)).reshape(tt, W))
    e_neg = jnp.exp(-cum)
    beta = kk * a
    kp_ref[...] = (kk * jnp.exp(cum - logw)).astype(kp_ref.dtype)
    rp_ref[...] = (r * jnp.exp(cum)).astype(rp_ref.dtype)
    kn_ref[...] = (kt * e_neg).astype(kn_ref.dtype)
    bn_ref[...] = (beta * e_neg).astype(bn_ref.dtype)
    knp_ref[...] = (kt * to_end).astype(knp_ref.dtype)
    bnp_ref[...] = (beta * to_end).astype(bnp_ref.dtype)


def _split_dot_lhs_exact(a_exact, b, terms):
    acc = None
    rem = b
    for _ in range(terms):
        piece = rem.astype(BF16)
        rem = rem - piece.astype(F32)
        d = jnp.dot(a_exact, piece, preferred_element_type=F32)
        acc = d if acc is None else acc + d
    return acc


def _rwkv_prep(p_rw, mu, w0, a0, wup, aup, gup, k_k, k_a, r_k):
    T = p_rw.shape[0]
    W, C = RWKV_WIDTH, RWKV_CHUNK
    tt = 512
    hid = np.arange(W) // RWKV_HEAD_DIM
    ones_bd = jnp.asarray((hid[:, None] == hid[None, :]).astype(np.float32), dtype=BF16)
    tid = np.arange(C)
    tri = jnp.asarray((tid[:, None] >= tid[None, :]).astype(np.float32), dtype=BF16)
    row = lambda i: (0, 0)
    vec = pl.BlockSpec((1, W), row)
    lora = pl.BlockSpec((LANE, W), row)
    out = pl.BlockSpec((tt, W), lambda i: (i, 0))
    lo = jax.ShapeDtypeStruct((T, W), BF16)
    hi = jax.ShapeDtypeStruct((T, W), F32)
    return pl.pallas_call(
        _rwkv_prep_kernel,
        grid=(T // tt,),
        in_specs=[pl.BlockSpec((tt, RWKV_PAD_COLS), lambda i: (i, 0)),
                  pl.BlockSpec((1, RWKV_PAD_COLS), row),
                  vec, vec, lora, lora, lora, vec, vec, vec,
                  pl.BlockSpec((W, W), row),
                  pl.BlockSpec((C, C), row)],
        out_specs=[out] * 9 + [pl.BlockSpec((tt // C, W), lambda i: (i, 0))],
        out_shape=[lo] * 7 + [hi, hi, jax.ShapeDtypeStruct((T // C, W), F32)],
        scratch_shapes=[pltpu.VMEM((tt + SUBLANE, RWKV_PAD_COLS), F32)],
        compiler_params=_params("arbitrary"),
        name="rwkv_prep",
    )(p_rw, mu, w0, a0, wup, aup, gup, k_k, k_a, r_k, ones_bd, tri)


def _bmm(a, b, dims):
    return lax.dot_general(a.astype(BF16), b.astype(BF16), dims, preferred_element_type=F32)


_BNN = (((2,), (1,)), ((0,), (0,)))
_BNT = (((2,), (2,)), ((0,), (0,)))
_BTN = (((1,), (1,)), ((0,), (0,)))
RWKV_GROUP = 4


def _rwkv_rec_kernel(kp_ref, rp_ref, kn_ref, bn_ref, knp_ref, bnp_ref, v_ref, pc_ref, gn_ref,
                     bonus_ref, g_ref, y_ref, s_ref, *, nchunk):
    C = RWKV_CHUNK
    P = 2 * C
    assert P == LANE

    @pl.when(pl.program_id(0) == 0)
    def _():
        s_ref[...] = jnp.zeros_like(s_ref)

    row = lax.broadcasted_iota(jnp.int32, (P, P), 0)
    col = lax.broadcasted_iota(jnp.int32, (P, P), 1)

    def blocks(n):
        return jnp.where((row // n) == (col // n), 1.0, 0.0)

    own = blocks(C)
    own_bf = own.astype(BF16)
    lower = jnp.where(row > col, 1.0, 0.0)
    m_strict = own * lower
    m_incl = own * jnp.where(row >= col, 1.0, 0.0)
    m_base = blocks(16) * lower
    m_l32 = blocks(32) * lower - m_base
    m_l64 = m_strict - m_base - m_l32
    eye = jnp.where(row == col, 1.0, 0.0)

    npair = RWKV_HEADS // 2
    lanes = [slice(p * LANE, (p + 1) * LANE) for p in range(npair)]
    G = RWKV_GROUP

    def group(gi, carry):
        rows = [pl.ds(pl.multiple_of((gi * G + c) * C, C), C) for c in range(G)]

        def load(ref):
            x = jnp.stack([ref[sl, ln] for sl in rows for ln in lanes])
            return jnp.concatenate([x, x], axis=1) * own_bf

        kp, rp, kn, bn, knp, bnp, v = (load(r) for r in (kp_ref, rp_ref, kn_ref, bn_ref,
                                                         knp_ref, bnp_ref, v_ref))
        q2 = jnp.concatenate([kp, rp], axis=1)
        a_kn = _bmm(q2, kn, _BNT)
        a_bn = _bmm(q2, bn, _BNT)
        a_k = a_kn[:, :P] * m_strict
        a_rk = a_kn[:, P:] * m_incl
        a_b = a_bn[:, :P]
        a_rb = a_bn[:, P:] * m_incl

        n1 = -(a_b * m_base)
        inv = eye + n1
        n2 = _bmm(n1, n1, _BNN)
        inv = inv + _bmm(inv, n2, _BNN)
        n4 = _bmm(n2, n2, _BNN)
        inv = inv + _bmm(inv, n4, _BNN)
        n8 = _bmm(n4, n4, _BNN)
        inv = inv + _bmm(inv, n8, _BNN)
        for msk in (m_l32, m_l64):
            inv = inv - _bmm(inv, _bmm(a_b * msk, inv, _BNN), _BNN)
        av = _bmm(jnp.concatenate([a_k, a_rk], axis=1), v, _BNN)
        k2 = jnp.concatenate([knp, bnp], axis=1)

        s = s_ref[...]
        inv_n = 1.0 / RWKV_HEAD_DIM
        for c in range(G):
            b = slice(c * npair, (c + 1) * npair)
            x1 = _bmm(q2[b], s, _BNT)
            u = _bmm(inv[b], x1[:, :P] + av[b, :P], _BNN)
            o = x1[:, P:] + av[b, P:] - _bmm(a_rb[b], u, _BNN)
            vu = jnp.concatenate([v[b], (-u).astype(BF16)], axis=1)
            pc = pc_ref[gi * G + c]
            pc = jnp.stack([pc[:, ln] for ln in lanes])
            s = s * pc + _bmm(vu, k2[b], _BTN)

            oc = (o - jnp.sum(o, axis=-1, keepdims=True) * inv_n) * own
            on = oc * lax.rsqrt(jnp.sum(oc * oc, axis=-1, keepdims=True) * inv_n + GN_EPS)
            on = on[:, :C] + on[:, C:]
            for p, ln in enumerate(lanes):
                y = (on[p] * gn_ref[:, ln] + bonus_ref[rows[c], ln]) * g_ref[rows[c], ln]
                y_ref[rows[c], ln] = y.astype(y_ref.dtype)
        s_ref[...] = s
        return carry

    lax.fori_loop(0, nchunk // G, group, 0)


def _rwkv_recurrence(kp, rp, kn, bn, knp, bnp, v, pc, gn, bonus, g):
    T, W = kp.shape
    C = RWKV_CHUNK
    tb = 512
    nchunk = tb // C
    blk = pl.BlockSpec((tb, W), lambda i: (i, 0))
    return pl.pallas_call(
        functools.partial(_rwkv_rec_kernel, nchunk=nchunk),
        grid=(T // tb,),
        in_specs=[blk] * 7 + [pl.BlockSpec((nchunk, 1, W), lambda i: (i, 0, 0)),
                              pl.BlockSpec((1, W), lambda i: (0, 0)), blk, blk],
        out_specs=blk,
        out_shape=jax.ShapeDtypeStruct((T, W), BF16),
        scratch_shapes=[pltpu.VMEM((RWKV_HEADS // 2, LANE, LANE), F32)],
        compiler_params=_params("arbitrary"),
        name="rwkv_recurrence",
    )(kp, rp, kn, bn, knp, bnp, v, pc, gn, bonus, g)


def _rwkv_branch(p_rw, mu, w0, a0, wup, aup, gup, k_k, k_a, r_k, g_norm):
    kp, rp, kn, bn, knp, bnp, v, g, bonus, pc = _rwkv_prep(p_rw, mu, w0, a0, wup, aup, gup,
                                                          k_k, k_a, r_k)
    pc = pc.reshape(pc.shape[0], 1, pc.shape[1])
    return _rwkv_recurrence(kp, rp, kn, bn, knp, bnp, v, pc, g_norm.reshape(1, -1), bonus, g)


def _pad_cols(w, n):
    return jnp.pad(w, ((0, 0), (0, n - w.shape[1])))


def _block_diag(w):
    n, bi, bj = w.shape
    eye = jnp.eye(n, dtype=w.dtype)
    return (eye[:, None, :, None] * w[:, :, None, :]).reshape(n * bi, n * bj)


def _rope_tables(positions):
    pos = positions.astype(F32).reshape(-1, 1)
    T = pos.shape[0]

    def cs(dim):
        inv = 1.0 / (ROPE_BASE ** (jnp.arange(0, dim, 2, dtype=F32) / dim))
        ang = pos * inv
        return jnp.cos(ang), jnp.sin(ang)

    cm, sm = cs(MLA_ROPE_DIM)
    z32 = jnp.zeros((T, MLA_ROPE_DIM // 2), F32)
    z64 = jnp.zeros((T, LANE - MLA_ROPE_DIM), F32)
    mla = (jnp.concatenate([cm, cm, z64], axis=1),
           jnp.concatenate([-sm, z32, z64], axis=1),
           jnp.concatenate([z32, sm, z64], axis=1))
    cr, sr = cs(RET_HEAD_DIM)
    ret = (jnp.concatenate([cr, cr], axis=1), jnp.concatenate([-sr, sr], axis=1))
    return mla, ret


def _mla_weights(w_uq, w_ukv, g_qn, g_kn):
    wq = w_uq.reshape(MLA_Q_LORA, MLA_HEADS, MLA_QK_DIM)
    wq = jnp.pad(wq, ((0, 0), (0, 0), (0, MLA_QK_PAD - MLA_QK_DIM)))
    wq = wq.reshape(MLA_Q_LORA, MLA_HEADS * MLA_QK_PAD)
    wkv = w_ukv.reshape(MLA_KV_LORA, MLA_HEADS, MLA_NOPE_DIM + MLA_V_DIM)
    wk = wkv[:, :, :MLA_NOPE_DIM].reshape(MLA_KV_LORA, MLA_HEADS * MLA_NOPE_DIM)
    wv = wkv[:, :, MLA_NOPE_DIM:].reshape(MLA_KV_LORA, MLA_HEADS * MLA_V_DIM)
    pad = MLA_QK_PAD - MLA_QK_DIM
    gq = jnp.pad(g_qn, (0, pad)).reshape(1, MLA_QK_PAD)
    gk = jnp.pad(g_kn, (0, pad)).reshape(1, MLA_QK_PAD)
    return wq.astype(BF16), wk.astype(BF16), wv.astype(BF16), gq, gk


def _pad_rows(w, n):
    return jnp.pad(w, ((0, n - w.shape[0]), (0, 0)))


def kernel(x, c, positions, ada_w, ada_b, norm_mix, norm_ffn, w_in, conv_w, conv_b, lru_wr, lru_br, lru_wi, lru_bi, lru_lam, mla_g_cq, mla_g_ckv, mla_w_uq, mla_w_ukv, mla_g_qn, mla_g_kn, ret_g_norm, rwkv_mu, rwkv_w0, rwkv_w_up, rwkv_a0, rwkv_a_up, rwkv_g_up, rwkv_k_k, rwkv_k_a, rwkv_r_k, rwkv_g_norm, w_branch, w_out, ffn_w_in, ffn_w_out):
    B, T, D = x.shape
    assert B == 1 and D == D_MODEL
    depth = ada_w.shape[0]
    xt = x.reshape(T, D)
    mod_all = _modulation(c, ada_w, ada_b)
    (cc_m, sa_m, sb_m), (cc_r, ss_r) = _rope_tables(positions)
    w_in_bf = w_in.astype(BF16)

    o_a = GATE_COLS
    o_cq = o_a + 2 * LRU_WIDTH
    o_ckv = o_cq + MLA_Q_LORA
    o_kr = o_ckv + MLA_KV_LORA
    o_ret = o_kr + MLA_ROPE_DIM
    o_rw = o_ret + 4 * RET_WIDTH
    o_lora = o_rw + 3 * RWKV_WIDTH
    W = RWKV_WIDTH

    for l in range(depth):
        mod = mod_all[l]
        w_ret = w_in_bf[l, :, o_ret:o_rw]
        dl, al = RWKV_DECAY_LORA, RWKV_AAA_LORA
        w_rw = jnp.concatenate([w_in_bf[l, :, o_rw:o_lora],
                                _pad_cols(w_in_bf[l, :, o_lora:o_lora + dl], LANE),
                                _pad_cols(w_in_bf[l, :, o_lora + dl:o_lora + dl + al], LANE),
                                w_in_bf[l, :, o_lora + dl + al:]], axis=1)
        mu = rwkv_mu[l]
        mu_p = jnp.concatenate([mu[:3 * W], jnp.pad(mu[3 * W:3 * W + dl], (0, LANE - dl)),
                                jnp.pad(mu[3 * W + dl:3 * W + dl + al], (0, LANE - al)),
                                mu[3 * W + dl + al:]]).reshape(1, RWKV_PAD_COLS)

        h = _mod_norm(xt, norm_mix[l].reshape(1, D), mod, 0)
        p_a = _ws_matmul(h, w_in_bf, l, o_a, 2 * LRU_WIDTH, 512, "in_proj_lru")
        p_mla = _ws_matmul(h, w_in_bf, l, o_cq, o_ret - o_cq + MLA_ROPE_DIM, 768, "in_proj_mla")
        p_ret = _matmul(h, w_ret, F32, "in_proj_ret")
        p_rw = _matmul(h, w_rw, F32, "in_proj_rwkv")

        y_a = _lru_branch(p_a, conv_w[l], conv_b[l].reshape(1, -1),
                          _block_diag(lru_wr[l]).astype(BF16), lru_br[l].reshape(1, -1),
                          _block_diag(lru_wi[l]).astype(BF16), lru_bi[l].reshape(1, -1),
                          lru_lam[l].reshape(1, -1))

        wq, wk, wv, gq, gk = _mla_weights(mla_w_uq[l], mla_w_ukv[l], mla_g_qn[l], mla_g_kn[l])
        q, k, v = _mla_prep(p_mla, mla_g_cq[l].reshape(1, -1), mla_g_ckv[l].reshape(1, -1),
                            wq, wk, wv, gq, gk, cc_m, sa_m, sb_m)
        y_b = _causal_attention(q, k, v)

        y_c = _retention_branch(p_ret, cc_r, ss_r, ret_g_norm[l].reshape(1, -1))

        y_d = _rwkv_branch(p_rw, mu_p, rwkv_w0[l].reshape(1, -1), rwkv_a0[l].reshape(1, -1),
                           _pad_rows(rwkv_w_up[l], LANE), _pad_rows(rwkv_a_up[l], LANE),
                           rwkv_g_up[l], rwkv_k_k[l].reshape(1, -1), rwkv_k_a[l].reshape(1, -1),
                           rwkv_r_k[l].reshape(1, -1), rwkv_g_norm[l])

        merged = _gated_merge(h, w_in_bf, (y_a, y_b, y_c, y_d), w_branch, l)
        xt = _matmul_gated_residual(merged, w_out, l, xt, mod, 2, 1024, "out_proj")

        h2 = _mod_norm(xt, norm_ffn[l].reshape(1, D), mod, 3)
        act = _swiglu_in(h2, ffn_w_in, l)
        xt = _matmul_gated_residual(act, ffn_w_out, l, xt, mod, 5, 512, "ffn_out")
    return xt.reshape(B, T, D)
```

```python
import functools
import math

import numpy as np
import jax
import jax.numpy as jnp
from jax import lax
from jax.experimental import pallas as pl
from jax.experimental.pallas import tpu as pltpu

F32 = jnp.float32
BF16 = jnp.bfloat16
HIGHEST = lax.Precision.HIGHEST

D_MODEL = 2048
N_BRANCH = 4
BRANCH_WIDTH = D_MODEL // N_BRANCH
NORM_EPS = 1e-6
GN_EPS = 1e-5
ROPE_BASE = 10000.0
LRU_WIDTH = BRANCH_WIDTH
LRU_C = 8.0
CONV_WIDTH = 4
MLA_HEADS = 4
MLA_NOPE_DIM = 128
MLA_ROPE_DIM = 64
MLA_V_DIM = 128
MLA_QK_DIM = MLA_NOPE_DIM + MLA_ROPE_DIM
MLA_QK_PAD = 256
MLA_Q_LORA = 384
MLA_KV_LORA = 256
RET_HEADS = 4
RET_HEAD_DIM = 128
RET_WIDTH = RET_HEADS * RET_HEAD_DIM
RET_CHUNK = 128
RWKV_HEAD_DIM = 64
RWKV_HEADS = 8
RWKV_WIDTH = RWKV_HEADS * RWKV_HEAD_DIM
RWKV_DECAY_LORA = 64
RWKV_AAA_LORA = 64
RWKV_GATE_LORA = 128
RWKV_CHUNK = 64
FFN_HIDDEN = 5632
GATE_COLS = N_BRANCH * D_MODEL
LANE = 128
SUBLANE = 8
VMEM_LIMIT_BYTES = 56 * 1024 * 1024


def _params(*sem):
    return pltpu.CompilerParams(dimension_semantics=sem, vmem_limit_bytes=VMEM_LIMIT_BYTES)


def _sigmoid(x):
    return 1.0 / (1.0 + jnp.exp(-x))


def _softplus(x):
    return jnp.maximum(x, 0.0) + jnp.log(1.0 + jnp.exp(-jnp.abs(x)))


def _bdot(a, b):
    return jnp.dot(a.astype(BF16), b.astype(BF16), preferred_element_type=F32)


def _fdot(a, b):
    return jnp.dot(a, b, preferred_element_type=F32, precision=HIGHEST)


def _mod_kernel(c_ref, w_ref, b_ref, o_ref):
    c = c_ref[...]
    ca = c * _sigmoid(c)
    o_ref[0] = jnp.sum(ca * w_ref[0], axis=0, keepdims=True) + b_ref[0]


def _modulation(c, ada_w, ada_b):
    L, D, N = ada_w.shape
    tn = 1024
    return pl.pallas_call(
        _mod_kernel,
        grid=(L, N // tn),
        in_specs=[pl.BlockSpec((D, 1), lambda l, j: (0, 0)),
                  pl.BlockSpec((1, D, tn), lambda l, j: (l, 0, j)),
                  pl.BlockSpec((1, 1, tn), lambda l, j: (l, 0, j))],
        out_specs=pl.BlockSpec((1, 1, tn), lambda l, j: (l, 0, j)),
        out_shape=jax.ShapeDtypeStruct((L, 1, N), F32),
        compiler_params=_params("arbitrary", "arbitrary"),
        name="adaln_mod",
    )(c.reshape(D, 1), ada_w, ada_b.reshape(L, 1, N))


def _norm_kernel(x_ref, g_ref, sh_ref, sc_ref, o_ref):
    x = x_ref[...]
    ms = jnp.mean(x * x, axis=-1, keepdims=True)
    y = x * lax.rsqrt(ms + NORM_EPS) * g_ref[...]
    o_ref[...] = (y * (1.0 + sc_ref[...]) + sh_ref[...]).astype(o_ref.dtype)


def _mod_norm(x, g, mod, shift_idx):
    T, D = x.shape
    tm = 512
    return pl.pallas_call(
        _norm_kernel,
        grid=(T // tm,),
        in_specs=[pl.BlockSpec((tm, D), lambda i: (i, 0)),
                  pl.BlockSpec((1, D), lambda i: (0, 0)),
                  pl.BlockSpec((1, D), lambda i: (0, shift_idx)),
                  pl.BlockSpec((1, D), lambda i: (0, shift_idx + 1))],
        out_specs=pl.BlockSpec((tm, D), lambda i: (i, 0)),
        out_shape=jax.ShapeDtypeStruct((T, D), BF16),
        compiler_params=_params("arbitrary"),
        name="mod_norm",
    )(x, g, mod, mod)


def _mm_kernel(a_ref, b_ref, o_ref):
    o_ref[...] = jnp.dot(a_ref[...], b_ref[...], preferred_element_type=F32).astype(o_ref.dtype)


def _matmul(a, b, out_dtype, name):
    M, K = a.shape
    N = b.shape[1]
    tm = 1024
    return pl.pallas_call(
        _mm_kernel,
        grid=(M // tm,),
        in_specs=[pl.BlockSpec((tm, K), lambda i: (i, 0)),
                  pl.BlockSpec((K, N), lambda i: (0, 0))],
        out_specs=pl.BlockSpec((tm, N), lambda i: (i, 0)),
        out_shape=jax.ShapeDtypeStruct((M, N), out_dtype),
        compiler_params=_params("arbitrary"),
        name=name,
    )(a, b)


def _first_row_tile():
    return pl.program_id(1) == 0


def _ws_matmul(a, w, l, col0, ncols, tn, name):
    M, K = a.shape
    tm = 1024
    jb = col0 // tn
    assert col0 % tn == 0 and ncols % tn == 0 and w.dtype == BF16
    return pl.pallas_call(
        _mm_kernel,
        grid=(ncols // tn, M // tm),
        in_specs=[pl.BlockSpec((tm, K), lambda j, i: (i, 0)),
                  pl.BlockSpec((None, K, tn), lambda j, i: (l, 0, jb + j))],
        out_specs=pl.BlockSpec((tm, tn), lambda j, i: (i, j)),
        out_shape=jax.ShapeDtypeStruct((M, ncols), F32),
        compiler_params=_params("arbitrary", "arbitrary"),
        name=name,
    )(a, w)


def _ws_mm_res_kernel(a_ref, w_ref, x_ref, g_ref, o_ref, wb_ref):
    @pl.when(_first_row_tile())
    def _():
        wb_ref[...] = w_ref[...].astype(BF16)

    acc = jnp.dot(a_ref[...], wb_ref[...], preferred_element_type=F32)
    o_ref[...] = x_ref[...] + g_ref[...] * acc


def _mm_res_kernel(a_ref, w_ref, x_ref, g_ref, o_ref):
    acc = jnp.dot(a_ref[...], w_ref[...], preferred_element_type=F32)
    o_ref[...] = x_ref[...] + g_ref[...] * acc


def _matmul_gated_residual(a, w, l, x, mod, gate_idx, tm, tn, name):
    M, K = a.shape
    N = w.shape[2]
    nj = N // tn
    cast = w.dtype != BF16
    return pl.pallas_call(
        _ws_mm_res_kernel if cast else _mm_res_kernel,
        grid=(nj, M // tm),
        in_specs=[pl.BlockSpec((tm, K), lambda j, i: (i, 0)),
                  pl.BlockSpec((None, K, tn), lambda j, i: (l, 0, j)),
                  pl.BlockSpec((tm, tn), lambda j, i: (i, j)),
                  pl.BlockSpec((1, tn), lambda j, i: (0, gate_idx * nj + j))],
        out_specs=pl.BlockSpec((tm, tn), lambda j, i: (i, j)),
        out_shape=jax.ShapeDtypeStruct((M, N), F32),
        scratch_shapes=[pltpu.VMEM((K, tn), BF16)] if cast else [],
        compiler_params=_params("arbitrary", "arbitrary"),
        name=name,
    )(a, w, x, mod)


def _swiglu_kernel(a_ref, wg_ref, wv_ref, o_ref, wgb_ref, wvb_ref):
    @pl.when(_first_row_tile())
    def _():
        wgb_ref[...] = wg_ref[...].astype(BF16)
        wvb_ref[...] = wv_ref[...].astype(BF16)

    a = a_ref[...]
    ug = jnp.dot(a, wgb_ref[...], preferred_element_type=F32)
    uv = jnp.dot(a, wvb_ref[...], preferred_element_type=F32)
    o_ref[...] = (ug * _sigmoid(ug) * uv).astype(o_ref.dtype)


def _swiglu_in(a, w, l):
    M, K = a.shape
    H = w.shape[2] // 2
    tm, tn = 2048, 512
    nj = H // tn
    return pl.pallas_call(
        _swiglu_kernel,
        grid=(nj, M // tm),
        in_specs=[pl.BlockSpec((tm, K), lambda j, i: (i, 0)),
                  pl.BlockSpec((None, K, tn), lambda j, i: (l, 0, j)),
                  pl.BlockSpec((None, K, tn), lambda j, i: (l, 0, nj + j))],
        out_specs=pl.BlockSpec((tm, tn), lambda j, i: (i, j)),
        out_shape=jax.ShapeDtypeStruct((M, H), BF16),
        scratch_shapes=[pltpu.VMEM((K, tn), BF16), pltpu.VMEM((K, tn), BF16)],
        compiler_params=_params("arbitrary", "arbitrary"),
        name="ffn_in_swiglu",
    )(a, w, w)


def _merge_kernel(h_ref, g0_ref, g1_ref, g2_ref, g3_ref, y0_ref, y1_ref, y2_ref, y3_ref,
                  wb_ref, o_ref, wbb_ref):
    @pl.when(_first_row_tile())
    def _():
        wbb_ref[...] = wb_ref[...].astype(BF16)

    h = h_ref[...]
    acc = None
    for n, (g_ref, y_ref) in enumerate(((g0_ref, y0_ref), (g1_ref, y1_ref),
                                        (g2_ref, y2_ref), (g3_ref, y3_ref))):
        logits = jnp.dot(h, g_ref[...], preferred_element_type=F32)
        branch = jnp.dot(y_ref[...], wbb_ref[n], preferred_element_type=F32)
        term = _sigmoid(logits) * branch
        acc = term if acc is None else acc + term
    o_ref[...] = acc.astype(o_ref.dtype)


def _gated_merge(h, w_in, ys, w_branch, l):
    M, K = h.shape
    D = w_branch.shape[3]
    BW = w_branch.shape[2]
    tm, tn = 512, 512
    nj = D // tn
    assert w_in.dtype == BF16
    gate_specs = [pl.BlockSpec((None, K, tn),
                               functools.partial(lambda j, i, n: (l, 0, n * nj + j), n=n))
                  for n in range(N_BRANCH)]
    y_specs = [pl.BlockSpec((tm, BW), lambda j, i: (i, 0)) for _ in range(N_BRANCH)]
    return pl.pallas_call(
        _merge_kernel,
        grid=(nj, M // tm),
        in_specs=[pl.BlockSpec((tm, K), lambda j, i: (i, 0))] + gate_specs + y_specs
                 + [pl.BlockSpec((None, N_BRANCH, BW, tn), lambda j, i: (l, 0, 0, j))],
        out_specs=pl.BlockSpec((tm, tn), lambda j, i: (i, j)),
        out_shape=jax.ShapeDtypeStruct((M, D), BF16),
        scratch_shapes=[pltpu.VMEM((N_BRANCH, BW, tn), BF16)],
        compiler_params=_params("arbitrary", "arbitrary"),
        name="gated_merge",
    )(h, w_in, w_in, w_in, w_in, *ys, w_branch)


def _lru_kernel(ax_ref, ag_ref, cw_ref, cb_ref, wr_ref, br_ref, wi_ref, bi_ref, lam_ref,
                o_ref, xbuf, hcar):
    i = pl.program_id(0)
    tt = ax_ref.shape[0]

    @pl.when(i == 0)
    def _():
        xbuf[0:SUBLANE, :] = jnp.zeros((SUBLANE, LRU_WIDTH), F32)
        hcar[...] = jnp.zeros_like(hcar)

    xbuf[SUBLANE:SUBLANE + tt, :] = ax_ref[...]
    xa = cb_ref[...]
    for j in range(CONV_WIDTH):
        xa = xa + cw_ref[j:j + 1, :] * xbuf[pl.ds(SUBLANE - (CONV_WIDTH - 1) + j, tt), :]
    xbuf[0:SUBLANE, :] = ax_ref[tt - SUBLANE:tt, :]

    r = _sigmoid(_bdot(xa, wr_ref[...]) + br_ref[...])
    ig = _sigmoid(_bdot(xa, wi_ref[...]) + bi_ref[...])
    log_a = (-LRU_C * _softplus(-lam_ref[...])) * r
    a = jnp.exp(log_a)
    th = jnp.tanh(log_a)
    u = jnp.sqrt(-2.0 * th / (1.0 - th)) * (ig * xa)

    row = lax.broadcasted_iota(jnp.int32, (tt, LRU_WIDTH), 0)
    d = 1
    while d < tt:
        keep = row >= d
        u_s = jnp.where(keep, pltpu.roll(u, d, 0), 0.0)
        a_s = jnp.where(keep, pltpu.roll(a, d, 0), 1.0)
        u = u + a * u_s
        a = a * a_s
        d *= 2
    h = u + a * hcar[...]
    hcar[...] = h[tt - 1:tt, :]

    g = ag_ref[...]
    gelu = 0.5 * g * (1.0 + jnp.tanh(math.sqrt(2.0 / math.pi) * (g + 0.044715 * (g * g * g))))
    o_ref[...] = (h * gelu).astype(o_ref.dtype)


def _lru_branch(p_a, conv_w, conv_b, wr_bd, br, wi_bd, bi, lam):
    T = p_a.shape[0]
    W = LRU_WIDTH
    tt = 256
    row = lambda i: (0, 0)
    return pl.pallas_call(
        _lru_kernel,
        grid=(T // tt,),
        in_specs=[pl.BlockSpec((tt, W), lambda i: (i, 0)),
                  pl.BlockSpec((tt, W), lambda i: (i, 1)),
                  pl.BlockSpec((CONV_WIDTH, W), row),
                  pl.BlockSpec((1, W), row),
                  pl.BlockSpec((W, W), row),
                  pl.BlockSpec((1, W), row),
                  pl.BlockSpec((W, W), row),
                  pl.BlockSpec((1, W), row),
                  pl.BlockSpec((1, W), row)],
        out_specs=pl.BlockSpec((tt, W), lambda i: (i, 0)),
        out_shape=jax.ShapeDtypeStruct((T, W), BF16),
        scratch_shapes=[pltpu.VMEM((tt + SUBLANE, W), F32), pltpu.VMEM((1, W), F32)],
        compiler_params=_params("arbitrary"),
        name="rg_lru",
    )(p_a, p_a, conv_w, conv_b, wr_bd, br, wi_bd, bi, lam)


def _mla_prep_kernel(p_ref, gcq_ref, gckv_ref, wuq_ref, wuk_ref, wuv_ref, gq_ref, gk_ref,
                     cc_ref, sa_ref, sb_ref, q_ref, k_ref, v_ref):
    p = p_ref[...]
    cq = p[:, :MLA_Q_LORA]
    ckv = p[:, MLA_Q_LORA:MLA_Q_LORA + MLA_KV_LORA]
    kr = p[:, MLA_Q_LORA + MLA_KV_LORA:]
    lane = lax.broadcasted_iota(jnp.int32, kr.shape, 1)
    kr = jnp.where(lane < MLA_ROPE_DIM, kr, 0.0)

    def rms(x, g, n):
        return x * lax.rsqrt(jnp.sum(x * x, axis=-1, keepdims=True) * (1.0 / n) + NORM_EPS) * g

    cqn = rms(cq, gcq_ref[...], MLA_Q_LORA)
    ckvn = rms(ckv, gckv_ref[...], MLA_KV_LORA)
    q = _bdot(cqn, wuq_ref[...])
    kn = _bdot(ckvn, wuk_ref[...])
    v_ref[...] = _bdot(ckvn, wuv_ref[...]).astype(v_ref.dtype)
    cc, sa, sb = cc_ref[...], sa_ref[...], sb_ref[...]
    half = MLA_ROPE_DIM // 2

    def rope(x):
        x2 = x[:, LANE:]
        x2 = x2 * cc + pltpu.roll(x2, LANE - half, 1) * sa + pltpu.roll(x2, half, 1) * sb
        return jnp.concatenate([x[:, :LANE], x2], axis=-1)

    scale = MLA_QK_DIM ** -0.5 * math.log2(math.e)
    for h in range(MLA_HEADS):
        qh = rms(q[:, h * MLA_QK_PAD:(h + 1) * MLA_QK_PAD], gq_ref[...], MLA_QK_DIM)
        q_ref[:, h * MLA_QK_PAD:(h + 1) * MLA_QK_PAD] = (rope(qh) * scale).astype(q_ref.dtype)
        kh = jnp.concatenate([kn[:, h * MLA_NOPE_DIM:(h + 1) * MLA_NOPE_DIM], kr], axis=-1)
        kh = rms(kh, gk_ref[...], MLA_QK_DIM)
        k_ref[:, h * MLA_QK_PAD:(h + 1) * MLA_QK_PAD] = rope(kh).astype(k_ref.dtype)


def _mla_prep(p_mla, g_cq, g_ckv, wuq, wuk, wuv, gq, gk, cc, sa, sb):
    T, PW = p_mla.shape
    tt = 512
    QW = MLA_HEADS * MLA_QK_PAD
    VW = MLA_HEADS * MLA_V_DIM
    row = lambda i: (0, 0)
    tab = pl.BlockSpec((tt, LANE), lambda i: (i, 0))
    return pl.pallas_call(
        _mla_prep_kernel,
        grid=(T // tt,),
        in_specs=[pl.BlockSpec((tt, PW), lambda i: (i, 0)),
                  pl.BlockSpec((1, MLA_Q_LORA), row),
                  pl.BlockSpec((1, MLA_KV_LORA), row),
                  pl.BlockSpec((MLA_Q_LORA, QW), row),
                  pl.BlockSpec((MLA_KV_LORA, MLA_HEADS * MLA_NOPE_DIM), row),
                  pl.BlockSpec((MLA_KV_LORA, VW), row),
                  pl.BlockSpec((1, MLA_QK_PAD), row),
                  pl.BlockSpec((1, MLA_QK_PAD), row),
                  tab, tab, tab],
        out_specs=[pl.BlockSpec((tt, QW), lambda i: (i, 0)),
                   pl.BlockSpec((tt, QW), lambda i: (i, 0)),
                   pl.BlockSpec((tt, VW), lambda i: (i, 0))],
        out_shape=[jax.ShapeDtypeStruct((T, QW), BF16),
                   jax.ShapeDtypeStruct((T, QW), BF16),
                   jax.ShapeDtypeStruct((T, VW), BF16)],
        compiler_params=_params("arbitrary"),
        name="mla_prep",
    )(p_mla, g_cq, g_ckv, wuq, wuk, wuv, gq, gk, cc, sa, sb)


ATTN_BLOCK = 512


def _attn_kernel(q_ref, k_ref, v_ref, o_ref, *, blk):
    i = pl.program_id(1)
    q = q_ref[...]

    def step(j, carry, masked):
        m, l, acc = carry
        start = pl.multiple_of(j * blk, blk)
        kb = k_ref[pl.ds(start, blk), :]
        vb = v_ref[pl.ds(start, blk), :]
        s = lax.dot_general(q, kb, (((1,), (1,)), ((), ())), preferred_element_type=F32)
        if masked:
            row = lax.broadcasted_iota(jnp.int32, (blk, blk), 0)
            col = lax.broadcasted_iota(jnp.int32, (blk, blk), 1)
            s = jnp.where(col <= row, s, -jnp.inf)
        m_new = jnp.maximum(m, jnp.max(s, axis=-1, keepdims=True))
        alpha = jnp.exp2(m - m_new)
        p = jnp.exp2(s - m_new)
        l = alpha * l + jnp.sum(p, axis=-1, keepdims=True)
        acc = alpha * acc + jnp.dot(p.astype(BF16), vb, preferred_element_type=F32)
        return m_new, l, acc

    init = (jnp.full((blk, 1), -jnp.inf, F32), jnp.zeros((blk, 1), F32),
            jnp.zeros((blk, MLA_V_DIM), F32))
    carry = lax.fori_loop(0, i, lambda j, c: step(j, c, False), init)
    _, l, acc = step(i, carry, True)
    o_ref[...] = (acc / l).astype(o_ref.dtype)


def _causal_attention(q, k, v):
    T = q.shape[0]
    blk = ATTN_BLOCK
    return pl.pallas_call(
        functools.partial(_attn_kernel, blk=blk),
        grid=(MLA_HEADS, T // blk),
        in_specs=[pl.BlockSpec((blk, MLA_QK_PAD), lambda h, i: (i, h)),
                  pl.BlockSpec((T, MLA_QK_PAD), lambda h, i: (0, h)),
                  pl.BlockSpec((T, MLA_V_DIM), lambda h, i: (0, h))],
        out_specs=pl.BlockSpec((blk, MLA_V_DIM), lambda h, i: (i, h)),
        out_shape=jax.ShapeDtypeStruct((T, MLA_HEADS * MLA_V_DIM), BF16),
        compiler_params=_params("arbitrary", "arbitrary"),
        name="mla_attention",
    )(q, k, v)


_RET_LOG_GAMMA = np.log1p(-np.exp(np.linspace(math.log(1.0 / 32), math.log(1.0 / 512),
                                              RET_HEADS, dtype=np.float32))).astype(np.float32)


RET_GROUP = 4


def _ret_kernel(q_ref, k_ref, v_ref, g_ref, cc_ref, ss_ref, gn_ref, o_ref, st_ref):
    i = pl.program_id(0)
    C, Dh, H = RET_CHUNK, RET_HEAD_DIM, RET_HEADS

    @pl.when(i == 0)
    def _():
        st_ref[...] = jnp.zeros_like(st_ref)

    row = lax.broadcasted_iota(jnp.int32, (C, C), 0)
    col = lax.broadcasted_iota(jnp.int32, (C, C), 1)
    rel = (row - col).astype(F32)
    idx = lax.broadcasted_iota(jnp.int32, (C, 1), 0).astype(F32)
    lgs = [float(x) for x in _RET_LOG_GAMMA]
    decay_in = jnp.stack([jnp.where(rel >= 0, jnp.exp(lg * jnp.maximum(rel, 0.0)), 0.0)
                          for lg in lgs])
    q_dec = jnp.stack([jnp.exp(lg * (idx + 1.0)) for lg in lgs])
    k_dec = jnp.stack([jnp.exp(lg * (C - 1.0 - idx)) for lg in lgs])
    heads = [slice(h * Dh, (h + 1) * Dh) for h in range(H)]
    st = st_ref[...]
    for c in range(RET_GROUP):
        rows = slice(c * C, (c + 1) * C)
        cc, ss = cc_ref[rows, :], ss_ref[rows, :]

        def rope(ref):
            x = jnp.stack([ref[rows, sl] for sl in heads])
            return x * cc + pltpu.roll(x, Dh // 2, 2) * ss

        q = rope(q_ref)
        k = rope(k_ref) * (Dh ** -0.5)
        v = jnp.stack([v_ref[rows, sl] for sl in heads])
        inner = _bmm(q, k, _BNT) * decay_in
        o = _bmm(inner, v, _BNN) + _bmm(q * q_dec, st, _BNN)
        kv = _bmm(k * k_dec, v, _BTN)
        st = jnp.stack([math.exp(lg * C) * st[h] + kv[h] for h, lg in enumerate(lgs)])
        oc = o - jnp.mean(o, axis=-1, keepdims=True)
        on = oc * lax.rsqrt(jnp.mean(oc * oc, axis=-1, keepdims=True) + GN_EPS)
        for h, sl in enumerate(heads):
            g = g_ref[rows, sl]
            o_ref[rows, sl] = (g * _sigmoid(g) * (on[h] * gn_ref[:, sl])).astype(o_ref.dtype)
    st_ref[...] = st


def _retention_branch(p_ret, cc, ss, g_norm):
    T = p_ret.shape[0]
    C, W = RET_CHUNK * RET_GROUP, RET_WIDTH
    blk = lambda n: pl.BlockSpec((C, W), functools.partial(lambda i, n: (i, n), n=n))
    return pl.pallas_call(
        _ret_kernel,
        grid=(T // C,),
        in_specs=[blk(0), blk(1), blk(2), blk(3),
                  pl.BlockSpec((C, LANE), lambda i: (i, 0)),
                  pl.BlockSpec((C, LANE), lambda i: (i, 0)),
                  pl.BlockSpec((1, W), lambda i: (0, 0))],
        out_specs=pl.BlockSpec((C, W), lambda i: (i, 0)),
        out_shape=jax.ShapeDtypeStruct((T, W), BF16),
        scratch_shapes=[pltpu.VMEM((RET_HEADS, RET_HEAD_DIM, RET_HEAD_DIM), F32)],
        compiler_params=_params("arbitrary"),
        name="retention",
    )(p_ret, p_ret, p_ret, p_ret, cc, ss, g_norm)


RWKV_PAD_COLS = 3 * RWKV_WIDTH + 3 * LANE


def _split_dot(a, b_exact, terms):
    acc = None
    rem = a
    for _ in range(terms):
        piece = rem.astype(BF16)
        rem = rem - piece.astype(F32)
        d = jnp.dot(piece, b_exact, preferred_element_type=F32)
        acc = d if acc is None else acc + d
    return acc


def _rwkv_prep_kernel(p_ref, mu_ref, w0_ref, a0_ref, wup_ref, aup_ref, gup_ref, kk_ref, ka_ref,
                      rk_ref, ones_ref, tri_ref,
                      kp_ref, rp_ref, kn_ref, bn_ref, knp_ref, bnp_ref, v_ref, g_ref, bonus_ref,
                      pc_ref, buf):
    i = pl.program_id(0)
    tt = p_ref.shape[0]
    W = RWKV_WIDTH
    C = RWKV_CHUNK

    @pl.when(i == 0)
    def _():
        buf[0:SUBLANE, :] = jnp.zeros((SUBLANE, RWKV_PAD_COLS), F32)

    p = p_ref[...]
    buf[SUBLANE:SUBLANE + tt, :] = p
    prev = buf[pl.ds(SUBLANE - 1, tt), :]
    buf[0:SUBLANE, :] = p_ref[tt - SUBLANE:tt, :]
    xs = p + (prev - p) * mu_ref[...]
    r = xs[:, 0:W]
    k = xs[:, W:2 * W]
    v = xs[:, 2 * W:3 * W]
    wd = xs[:, 3 * W:3 * W + LANE]
    ad = xs[:, 3 * W + LANE:3 * W + 2 * LANE]
    gd = xs[:, 3 * W + 2 * LANE:3 * W + 3 * LANE]

    w_log = -_softplus(-(w0_ref[...] + _fdot(jnp.tanh(wd), wup_ref[...]))) - 0.5
    logw = -jnp.exp(w_log)
    a = _sigmoid(a0_ref[...] + _bdot(ad, aup_ref[...]))
    g_ref[...] = _bdot(_sigmoid(gd), gup_ref[...])

    ones_bd = ones_ref[...]
    kk = k * kk_ref[...]
    kk = kk / jnp.maximum(jnp.sqrt(_split_dot(kk * kk, ones_bd, 2)), 1e-12)
    kt = k * (1.0 + (a - 1.0) * ka_ref[...])
    bonus_ref[...] = _split_dot(r * kt * rk_ref[...], ones_bd, 2) * v
    v_ref[...] = v.astype(v_ref.dtype)

    tri = tri_ref[...]
    cum = jnp.concatenate([_split_dot_lhs_exact(tri, logw[c * C:(c + 1) * C], 3)
                           for c in range(tt // C)], axis=0)
    tot = jnp.sum(logw.reshape(tt // C, C, W), axis=1)
    pc_ref[...] = jnp.exp(tot)
    to_end = jnp.exp((tot[:, None, :] - cum.reshape(tt // C, C, W)).reshape(tt, W))
    e_neg = jnp.exp(-cum)
    beta = kk * a
    kp_ref[...] = (kk * jnp.exp(cum - logw)).astype(kp_ref.dtype)
    rp_ref[...] = (r * jnp.exp(cum)).astype(rp_ref.dtype)
    kn_ref[...] = (kt * e_neg).astype(kn_ref.dtype)
    bn_ref[...] = (beta * e_neg).astype(bn_ref.dtype)
    knp_ref[...] = (kt * to_end).astype(knp_ref.dtype)
    bnp_ref[...] = (beta * to_end).astype(bnp_ref.dtype)


def _split_dot_lhs_exact(a_exact, b, terms):
    acc = None
    rem = b
    for _ in range(terms):
        piece = rem.astype(BF16)
        rem = rem - piece.astype(F32)
        d = jnp.dot(a_exact, piece, preferred_element_type=F32)
        acc = d if acc is None else acc + d
    return acc


def _rwkv_prep(p_rw, mu, w0, a0, wup, aup, gup, k_k, k_a, r_k):
    T = p_rw.shape[0]
    W, C = RWKV_WIDTH, RWKV_CHUNK
    tt = 512
    hid = np.arange(W) // RWKV_HEAD_DIM
    ones_bd = jnp.asarray((hid[:, None] == hid[None, :]).astype(np.float32), dtype=BF16)
    tid = np.arange(C)
    tri = jnp.asarray((tid[:, None] >= tid[None, :]).astype(np.float32), dtype=BF16)
    row = lambda i: (0, 0)
    vec = pl.BlockSpec((1, W), row)
    lora = pl.BlockSpec((LANE, W), row)
    out = pl.BlockSpec((tt, W), lambda i: (i, 0))
    lo = jax.ShapeDtypeStruct((T, W), BF16)
    hi = jax.ShapeDtypeStruct((T, W), F32)
    return pl.pallas_call(
        _rwkv_prep_kernel,
        grid=(T // tt,),
        in_specs=[pl.BlockSpec((tt, RWKV_PAD_COLS), lambda i: (i, 0)),
                  pl.BlockSpec((1, RWKV_PAD_COLS), row),
                  vec, vec, lora, lora, lora, vec, vec, vec,
                  pl.BlockSpec((W, W), row),
                  pl.BlockSpec((C, C), row)],
        out_specs=[out] * 9 + [pl.BlockSpec((tt // C, W), lambda i: (i, 0))],
        out_shape=[lo] * 7 + [hi, hi, jax.ShapeDtypeStruct((T // C, W), F32)],
        scratch_shapes=[pltpu.VMEM((tt + SUBLANE, RWKV_PAD_COLS), F32)],
        compiler_params=_params("arbitrary"),
        name="rwkv_prep",
    )(p_rw, mu, w0, a0, wup, aup, gup, k_k, k_a, r_k, ones_bd, tri)


def _bmm(a, b, dims):
    return lax.dot_general(a.astype(BF16), b.astype(BF16), dims, preferred_element_type=F32)


_BNN = (((2,), (1,)), ((0,), (0,)))
_BNT = (((2,), (2,)), ((0,), (0,)))
_BTN = (((1,), (1,)), ((0,), (0,)))
RWKV_GROUP = 4


def _rwkv_rec_kernel(kp_ref, rp_ref, kn_ref, bn_ref, knp_ref, bnp_ref, v_ref, pc_ref, gn_ref,
                     bonus_ref, g_ref, y_ref, s_ref, *, nchunk):
    C = RWKV_CHUNK
    P = 2 * C
    assert P == LANE

    @pl.when(pl.program_id(0) == 0)
    def _():
        s_ref[...] = jnp.zeros_like(s_ref)

    row = lax.broadcasted_iota(jnp.int32, (P, P), 0)
    col = lax.broadcasted_iota(jnp.int32, (P, P), 1)

    def blocks(n):
        return jnp.where((row // n) == (col // n), 1.0, 0.0)

    own = blocks(C)
    own_bf = own.astype(BF16)
    lower = jnp.where(row > col, 1.0, 0.0)
    m_strict = own * lower
    m_incl = own * jnp.where(row >= col, 1.0, 0.0)
    m_base = blocks(16) * lower
    m_l32 = blocks(32) * lower - m_base
    m_l64 = m_strict - m_base - m_l32
    eye = jnp.where(row == col, 1.0, 0.0)

    npair = RWKV_HEADS // 2
    lanes = [slice(p * LANE, (p + 1) * LANE) for p in range(npair)]
    G = RWKV_GROUP

    def group(gi, carry):
        rows = [pl.ds(pl.multiple_of((gi * G + c) * C, C), C) for c in range(G)]

        def load(ref):
            x = jnp.stack([ref[sl, ln] for sl in rows for ln in lanes])
            return jnp.concatenate([x, x], axis=1) * own_bf

        kp, rp, kn, bn, knp, bnp, v = (load(r) for r in (kp_ref, rp_ref, kn_ref, bn_ref,
                                                         knp_ref, bnp_ref, v_ref))
        q2 = jnp.concatenate([kp, rp], axis=1)
        a_kn = _bmm(q2, kn, _BNT)
        a_bn = _bmm(q2, bn, _BNT)
        a_k = a_kn[:, :P] * m_strict
        a_rk = a_kn[:, P:] * m_incl
        a_b = a_bn[:, :P]
        a_rb = a_bn[:, P:] * m_incl

        n1 = -(a_b * m_base)
        inv = eye + n1
        n2 = _bmm(n1, n1, _BNN)
        inv = inv + _bmm(inv, n2, _BNN)
        n4 = _bmm(n2, n2, _BNN)
        inv = inv + _bmm(inv, n4, _BNN)
        n8 = _bmm(n4, n4, _BNN)
        inv = inv + _bmm(inv, n8, _BNN)
        for msk in (m_l32, m_l64):
            inv = inv - _bmm(inv, _bmm(a_b * msk, inv, _BNN), _BNN)
        av = _bmm(jnp.concatenate([a_k, a_rk], axis=1), v, _BNN)
        k2 = jnp.concatenate([knp, bnp], axis=1)

        s = s_ref[...]
        inv_n = 1.0 / RWKV_HEAD_DIM
        for c in range(G):
            b = slice(c * npair, (c + 1) * npair)
            x1 = _bmm(q2[b], s, _BNT)
            u = _bmm(inv[b], x1[:, :P] + av[b, :P], _BNN)
            o = x1[:, P:] + av[b, P:] - _bmm(a_rb[b], u, _BNN)
            vu = jnp.concatenate([v[b], (-u).astype(BF16)], axis=1)
            pc = pc_ref[gi * G + c]
            pc = jnp.stack([pc[:, ln] for ln in lanes])
            s = s * pc + _bmm(vu, k2[b], _BTN)

            oc = (o - jnp.sum(o, axis=-1, keepdims=True) * inv_n) * own
            on = oc * lax.rsqrt(jnp.sum(oc * oc, axis=-1, keepdims=True) * inv_n + GN_EPS)
            on = on[:, :C] + on[:, C:]
            for p, ln in enumerate(lanes):
                y = (on[p] * gn_ref[:, ln] + bonus_ref[rows[c], ln]) * g_ref[rows[c], ln]
                y_ref[rows[c], ln] = y.astype(y_ref.dtype)
        s_ref[...] = s
        return carry

    lax.fori_loop(0, nchunk // G, group, 0)


def _rwkv_recurrence(kp, rp, kn, bn, knp, bnp, v, pc, gn, bonus, g):
    T, W = kp.shape
    C = RWKV_CHUNK
    tb = 512
    nchunk = tb // C
    blk = pl.BlockSpec((tb, W), lambda i: (i, 0))
    return pl.pallas_call(
        functools.partial(_rwkv_rec_kernel, nchunk=nchunk),
        grid=(T // tb,),
        in_specs=[blk] * 7 + [pl.BlockSpec((nchunk, 1, W), lambda i: (i, 0, 0)),
                              pl.BlockSpec((1, W), lambda i: (0, 0)), blk, blk],
        out_specs=blk,
        out_shape=jax.ShapeDtypeStruct((T, W), BF16),
        scratch_shapes=[pltpu.VMEM((RWKV_HEADS // 2, LANE, LANE), F32)],
        compiler_params=_params("arbitrary"),
        name="rwkv_recurrence",
    )(kp, rp, kn, bn, knp, bnp, v, pc, gn, bonus, g)


def _rwkv_branch(p_rw, mu, w0, a0, wup, aup, gup, k_k, k_a, r_k, g_norm):
    kp, rp, kn, bn, knp, bnp, v, g, bonus, pc = _rwkv_prep(p_rw, mu, w0, a0, wup, aup, gup,
                                                          k_k, k_a, r_k)
    pc = pc.reshape(pc.shape[0], 1, pc.shape[1])
    return _rwkv_recurrence(kp, rp, kn, bn, knp, bnp, v, pc, g_norm.reshape(1, -1), bonus, g)


def _pad_cols(w, n):
    return jnp.pad(w, ((0, 0), (0, n - w.shape[1])))


def _block_diag(w):
    n, bi, bj = w.shape
    eye = jnp.eye(n, dtype=w.dtype)
    return (eye[:, None, :, None] * w[:, :, None, :]).reshape(n * bi, n * bj)


def _rope_tables(positions):
    pos = positions.astype(F32).reshape(-1, 1)
    T = pos.shape[0]

    def cs(dim):
        inv = 1.0 / (ROPE_BASE ** (jnp.arange(0, dim, 2, dtype=F32) / dim))
        ang = pos * inv
        return jnp.cos(ang), jnp.sin(ang)

    cm, sm = cs(MLA_ROPE_DIM)
    z32 = jnp.zeros((T, MLA_ROPE_DIM // 2), F32)
    z64 = jnp.zeros((T, LANE - MLA_ROPE_DIM), F32)
    mla = (jnp.concatenate([cm, cm, z64], axis=1),
           jnp.concatenate([-sm, z32, z64], axis=1),
           jnp.concatenate([z32, sm, z64], axis=1))
    cr, sr = cs(RET_HEAD_DIM)
    ret = (jnp.concatenate([cr, cr], axis=1), jnp.concatenate([-sr, sr], axis=1))
    return mla, ret


def _mla_weights(w_uq, w_ukv, g_qn, g_kn):
    wq = w_uq.reshape(MLA_Q_LORA, MLA_HEADS, MLA_QK_DIM)
    wq = jnp.pad(wq, ((0, 0), (0, 0), (0, MLA_QK_PAD - MLA_QK_DIM)))
    wq = wq.reshape(MLA_Q_LORA, MLA_HEADS * MLA_QK_PAD)
    wkv = w_ukv.reshape(MLA_KV_LORA, MLA_HEADS, MLA_NOPE_DIM + MLA_V_DIM)
    wk = wkv[:, :, :MLA_NOPE_DIM].reshape(MLA_KV_LORA, MLA_HEADS * MLA_NOPE_DIM)
    wv = wkv[:, :, MLA_NOPE_DIM:].reshape(MLA_KV_LORA, MLA_HEADS * MLA_V_DIM)
    pad = MLA_QK_PAD - MLA_QK_DIM
    gq = jnp.pad(g_qn, (0, pad)).reshape(1, MLA_QK_PAD)
    gk = jnp.pad(g_kn, (0, pad)).reshape(1, MLA_QK_PAD)
    return wq.astype(BF16), wk.astype(BF16), wv.astype(BF16), gq, gk


def _pad_rows(w, n):
    return jnp.pad(w, ((0, n - w.shape[0]), (0, 0)))


def kernel(x, c, positions, ada_w, ada_b, norm_mix, norm_ffn, w_in, conv_w, conv_b, lru_wr, lru_br, lru_wi, lru_bi, lru_lam, mla_g_cq, mla_g_ckv, mla_w_uq, mla_w_ukv, mla_g_qn, mla_g_kn, ret_g_norm, rwkv_mu, rwkv_w0, rwkv_w_up, rwkv_a0, rwkv_a_up, rwkv_g_up, rwkv_k_k, rwkv_k_a, rwkv_r_k, rwkv_g_norm, w_branch, w_out, ffn_w_in, ffn_w_out):
    B, T, D = x.shape
    assert B == 1 and D == D_MODEL
    depth = ada_w.shape[0]
    xt = x.reshape(T, D)
    mod_all = _modulation(c, ada_w, ada_b)
    (cc_m, sa_m, sb_m), (cc_r, ss_r) = _rope_tables(positions)
    w_in_bf = w_in.astype(BF16)
    ffn_w_out_bf = ffn_w_out.astype(BF16)

    o_a = GATE_COLS
    o_cq = o_a + 2 * LRU_WIDTH
    o_ckv = o_cq + MLA_Q_LORA
    o_kr = o_ckv + MLA_KV_LORA
    o_ret = o_kr + MLA_ROPE_DIM
    o_rw = o_ret + 4 * RET_WIDTH
    o_lora = o_rw + 3 * RWKV_WIDTH
    W = RWKV_WIDTH

    for l in range(depth):
        mod = mod_all[l]
        w_ret = w_in_bf[l, :, o_ret:o_rw]
        dl, al = RWKV_DECAY_LORA, RWKV_AAA_LORA
        w_rw = jnp.concatenate([w_in_bf[l, :, o_rw:o_lora],
                                _pad_cols(w_in_bf[l, :, o_lora:o_lora + dl], LANE),
                                _pad_cols(w_in_bf[l, :, o_lora + dl:o_lora + dl + al], LANE),
                                w_in_bf[l, :, o_lora + dl + al:]], axis=1)
        mu = rwkv_mu[l]
        mu_p = jnp.concatenate([mu[:3 * W], jnp.pad(mu[3 * W:3 * W + dl], (0, LANE - dl)),
                                jnp.pad(mu[3 * W + dl:3 * W + dl + al], (0, LANE - al)),
                                mu[3 * W + dl + al:]]).reshape(1, RWKV_PAD_COLS)

        h = _mod_norm(xt, norm_mix[l].reshape(1, D), mod, 0)
        p_a = _ws_matmul(h, w_in_bf, l, o_a, 2 * LRU_WIDTH, 512, "in_proj_lru")
        p_mla = _ws_matmul(h, w_in_bf, l, o_cq, o_ret - o_cq + MLA_ROPE_DIM, 768, "in_proj_mla")
        p_ret = _matmul(h, w_ret, F32, "in_proj_ret")
        p_rw = _matmul(h, w_rw, F32, "in_proj_rwkv")

        y_a = _lru_branch(p_a, conv_w[l], conv_b[l].reshape(1, -1),
                          _block_diag(lru_wr[l]).astype(BF16), lru_br[l].reshape(1, -1),
                          _block_diag(lru_wi[l]).astype(BF16), lru_bi[l].reshape(1, -1),
                          lru_lam[l].reshape(1, -1))

        wq, wk, wv, gq, gk = _mla_weights(mla_w_uq[l], mla_w_ukv[l], mla_g_qn[l], mla_g_kn[l])
        q, k, v = _mla_prep(p_mla, mla_g_cq[l].reshape(1, -1), mla_g_ckv[l].reshape(1, -1),
                            wq, wk, wv, gq, gk, cc_m, sa_m, sb_m)
        y_b = _causal_attention(q, k, v)

        y_c = _retention_branch(p_ret, cc_r, ss_r, ret_g_norm[l].reshape(1, -1))

        y_d = _rwkv_branch(p_rw, mu_p, rwkv_w0[l].reshape(1, -1), rwkv_a0[l].reshape(1, -1),
                           _pad_rows(rwkv_w_up[l], LANE), _pad_rows(rwkv_a_up[l], LANE),
                           rwkv_g_up[l], rwkv_k_k[l].reshape(1, -1), rwkv_k_a[l].reshape(1, -1),
                           rwkv_r_k[l].reshape(1, -1), rwkv_g_norm[l])

        merged = _gated_merge(h, w_in_bf, (y_a, y_b, y_c, y_d), w_branch, l)
        xt = _matmul_gated_residual(merged, w_out, l, xt, mod, 2, 512, 1024, "out_proj")

        h2 = _mod_norm(xt, norm_ffn[l].reshape(1, D), mod, 3)
        act = _swiglu_in(h2, ffn_w_in, l)
        xt = _matmul_gated_residual(act, ffn_w_out_bf, l, xt, mod, 5, 512, 1024, "ffn_out")
    return xt.reshape(B, T, D)
```

```python
import functools
import math

import numpy as np
import jax
import jax.numpy as jnp
from jax import lax
from jax.experimental import pallas as pl
from jax.experimental.pallas import tpu as pltpu

F32 = jnp.float32
BF16 = jnp.bfloat16
HIGHEST = lax.Precision.HIGHEST

D_MODEL = 2048
N_BRANCH = 4
BRANCH_WIDTH = D_MODEL // N_BRANCH
NORM_EPS = 1e-6
GN_EPS = 1e-5
ROPE_BASE = 10000.0
LRU_WIDTH = BRANCH_WIDTH
LRU_C = 8.0
CONV_WIDTH = 4
MLA_HEADS = 4
MLA_NOPE_DIM = 128
MLA_ROPE_DIM = 64
MLA_V_DIM = 128
MLA_QK_DIM = MLA_NOPE_DIM + MLA_ROPE_DIM
MLA_QK_PAD = 256
MLA_Q_LORA = 384
MLA_KV_LORA = 256
RET_HEADS = 4
RET_HEAD_DIM = 128
RET_WIDTH = RET_HEADS * RET_HEAD_DIM
RET_CHUNK = 128
RWKV_HEAD_DIM = 64
RWKV_HEADS = 8
RWKV_WIDTH = RWKV_HEADS * RWKV_HEAD_DIM
RWKV_DECAY_LORA = 64
RWKV_AAA_LORA = 64
RWKV_GATE_LORA = 128
RWKV_CHUNK = 64
FFN_HIDDEN = 5632
GATE_COLS = N_BRANCH * D_MODEL
LANE = 128
SUBLANE = 8
VMEM_LIMIT_BYTES = 56 * 1024 * 1024


def _params(*sem):
    return pltpu.CompilerParams(dimension_semantics=sem, vmem_limit_bytes=VMEM_LIMIT_BYTES)


def _sigmoid(x):
    return 1.0 / (1.0 + jnp.exp(-x))


def _softplus(x):
    return jnp.maximum(x, 0.0) + jnp.log(1.0 + jnp.exp(-jnp.abs(x)))


def _bdot(a, b):
    return jnp.dot(a.astype(BF16), b.astype(BF16), preferred_element_type=F32)


def _fdot(a, b):
    return jnp.dot(a, b, preferred_element_type=F32, precision=HIGHEST)


def _mod_kernel(c_ref, w_ref, b_ref, o_ref):
    c = c_ref[...]
    ca = c * _sigmoid(c)
    o_ref[0] = jnp.sum(ca * w_ref[0], axis=0, keepdims=True) + b_ref[0]


def _modulation(c, ada_w, ada_b):
    L, D, N = ada_w.shape
    tn = 1024
    return pl.pallas_call(
        _mod_kernel,
        grid=(L, N // tn),
        in_specs=[pl.BlockSpec((D, 1), lambda l, j: (0, 0)),
                  pl.BlockSpec((1, D, tn), lambda l, j: (l, 0, j)),
                  pl.BlockSpec((1, 1, tn), lambda l, j: (l, 0, j))],
        out_specs=pl.BlockSpec((1, 1, tn), lambda l, j: (l, 0, j)),
        out_shape=jax.ShapeDtypeStruct((L, 1, N), F32),
        compiler_params=_params("arbitrary", "arbitrary"),
        name="adaln_mod",
    )(c.reshape(D, 1), ada_w, ada_b.reshape(L, 1, N))


def _norm_kernel(x_ref, g_ref, sh_ref, sc_ref, o_ref):
    x = x_ref[...]
    ms = jnp.mean(x * x, axis=-1, keepdims=True)
    y = x * lax.rsqrt(ms + NORM_EPS) * g_ref[...]
    o_ref[...] = (y * (1.0 + sc_ref[...]) + sh_ref[...]).astype(o_ref.dtype)


def _mod_norm(x, g, mod, shift_idx):
    T, D = x.shape
    tm = 512
    return pl.pallas_call(
        _norm_kernel,
        grid=(T // tm,),
        in_specs=[pl.BlockSpec((tm, D), lambda i: (i, 0)),
                  pl.BlockSpec((1, D), lambda i: (0, 0)),
                  pl.BlockSpec((1, D), lambda i: (0, shift_idx)),
                  pl.BlockSpec((1, D), lambda i: (0, shift_idx + 1))],
        out_specs=pl.BlockSpec((tm, D), lambda i: (i, 0)),
        out_shape=jax.ShapeDtypeStruct((T, D), BF16),
        compiler_params=_params("arbitrary"),
        name="mod_norm",
    )(x, g, mod, mod)


def _mm_kernel(a_ref, b_ref, o_ref):
    o_ref[...] = jnp.dot(a_ref[...], b_ref[...], preferred_element_type=F32).astype(o_ref.dtype)


def _matmul(a, b, out_dtype, name):
    M, K = a.shape
    N = b.shape[1]
    tm = 1024
    return pl.pallas_call(
        _mm_kernel,
        grid=(M // tm,),
        in_specs=[pl.BlockSpec((tm, K), lambda i: (i, 0)),
                  pl.BlockSpec((K, N), lambda i: (0, 0))],
        out_specs=pl.BlockSpec((tm, N), lambda i: (i, 0)),
        out_shape=jax.ShapeDtypeStruct((M, N), out_dtype),
        compiler_params=_params("arbitrary"),
        name=name,
    )(a, b)


def _first_row_tile():
    return pl.program_id(1) == 0


def _ws_matmul(a, w, l, col0, ncols, tn, name):
    M, K = a.shape
    tm = 1024
    jb = col0 // tn
    assert col0 % tn == 0 and ncols % tn == 0 and w.dtype == BF16
    return pl.pallas_call(
        _mm_kernel,
        grid=(ncols // tn, M // tm),
        in_specs=[pl.BlockSpec((tm, K), lambda j, i: (i, 0)),
                  pl.BlockSpec((None, K, tn), lambda j, i: (l, 0, jb + j))],
        out_specs=pl.BlockSpec((tm, tn), lambda j, i: (i, j)),
        out_shape=jax.ShapeDtypeStruct((M, ncols), F32),
        compiler_params=_params("arbitrary", "arbitrary"),
        name=name,
    )(a, w)


def _ws_mm_res_kernel(a_ref, w_ref, x_ref, g_ref, o_ref, wb_ref):
    @pl.when(_first_row_tile())
    def _():
        wb_ref[...] = w_ref[...].astype(BF16)

    acc = jnp.dot(a_ref[...], wb_ref[...], preferred_element_type=F32)
    o_ref[...] = x_ref[...] + g_ref[...] * acc


def _mm_res_kernel(a_ref, w_ref, x_ref, g_ref, o_ref):
    acc = jnp.dot(a_ref[...], w_ref[...], preferred_element_type=F32)
    o_ref[...] = x_ref[...] + g_ref[...] * acc


def _matmul_gated_residual(a, w, l, x, mod, gate_idx, tm, tn, name):
    M, K = a.shape
    N = w.shape[2]
    nj = N // tn
    cast = w.dtype != BF16
    return pl.pallas_call(
        _ws_mm_res_kernel if cast else _mm_res_kernel,
        grid=(nj, M // tm),
        in_specs=[pl.BlockSpec((tm, K), lambda j, i: (i, 0)),
                  pl.BlockSpec((None, K, tn), lambda j, i: (l, 0, j)),
                  pl.BlockSpec((tm, tn), lambda j, i: (i, j)),
                  pl.BlockSpec((1, tn), lambda j, i: (0, gate_idx * nj + j))],
        out_specs=pl.BlockSpec((tm, tn), lambda j, i: (i, j)),
        out_shape=jax.ShapeDtypeStruct((M, N), F32),
        scratch_shapes=[pltpu.VMEM((K, tn), BF16)] if cast else [],
        compiler_params=_params("arbitrary", "arbitrary"),
        name=name,
    )(a, w, x, mod)


def _swiglu_kernel(a_ref, wg_ref, wv_ref, o_ref, wgb_ref, wvb_ref):
    @pl.when(_first_row_tile())
    def _():
        wgb_ref[...] = wg_ref[...].astype(BF16)
        wvb_ref[...] = wv_ref[...].astype(BF16)

    a = a_ref[...]
    ug = jnp.dot(a, wgb_ref[...], preferred_element_type=F32)
    uv = jnp.dot(a, wvb_ref[...], preferred_element_type=F32)
    o_ref[...] = (ug * _sigmoid(ug) * uv).astype(o_ref.dtype)


def _swiglu_in(a, w, l):
    M, K = a.shape
    H = w.shape[2] // 2
    tm, tn = 2048, 512
    nj = H // tn
    return pl.pallas_call(
        _swiglu_kernel,
        grid=(nj, M // tm),
        in_specs=[pl.BlockSpec((tm, K), lambda j, i: (i, 0)),
                  pl.BlockSpec((None, K, tn), lambda j, i: (l, 0, j)),
                  pl.BlockSpec((None, K, tn), lambda j, i: (l, 0, nj + j))],
        out_specs=pl.BlockSpec((tm, tn), lambda j, i: (i, j)),
        out_shape=jax.ShapeDtypeStruct((M, H), BF16),
        scratch_shapes=[pltpu.VMEM((K, tn), BF16), pltpu.VMEM((K, tn), BF16)],
        compiler_params=_params("arbitrary", "arbitrary"),
        name="ffn_in_swiglu",
    )(a, w, w)


def _merge_kernel(h_ref, g0_ref, g1_ref, g2_ref, g3_ref, y0_ref, y1_ref, y2_ref, y3_ref,
                  wb_ref, o_ref, wbb_ref):
    @pl.when(_first_row_tile())
    def _():
        wbb_ref[...] = wb_ref[...].astype(BF16)

    h = h_ref[...]
    acc = None
    for n, (g_ref, y_ref) in enumerate(((g0_ref, y0_ref), (g1_ref, y1_ref),
                                        (g2_ref, y2_ref), (g3_ref, y3_ref))):
        logits = jnp.dot(h, g_ref[...], preferred_element_type=F32)
        branch = jnp.dot(y_ref[...], wbb_ref[n], preferred_element_type=F32)
        term = _sigmoid(logits) * branch
        acc = term if acc is None else acc + term
    o_ref[...] = acc.astype(o_ref.dtype)


def _gated_merge(h, w_in, ys, w_branch, l):
    M, K = h.shape
    D = w_branch.shape[3]
    BW = w_branch.shape[2]
    tm, tn = 512, 512
    nj = D // tn
    assert w_in.dtype == BF16
    gate_specs = [pl.BlockSpec((None, K, tn),
                               functools.partial(lambda j, i, n: (l, 0, n * nj + j), n=n))
                  for n in range(N_BRANCH)]
    y_specs = [pl.BlockSpec((tm, BW), lambda j, i: (i, 0)) for _ in range(N_BRANCH)]
    return pl.pallas_call(
        _merge_kernel,
        grid=(nj, M // tm),
        in_specs=[pl.BlockSpec((tm, K), lambda j, i: (i, 0))] + gate_specs + y_specs
                 + [pl.BlockSpec((None, N_BRANCH, BW, tn), lambda j, i: (l, 0, 0, j))],
        out_specs=pl.BlockSpec((tm, tn), lambda j, i: (i, j)),
        out_shape=jax.ShapeDtypeStruct((M, D), BF16),
        scratch_shapes=[pltpu.VMEM((N_BRANCH, BW, tn), BF16)],
        compiler_params=_params("arbitrary", "arbitrary"),
        name="gated_merge",
    )(h, w_in, w_in, w_in, w_in, *ys, w_branch)


def _lru_kernel(ax_ref, ag_ref, cw_ref, cb_ref, wr_ref, br_ref, wi_ref, bi_ref, lam_ref,
                o_ref, xbuf, hcar):
    i = pl.program_id(0)
    tt = ax_ref.shape[0]

    @pl.when(i == 0)
    def _():
        xbuf[0:SUBLANE, :] = jnp.zeros((SUBLANE, LRU_WIDTH), F32)
        hcar[...] = jnp.zeros_like(hcar)

    xbuf[SUBLANE:SUBLANE + tt, :] = ax_ref[...]
    xa = cb_ref[...]
    for j in range(CONV_WIDTH):
        xa = xa + cw_ref[j:j + 1, :] * xbuf[pl.ds(SUBLANE - (CONV_WIDTH - 1) + j, tt), :]
    xbuf[0:SUBLANE, :] = ax_ref[tt - SUBLANE:tt, :]

    r = _sigmoid(_bdot(xa, wr_ref[...]) + br_ref[...])
    ig = _sigmoid(_bdot(xa, wi_ref[...]) + bi_ref[...])
    log_a = (-LRU_C * _softplus(-lam_ref[...])) * r
    a = jnp.exp(log_a)
    th = jnp.tanh(log_a)
    u = jnp.sqrt(-2.0 * th / (1.0 - th)) * (ig * xa)

    row = lax.broadcasted_iota(jnp.int32, (tt, LRU_WIDTH), 0)
    d = 1
    while d < tt:
        keep = row >= d
        u_s = jnp.where(keep, pltpu.roll(u, d, 0), 0.0)
        a_s = jnp.where(keep, pltpu.roll(a, d, 0), 1.0)
        u = u + a * u_s
        a = a * a_s
        d *= 2
    h = u + a * hcar[...]
    hcar[...] = h[tt - 1:tt, :]

    g = ag_ref[...]
    gelu = 0.5 * g * (1.0 + jnp.tanh(math.sqrt(2.0 / math.pi) * (g + 0.044715 * (g * g * g))))
    o_ref[...] = (h * gelu).astype(o_ref.dtype)


def _lru_branch(p_a, conv_w, conv_b, wr_bd, br, wi_bd, bi, lam):
    T = p_a.shape[0]
    W = LRU_WIDTH
    tt = 256
    row = lambda i: (0, 0)
    return pl.pallas_call(
        _lru_kernel,
        grid=(T // tt,),
        in_specs=[pl.BlockSpec((tt, W), lambda i: (i, 0)),
                  pl.BlockSpec((tt, W), lambda i: (i, 1)),
                  pl.BlockSpec((CONV_WIDTH, W), row),
                  pl.BlockSpec((1, W), row),
                  pl.BlockSpec((W, W), row),
                  pl.BlockSpec((1, W), row),
                  pl.BlockSpec((W, W), row),
                  pl.BlockSpec((1, W), row),
                  pl.BlockSpec((1, W), row)],
        out_specs=pl.BlockSpec((tt, W), lambda i: (i, 0)),
        out_shape=jax.ShapeDtypeStruct((T, W), BF16),
        scratch_shapes=[pltpu.VMEM((tt + SUBLANE, W), F32), pltpu.VMEM((1, W), F32)],
        compiler_params=_params("arbitrary"),
        name="rg_lru",
    )(p_a, p_a, conv_w, conv_b, wr_bd, br, wi_bd, bi, lam)


def _mla_prep_kernel(p_ref, gcq_ref, gckv_ref, wuq_ref, wuk_ref, wuv_ref, gq_ref, gk_ref,
                     cc_ref, sa_ref, sb_ref, q_ref, k_ref, v_ref):
    p = p_ref[...]
    cq = p[:, :MLA_Q_LORA]
    ckv = p[:, MLA_Q_LORA:MLA_Q_LORA + MLA_KV_LORA]
    kr = p[:, MLA_Q_LORA + MLA_KV_LORA:]
    lane = lax.broadcasted_iota(jnp.int32, kr.shape, 1)
    kr = jnp.where(lane < MLA_ROPE_DIM, kr, 0.0)

    def rms(x, g, n):
        return x * lax.rsqrt(jnp.sum(x * x, axis=-1, keepdims=True) * (1.0 / n) + NORM_EPS) * g

    cqn = rms(cq, gcq_ref[...], MLA_Q_LORA)
    ckvn = rms(ckv, gckv_ref[...], MLA_KV_LORA)
    q = _bdot(cqn, wuq_ref[...])
    kn = _bdot(ckvn, wuk_ref[...])
    v_ref[...] = _bdot(ckvn, wuv_ref[...]).astype(v_ref.dtype)
    cc, sa, sb = cc_ref[...], sa_ref[...], sb_ref[...]
    half = MLA_ROPE_DIM // 2

    def rope(x):
        x2 = x[:, LANE:]
        x2 = x2 * cc + pltpu.roll(x2, LANE - half, 1) * sa + pltpu.roll(x2, half, 1) * sb
        return jnp.concatenate([x[:, :LANE], x2], axis=-1)

    scale = MLA_QK_DIM ** -0.5 * math.log2(math.e)
    for h in range(MLA_HEADS):
        qh = rms(q[:, h * MLA_QK_PAD:(h + 1) * MLA_QK_PAD], gq_ref[...], MLA_QK_DIM)
        q_ref[:, h * MLA_QK_PAD:(h + 1) * MLA_QK_PAD] = (rope(qh) * scale).astype(q_ref.dtype)
        kh = jnp.concatenate([kn[:, h * MLA_NOPE_DIM:(h + 1) * MLA_NOPE_DIM], kr], axis=-1)
        kh = rms(kh, gk_ref[...], MLA_QK_DIM)
        k_ref[:, h * MLA_QK_PAD:(h + 1) * MLA_QK_PAD] = rope(kh).astype(k_ref.dtype)


def _mla_prep(p_mla, g_cq, g_ckv, wuq, wuk, wuv, gq, gk, cc, sa, sb):
    T, PW = p_mla.shape
    tt = 512
    QW = MLA_HEADS * MLA_QK_PAD
    VW = MLA_HEADS * MLA_V_DIM
    row = lambda i: (0, 0)
    tab = pl.BlockSpec((tt, LANE), lambda i: (i, 0))
    return pl.pallas_call(
        _mla_prep_kernel,
        grid=(T // tt,),
        in_specs=[pl.BlockSpec((tt, PW), lambda i: (i, 0)),
                  pl.BlockSpec((1, MLA_Q_LORA), row),
                  pl.BlockSpec((1, MLA_KV_LORA), row),
                  pl.BlockSpec((MLA_Q_LORA, QW), row),
                  pl.BlockSpec((MLA_KV_LORA, MLA_HEADS * MLA_NOPE_DIM), row),
                  pl.BlockSpec((MLA_KV_LORA, VW), row),
                  pl.BlockSpec((1, MLA_QK_PAD), row),
                  pl.BlockSpec((1, MLA_QK_PAD), row),
                  tab, tab, tab],
        out_specs=[pl.BlockSpec((tt, QW), lambda i: (i, 0)),
                   pl.BlockSpec((tt, QW), lambda i: (i, 0)),
                   pl.BlockSpec((tt, VW), lambda i: (i, 0))],
        out_shape=[jax.ShapeDtypeStruct((T, QW), BF16),
                   jax.ShapeDtypeStruct((T, QW), BF16),
                   jax.ShapeDtypeStruct((T, VW), BF16)],
        compiler_params=_params("arbitrary"),
        name="mla_prep",
    )(p_mla, g_cq, g_ckv, wuq, wuk, wuv, gq, gk, cc, sa, sb)


ATTN_BLOCK = 512


ATTN_HEADS_PER_STEP = 2
ATTN_ROW_GROUP = 32


def _attn_kernel(q_ref, k_ref, v_ref, o_ref, s_scr, p_scr, m_scr, l_scr, a_scr, acc_scr, *, blk):
    i = pl.program_id(1)
    NH, RG = ATTN_HEADS_PER_STEP, ATTN_ROW_GROUP
    ntile = blk // LANE
    qk = [slice(t * MLA_QK_PAD, (t + 1) * MLA_QK_PAD) for t in range(NH)]
    vd = [slice(t * MLA_V_DIM, (t + 1) * MLA_V_DIM) for t in range(NH)]
    m_scr[...] = jnp.full(m_scr.shape, -jnp.inf, F32)
    l_scr[...] = jnp.zeros(l_scr.shape, F32)
    acc_scr[...] = jnp.zeros(acc_scr.shape, F32)

    def step(j, masked):
        start = pl.multiple_of(j * blk, blk)
        for t in range(NH):
            s_scr[t] = lax.dot_general(q_ref[:, qk[t]], k_ref[pl.ds(start, blk), qk[t]],
                                       (((1,), (1,)), ((), ())), preferred_element_type=F32)
        for t in range(NH):
            for g in range(blk // RG):
                r = slice(g * RG, (g + 1) * RG)
                s = s_scr[t, r, :]
                if masked:
                    row = g * RG + lax.broadcasted_iota(jnp.int32, (RG, blk), 0)
                    col = lax.broadcasted_iota(jnp.int32, (RG, blk), 1)
                    s = jnp.where(col <= row, s, -jnp.inf)
                    s_scr[t, r, :] = s
                m_old = m_scr[t, r, :]
                m_new = jnp.maximum(m_old, jnp.max(s, axis=-1, keepdims=True))
                a_scr[t, r, :] = jnp.exp2(m_old - m_new)
                m_scr[t, r, :] = m_new
            for g in range(blk // RG):
                r = slice(g * RG, (g + 1) * RG)
                m_new = m_scr[t, r, :]
                p = jnp.exp2(s_scr[t, r, :] - jnp.concatenate([m_new] * ntile, axis=1))
                psum = p[:, :LANE]
                for c in range(1, ntile):
                    psum = psum + p[:, c * LANE:(c + 1) * LANE]
                l_scr[t, r, :] = a_scr[t, r, :] * l_scr[t, r, :] + psum
                p_scr[t, r, :] = p.astype(BF16)
        for t in range(NH):
            acc_scr[t] = acc_scr[t] * a_scr[t] + jnp.dot(
                p_scr[t], v_ref[pl.ds(start, blk), vd[t]], preferred_element_type=F32)

    def body(j, carry):
        step(j, False)
        return carry

    lax.fori_loop(0, i, body, 0)
    step(i, True)
    for t in range(NH):
        l = jnp.sum(l_scr[t], axis=-1, keepdims=True)
        o_ref[:, vd[t]] = (acc_scr[t] / l).astype(o_ref.dtype)


def _causal_attention(q, k, v):
    T = q.shape[0]
    blk, NH = ATTN_BLOCK, ATTN_HEADS_PER_STEP
    assert MLA_V_DIM == LANE and MLA_HEADS % NH == 0
    stat = pltpu.VMEM((NH, blk, LANE), F32)
    return pl.pallas_call(
        functools.partial(_attn_kernel, blk=blk),
        grid=(MLA_HEADS // NH, T // blk),
        in_specs=[pl.BlockSpec((blk, NH * MLA_QK_PAD), lambda h, i: (i, h)),
                  pl.BlockSpec((T, NH * MLA_QK_PAD), lambda h, i: (0, h)),
                  pl.BlockSpec((T, NH * MLA_V_DIM), lambda h, i: (0, h))],
        out_specs=pl.BlockSpec((blk, NH * MLA_V_DIM), lambda h, i: (i, h)),
        out_shape=jax.ShapeDtypeStruct((T, MLA_HEADS * MLA_V_DIM), BF16),
        scratch_shapes=[pltpu.VMEM((NH, blk, blk), F32), pltpu.VMEM((NH, blk, blk), BF16),
                        stat, stat, stat, stat],
        compiler_params=_params("arbitrary", "arbitrary"),
        name="mla_attention",
    )(q, k, v)


_RET_LOG_GAMMA = np.log1p(-np.exp(np.linspace(math.log(1.0 / 32), math.log(1.0 / 512),
                                              RET_HEADS, dtype=np.float32))).astype(np.float32)


RET_GROUP = 4


def _ret_kernel(q_ref, k_ref, v_ref, g_ref, cc_ref, ss_ref, gn_ref, o_ref, st_ref):
    i = pl.program_id(0)
    C, Dh, H = RET_CHUNK, RET_HEAD_DIM, RET_HEADS

    @pl.when(i == 0)
    def _():
        st_ref[...] = jnp.zeros_like(st_ref)

    row = lax.broadcasted_iota(jnp.int32, (C, C), 0)
    col = lax.broadcasted_iota(jnp.int32, (C, C), 1)
    rel = (row - col).astype(F32)
    idx = lax.broadcasted_iota(jnp.int32, (C, 1), 0).astype(F32)
    lgs = [float(x) for x in _RET_LOG_GAMMA]
    decay_in = jnp.stack([jnp.where(rel >= 0, jnp.exp(lg * jnp.maximum(rel, 0.0)), 0.0)
                          for lg in lgs])
    q_dec = jnp.stack([jnp.exp(lg * (idx + 1.0)) for lg in lgs])
    k_dec = jnp.stack([jnp.exp(lg * (C - 1.0 - idx)) for lg in lgs])
    heads = [slice(h * Dh, (h + 1) * Dh) for h in range(H)]
    st = st_ref[...]
    for c in range(RET_GROUP):
        rows = slice(c * C, (c + 1) * C)
        cc, ss = cc_ref[rows, :], ss_ref[rows, :]

        def rope(ref):
            x = jnp.stack([ref[rows, sl] for sl in heads])
            return x * cc + pltpu.roll(x, Dh // 2, 2) * ss

        q = rope(q_ref)
        k = rope(k_ref) * (Dh ** -0.5)
        v = jnp.stack([v_ref[rows, sl] for sl in heads])
        inner = _bmm(q, k, _BNT) * decay_in
        o = _bmm(inner, v, _BNN) + _bmm(q * q_dec, st, _BNN)
        kv = _bmm(k * k_dec, v, _BTN)
        st = jnp.stack([math.exp(lg * C) * st[h] + kv[h] for h, lg in enumerate(lgs)])
        oc = o - jnp.mean(o, axis=-1, keepdims=True)
        on = oc * lax.rsqrt(jnp.mean(oc * oc, axis=-1, keepdims=True) + GN_EPS)
        for h, sl in enumerate(heads):
            g = g_ref[rows, sl]
            o_ref[rows, sl] = (g * _sigmoid(g) * (on[h] * gn_ref[:, sl])).astype(o_ref.dtype)
    st_ref[...] = st


def _retention_branch(p_ret, cc, ss, g_norm):
    T = p_ret.shape[0]
    C, W = RET_CHUNK * RET_GROUP, RET_WIDTH
    blk = lambda n: pl.BlockSpec((C, W), functools.partial(lambda i, n: (i, n), n=n))
    return pl.pallas_call(
        _ret_kernel,
        grid=(T // C,),
        in_specs=[blk(0), blk(1), blk(2), blk(3),
                  pl.BlockSpec((C, LANE), lambda i: (i, 0)),
                  pl.BlockSpec((C, LANE), lambda i: (i, 0)),
                  pl.BlockSpec((1, W), lambda i: (0, 0))],
        out_specs=pl.BlockSpec((C, W), lambda i: (i, 0)),
        out_shape=jax.ShapeDtypeStruct((T, W), BF16),
        scratch_shapes=[pltpu.VMEM((RET_HEADS, RET_HEAD_DIM, RET_HEAD_DIM), F32)],
        compiler_params=_params("arbitrary"),
        name="retention",
    )(p_ret, p_ret, p_ret, p_ret, cc, ss, g_norm)


RWKV_PAD_COLS = 3 * RWKV_WIDTH + 3 * LANE


def _split_dot(a, b_exact, terms):
    acc = None
    rem = a
    for _ in range(terms):
        piece = rem.astype(BF16)
        rem = rem - piece.astype(F32)
        d = jnp.dot(piece, b_exact, preferred_element_type=F32)
        acc = d if acc is None else acc + d
    return acc


def _rwkv_prep_kernel(p_ref, mu_ref, w0_ref, a0_ref, wup_ref, aup_ref, gup_ref, kk_ref, ka_ref,
                      rk_ref, ones_ref, tri_ref,
                      kp_ref, rp_ref, kn_ref, bn_ref, knp_ref, bnp_ref, v_ref, g_ref, bonus_ref,
                      pc_ref, buf):
    i = pl.program_id(0)
    tt = p_ref.shape[0]
    W = RWKV_WIDTH
    C = RWKV_CHUNK

    @pl.when(i == 0)
    def _():
        buf[0:SUBLANE, :] = jnp.zeros((SUBLANE, RWKV_PAD_COLS), F32)

    p = p_ref[...]
    buf[SUBLANE:SUBLANE + tt, :] = p
    prev = buf[pl.ds(SUBLANE - 1, tt), :]
    buf[0:SUBLANE, :] = p_ref[tt - SUBLANE:tt, :]
    xs = p + (prev - p) * mu_ref[...]
    r = xs[:, 0:W]
    k = xs[:, W:2 * W]
    v = xs[:, 2 * W:3 * W]
    wd = xs[:, 3 * W:3 * W + LANE]
    ad = xs[:, 3 * W + LANE:3 * W + 2 * LANE]
    gd = xs[:, 3 * W + 2 * LANE:3 * W + 3 * LANE]

    w_log = -_softplus(-(w0_ref[...] + _fdot(jnp.tanh(wd), wup_ref[...]))) - 0.5
    logw = -jnp.exp(w_log)
    a = _sigmoid(a0_ref[...] + _bdot(ad, aup_ref[...]))
    g_ref[...] = _bdot(_sigmoid(gd), gup_ref[...])

    ones_bd = ones_ref[...]
    kk = k * kk_ref[...]
    kk = kk / jnp.maximum(jnp.sqrt(_split_dot(kk * kk, ones_bd, 2)), 1e-12)
    kt = k * (1.0 + (a - 1.0) * ka_ref[...])
    bonus_ref[...] = _split_dot(r * kt * rk_ref[...], ones_bd, 2) * v
    v_ref[...] = v.astype(v_ref.dtype)

    tri = tri_ref[...]
    cum = jnp.concatenate([_split_dot_lhs_exact(tri, logw[c * C:(c + 1) * C], 3)
                           for c in range(tt // C)], axis=0)
    tot = jnp.sum(logw.reshape(tt // C, C, W), axis=1)
    pc_ref[...] = jnp.exp(tot)
    to_end = jnp.exp((tot[:, None, :] - cum.reshape(tt // C, C, W)).reshape(tt, W))
    e_neg = jnp.exp(-cum)
    beta = kk * a
    kp_ref[...] = (kk * jnp.exp(cum - logw)).astype(kp_ref.dtype)
    rp_ref[...] = (r * jnp.exp(cum)).astype(rp_ref.dtype)
    kn_ref[...] = (kt * e_neg).astype(kn_ref.dtype)
    bn_ref[...] = (beta * e_neg).astype(bn_ref.dtype)
    knp_ref[...] = (kt * to_end).astype(knp_ref.dtype)
    bnp_ref[...] = (beta * to_end).astype(bnp_ref.dtype)


def _split_dot_lhs_exact(a_exact, b, terms):
    acc = None
    rem = b
    for _ in range(terms):
        piece = rem.astype(BF16)
        rem = rem - piece.astype(F32)
        d = jnp.dot(a_exact, piece, preferred_element_type=F32)
        acc = d if acc is None else acc + d
    return acc


def _rwkv_prep(p_rw, mu, w0, a0, wup, aup, gup, k_k, k_a, r_k):
    T = p_rw.shape[0]
    W, C = RWKV_WIDTH, RWKV_CHUNK
    tt = 512
    hid = np.arange(W) // RWKV_HEAD_DIM
    ones_bd = jnp.asarray((hid[:, None] == hid[None, :]).astype(np.float32), dtype=BF16)
    tid = np.arange(C)
    tri = jnp.asarray((tid[:, None] >= tid[None, :]).astype(np.float32), dtype=BF16)
    row = lambda i: (0, 0)
    vec = pl.BlockSpec((1, W), row)
    lora = pl.BlockSpec((LANE, W), row)
    out = pl.BlockSpec((tt, W), lambda i: (i, 0))
    lo = jax.ShapeDtypeStruct((T, W), BF16)
    hi = jax.ShapeDtypeStruct((T, W), F32)
    return pl.pallas_call(
        _rwkv_prep_kernel,
        grid=(T // tt,),
        in_specs=[pl.BlockSpec((tt, RWKV_PAD_COLS), lambda i: (i, 0)),
                  pl.BlockSpec((1, RWKV_PAD_COLS), row),
                  vec, vec, lora, lora, lora, vec, vec, vec,
                  pl.BlockSpec((W, W), row),
                  pl.BlockSpec((C, C), row)],
        out_specs=[out] * 9 + [pl.BlockSpec((tt // C, W), lambda i: (i, 0))],
        out_shape=[lo] * 7 + [hi, hi, jax.ShapeDtypeStruct((T // C, W), F32)],
        scratch_shapes=[pltpu.VMEM((tt + SUBLANE, RWKV_PAD_COLS), F32)],
        compiler_params=_params("arbitrary"),
        name="rwkv_prep",
    )(p_rw, mu, w0, a0, wup, aup, gup, k_k, k_a, r_k, ones_bd, tri)


def _bmm(a, b, dims):
    return lax.dot_general(a.astype(BF16), b.astype(BF16), dims, preferred_element_type=F32)


_BNN = (((2,), (1,)), ((0,), (0,)))
_BNT = (((2,), (2,)), ((0,), (0,)))
_BTN = (((1,), (1,)), ((0,), (0,)))
RWKV_GROUP = 4


def _rwkv_rec_kernel(kp_ref, rp_ref, kn_ref, bn_ref, knp_ref, bnp_ref, v_ref, pc_ref, gn_ref,
                     bonus_ref, g_ref, y_ref, s_ref, *, nchunk):
    C = RWKV_CHUNK
    P = 2 * C
    assert P == LANE

    @pl.when(pl.program_id(0) == 0)
    def _():
        s_ref[...] = jnp.zeros_like(s_ref)

    row = lax.broadcasted_iota(jnp.int32, (P, P), 0)
    col = lax.broadcasted_iota(jnp.int32, (P, P), 1)

    def blocks(n):
        return jnp.where((row // n) == (col // n), 1.0, 0.0)

    own = blocks(C)
    own_bf = own.astype(BF16)
    lower = jnp.where(row > col, 1.0, 0.0)
    m_strict = own * lower
    m_incl = own * jnp.where(row >= col, 1.0, 0.0)
    m_base = blocks(16) * lower
    m_l32 = blocks(32) * lower - m_base
    m_l64 = m_strict - m_base - m_l32
    eye = jnp.where(row == col, 1.0, 0.0)

    npair = RWKV_HEADS // 2
    lanes = [slice(p * LANE, (p + 1) * LANE) for p in range(npair)]
    G = RWKV_GROUP

    def group(gi, carry):
        rows = [pl.ds(pl.multiple_of((gi * G + c) * C, C), C) for c in range(G)]

        def load(ref):
            x = jnp.stack([ref[sl, ln] for sl in rows for ln in lanes])
            return jnp.concatenate([x, x], axis=1) * own_bf

        kp, rp, kn, bn, knp, bnp, v = (load(r) for r in (kp_ref, rp_ref, kn_ref, bn_ref,
                                                         knp_ref, bnp_ref, v_ref))
        q2 = jnp.concatenate([kp, rp], axis=1)
        a_kn = _bmm(q2, kn, _BNT)
        a_bn = _bmm(q2, bn, _BNT)
        a_k = a_kn[:, :P] * m_strict
        a_rk = a_kn[:, P:] * m_incl
        a_b = a_bn[:, :P]
        a_rb = a_bn[:, P:] * m_incl

        n1 = -(a_b * m_base)
        inv = eye + n1
        n2 = _bmm(n1, n1, _BNN)
        inv = inv + _bmm(inv, n2, _BNN)
        n4 = _bmm(n2, n2, _BNN)
        inv = inv + _bmm(inv, n4, _BNN)
        n8 = _bmm(n4, n4, _BNN)
        inv = inv + _bmm(inv, n8, _BNN)
        for msk in (m_l32, m_l64):
            inv = inv - _bmm(inv, _bmm(a_b * msk, inv, _BNN), _BNN)
        av = _bmm(jnp.concatenate([a_k, a_rk], axis=1), v, _BNN)
        k2 = jnp.concatenate([knp, bnp], axis=1)

        s = s_ref[...]
        inv_n = 1.0 / RWKV_HEAD_DIM
        for c in range(G):
            b = slice(c * npair, (c + 1) * npair)
            x1 = _bmm(q2[b], s, _BNT)
            u = _bmm(inv[b], x1[:, :P] + av[b, :P], _BNN)
            o = x1[:, P:] + av[b, P:] - _bmm(a_rb[b], u, _BNN)
            vu = jnp.concatenate([v[b], (-u).astype(BF16)], axis=1)
            pc = pc_ref[gi * G + c]
            pc = jnp.stack([pc[:, ln] for ln in lanes])
            s = s * pc + _bmm(vu, k2[b], _BTN)

            oc = (o - jnp.sum(o, axis=-1, keepdims=True) * inv_n) * own
            on = oc * lax.rsqrt(jnp.sum(oc * oc, axis=-1, keepdims=True) * inv_n + GN_EPS)
            on = on[:, :C] + on[:, C:]
            for p, ln in enumerate(lanes):
                y = (on[p] * gn_ref[:, ln] + bonus_ref[rows[c], ln]) * g_ref[rows[c], ln]
                y_ref[rows[c], ln] = y.astype(y_ref.dtype)
        s_ref[...] = s
        return carry

    lax.fori_loop(0, nchunk // G, group, 0)


def _rwkv_recurrence(kp, rp, kn, bn, knp, bnp, v, pc, gn, bonus, g):
    T, W = kp.shape
    C = RWKV_CHUNK
    tb = 512
    nchunk = tb // C
    blk = pl.BlockSpec((tb, W), lambda i: (i, 0))
    return pl.pallas_call(
        functools.partial(_rwkv_rec_kernel, nchunk=nchunk),
        grid=(T // tb,),
        in_specs=[blk] * 7 + [pl.BlockSpec((nchunk, 1, W), lambda i: (i, 0, 0)),
                              pl.BlockSpec((1, W), lambda i: (0, 0)), blk, blk],
        out_specs=blk,
        out_shape=jax.ShapeDtypeStruct((T, W), BF16),
        scratch_shapes=[pltpu.VMEM((RWKV_HEADS // 2, LANE, LANE), F32)],
        compiler_params=_params("arbitrary"),
        name="rwkv_recurrence",
    )(kp, rp, kn, bn, knp, bnp, v, pc, gn, bonus, g)


def _rwkv_branch(p_rw, mu, w0, a0, wup, aup, gup, k_k, k_a, r_k, g_norm):
    kp, rp, kn, bn, knp, bnp, v, g, bonus, pc = _rwkv_prep(p_rw, mu, w0, a0, wup, aup, gup,
                                                          k_k, k_a, r_k)
    pc = pc.reshape(pc.shape[0], 1, pc.shape[1])
    return _rwkv_recurrence(kp, rp, kn, bn, knp, bnp, v, pc, g_norm.reshape(1, -1), bonus, g)


def _pad_cols(w, n):
    return jnp.pad(w, ((0, 0), (0, n - w.shape[1])))


def _block_diag(w):
    n, bi, bj = w.shape
    eye = jnp.eye(n, dtype=w.dtype)
    return (eye[:, None, :, None] * w[:, :, None, :]).reshape(n * bi, n * bj)


def _rope_tables(positions):
    pos = positions.astype(F32).reshape(-1, 1)
    T = pos.shape[0]

    def cs(dim):
        inv = 1.0 / (ROPE_BASE ** (jnp.arange(0, dim, 2, dtype=F32) / dim))
        ang = pos * inv
        return jnp.cos(ang), jnp.sin(ang)

    cm, sm = cs(MLA_ROPE_DIM)
    z32 = jnp.zeros((T, MLA_ROPE_DIM // 2), F32)
    z64 = jnp.zeros((T, LANE - MLA_ROPE_DIM), F32)
    mla = (jnp.concatenate([cm, cm, z64], axis=1),
           jnp.concatenate([-sm, z32, z64], axis=1),
           jnp.concatenate([z32, sm, z64], axis=1))
    cr, sr = cs(RET_HEAD_DIM)
    ret = (jnp.concatenate([cr, cr], axis=1), jnp.concatenate([-sr, sr], axis=1))
    return mla, ret


def _mla_weights(w_uq, w_ukv, g_qn, g_kn):
    wq = w_uq.reshape(MLA_Q_LORA, MLA_HEADS, MLA_QK_DIM)
    wq = jnp.pad(wq, ((0, 0), (0, 0), (0, MLA_QK_PAD - MLA_QK_DIM)))
    wq = wq.reshape(MLA_Q_LORA, MLA_HEADS * MLA_QK_PAD)
    wkv = w_ukv.reshape(MLA_KV_LORA, MLA_HEADS, MLA_NOPE_DIM + MLA_V_DIM)
    wk = wkv[:, :, :MLA_NOPE_DIM].reshape(MLA_KV_LORA, MLA_HEADS * MLA_NOPE_DIM)
    wv = wkv[:, :, MLA_NOPE_DIM:].reshape(MLA_KV_LORA, MLA_HEADS * MLA_V_DIM)
    pad = MLA_QK_PAD - MLA_QK_DIM
    gq = jnp.pad(g_qn, (0, pad)).reshape(1, MLA_QK_PAD)
    gk = jnp.pad(g_kn, (0, pad)).reshape(1, MLA_QK_PAD)
    return wq.astype(BF16), wk.astype(BF16), wv.astype(BF16), gq, gk


def _pad_rows(w, n):
    return jnp.pad(w, ((0, n - w.shape[0]), (0, 0)))


def kernel(x, c, positions, ada_w, ada_b, norm_mix, norm_ffn, w_in, conv_w, conv_b, lru_wr, lru_br, lru_wi, lru_bi, lru_lam, mla_g_cq, mla_g_ckv, mla_w_uq, mla_w_ukv, mla_g_qn, mla_g_kn, ret_g_norm, rwkv_mu, rwkv_w0, rwkv_w_up, rwkv_a0, rwkv_a_up, rwkv_g_up, rwkv_k_k, rwkv_k_a, rwkv_r_k, rwkv_g_norm, w_branch, w_out, ffn_w_in, ffn_w_out):
    B, T, D = x.shape
    assert B == 1 and D == D_MODEL
    depth = ada_w.shape[0]
    xt = x.reshape(T, D)
    mod_all = _modulation(c, ada_w, ada_b)
    (cc_m, sa_m, sb_m), (cc_r, ss_r) = _rope_tables(positions)
    w_in_bf = w_in.astype(BF16)
    ffn_w_out_bf = ffn_w_out.astype(BF16)

    o_a = GATE_COLS
    o_cq = o_a + 2 * LRU_WIDTH
    o_ckv = o_cq + MLA_Q_LORA
    o_kr = o_ckv + MLA_KV_LORA
    o_ret = o_kr + MLA_ROPE_DIM
    o_rw = o_ret + 4 * RET_WIDTH
    o_lora = o_rw + 3 * RWKV_WIDTH
    W = RWKV_WIDTH

    for l in range(depth):
        mod = mod_all[l]
        w_ret = w_in_bf[l, :, o_ret:o_rw]
        dl, al = RWKV_DECAY_LORA, RWKV_AAA_LORA
        w_rw = jnp.concatenate([w_in_bf[l, :, o_rw:o_lora],
                                _pad_cols(w_in_bf[l, :, o_lora:o_lora + dl], LANE),
                                _pad_cols(w_in_bf[l, :, o_lora + dl:o_lora + dl + al], LANE),
                                w_in_bf[l, :, o_lora + dl + al:]], axis=1)
        mu = rwkv_mu[l]
        mu_p = jnp.concatenate([mu[:3 * W], jnp.pad(mu[3 * W:3 * W + dl], (0, LANE - dl)),
                                jnp.pad(mu[3 * W + dl:3 * W + dl + al], (0, LANE - al)),
                                mu[3 * W + dl + al:]]).reshape(1, RWKV_PAD_COLS)

        h = _mod_norm(xt, norm_mix[l].reshape(1, D), mod, 0)
        p_a = _ws_matmul(h, w_in_bf, l, o_a, 2 * LRU_WIDTH, 512, "in_proj_lru")
        p_mla = _ws_matmul(h, w_in_bf, l, o_cq, o_ret - o_cq + MLA_ROPE_DIM, 768, "in_proj_mla")
        p_ret = _matmul(h, w_ret, F32, "in_proj_ret")
        p_rw = _matmul(h, w_rw, F32, "in_proj_rwkv")

        y_a = _lru_branch(p_a, conv_w[l], conv_b[l].reshape(1, -1),
                          _block_diag(lru_wr[l]).astype(BF16), lru_br[l].reshape(1, -1),
                          _block_diag(lru_wi[l]).astype(BF16), lru_bi[l].reshape(1, -1),
                          lru_lam[l].reshape(1, -1))

        wq, wk, wv, gq, gk = _mla_weights(mla_w_uq[l], mla_w_ukv[l], mla_g_qn[l], mla_g_kn[l])
        q, k, v = _mla_prep(p_mla, mla_g_cq[l].reshape(1, -1), mla_g_ckv[l].reshape(1, -1),
                            wq, wk, wv, gq, gk, cc_m, sa_m, sb_m)
        y_b = _causal_attention(q, k, v)

        y_c = _retention_branch(p_ret, cc_r, ss_r, ret_g_norm[l].reshape(1, -1))

        y_d = _rwkv_branch(p_rw, mu_p, rwkv_w0[l].reshape(1, -1), rwkv_a0[l].reshape(1, -1),
                           _pad_rows(rwkv_w_up[l], LANE), _pad_rows(rwkv_a_up[l], LANE),
                           rwkv_g_up[l], rwkv_k_k[l].reshape(1, -1), rwkv_k_a[l].reshape(1, -1),
                           rwkv_r_k[l].reshape(1, -1), rwkv_g_norm[l])

        merged = _gated_merge(h, w_in_bf, (y_a, y_b, y_c, y_d), w_branch, l)
        xt = _matmul_gated_residual(merged, w_out, l, xt, mod, 2, 512, 1024, "out_proj")

        h2 = _mod_norm(xt, norm_ffn[l].reshape(1, D), mod, 3)
        act = _swiglu_in(h2, ffn_w_in, l)
        xt = _matmul_gated_residual(act, ffn_w_out_bf, l, xt, mod, 5, 512, 1024, "ffn_out")
    return xt.reshape(B, T, D)
```

```python
import functools
import math

import numpy as np
import jax
import jax.numpy as jnp
from jax import lax
from jax.experimental import pallas as pl
from jax.experimental.pallas import tpu as pltpu

F32 = jnp.float32
BF16 = jnp.bfloat16
HIGHEST = lax.Precision.HIGHEST

D_MODEL = 2048
N_BRANCH = 4
BRANCH_WIDTH = D_MODEL // N_BRANCH
NORM_EPS = 1e-6
GN_EPS = 1e-5
ROPE_BASE = 10000.0
LRU_WIDTH = BRANCH_WIDTH
LRU_C = 8.0
CONV_WIDTH = 4
MLA_HEADS = 4
MLA_NOPE_DIM = 128
MLA_ROPE_DIM = 64
MLA_V_DIM = 128
MLA_QK_DIM = MLA_NOPE_DIM + MLA_ROPE_DIM
MLA_QK_PAD = 256
MLA_Q_LORA = 384
MLA_KV_LORA = 256
RET_HEADS = 4
RET_HEAD_DIM = 128
RET_WIDTH = RET_HEADS * RET_HEAD_DIM
RET_CHUNK = 128
RWKV_HEAD_DIM = 64
RWKV_HEADS = 8
RWKV_WIDTH = RWKV_HEADS * RWKV_HEAD_DIM
RWKV_DECAY_LORA = 64
RWKV_AAA_LORA = 64
RWKV_GATE_LORA = 128
RWKV_CHUNK = 64
FFN_HIDDEN = 5632
GATE_COLS = N_BRANCH * D_MODEL
LANE = 128
SUBLANE = 8
VMEM_LIMIT_BYTES = 56 * 1024 * 1024


def _params(*sem):
    return pltpu.CompilerParams(dimension_semantics=sem, vmem_limit_bytes=VMEM_LIMIT_BYTES)


def _sigmoid(x):
    return 1.0 / (1.0 + jnp.exp(-x))


def _softplus(x):
    return jnp.maximum(x, 0.0) + jnp.log(1.0 + jnp.exp(-jnp.abs(x)))


def _bdot(a, b):
    return jnp.dot(a.astype(BF16), b.astype(BF16), preferred_element_type=F32)


def _fdot(a, b):
    return jnp.dot(a, b, preferred_element_type=F32, precision=HIGHEST)


def _mod_kernel(c_ref, w_ref, b_ref, o_ref):
    c = c_ref[...]
    ca = c * _sigmoid(c)
    o_ref[0] = jnp.sum(ca * w_ref[0], axis=0, keepdims=True) + b_ref[0]


def _modulation(c, ada_w, ada_b):
    L, D, N = ada_w.shape
    tn = 1024
    return pl.pallas_call(
        _mod_kernel,
        grid=(L, N // tn),
        in_specs=[pl.BlockSpec((D, 1), lambda l, j: (0, 0)),
                  pl.BlockSpec((1, D, tn), lambda l, j: (l, 0, j)),
                  pl.BlockSpec((1, 1, tn), lambda l, j: (l, 0, j))],
        out_specs=pl.BlockSpec((1, 1, tn), lambda l, j: (l, 0, j)),
        out_shape=jax.ShapeDtypeStruct((L, 1, N), F32),
        compiler_params=_params("arbitrary", "arbitrary"),
        name="adaln_mod",
    )(c.reshape(D, 1), ada_w, ada_b.reshape(L, 1, N))


def _norm_kernel(x_ref, g_ref, sh_ref, sc_ref, o_ref):
    x = x_ref[...]
    ms = jnp.mean(x * x, axis=-1, keepdims=True)
    y = x * lax.rsqrt(ms + NORM_EPS) * g_ref[...]
    o_ref[...] = (y * (1.0 + sc_ref[...]) + sh_ref[...]).astype(o_ref.dtype)


def _mod_norm(x, g, mod, shift_idx):
    T, D = x.shape
    tm = 512
    return pl.pallas_call(
        _norm_kernel,
        grid=(T // tm,),
        in_specs=[pl.BlockSpec((tm, D), lambda i: (i, 0)),
                  pl.BlockSpec((1, D), lambda i: (0, 0)),
                  pl.BlockSpec((1, D), lambda i: (0, shift_idx)),
                  pl.BlockSpec((1, D), lambda i: (0, shift_idx + 1))],
        out_specs=pl.BlockSpec((tm, D), lambda i: (i, 0)),
        out_shape=jax.ShapeDtypeStruct((T, D), BF16),
        compiler_params=_params("arbitrary"),
        name="mod_norm",
    )(x, g, mod, mod)


def _mm_kernel(a_ref, b_ref, o_ref):
    o_ref[...] = jnp.dot(a_ref[...], b_ref[...], preferred_element_type=F32).astype(o_ref.dtype)


def _matmul(a, b, out_dtype, name):
    M, K = a.shape
    N = b.shape[1]
    tm = 1024
    return pl.pallas_call(
        _mm_kernel,
        grid=(M // tm,),
        in_specs=[pl.BlockSpec((tm, K), lambda i: (i, 0)),
                  pl.BlockSpec((K, N), lambda i: (0, 0))],
        out_specs=pl.BlockSpec((tm, N), lambda i: (i, 0)),
        out_shape=jax.ShapeDtypeStruct((M, N), out_dtype),
        compiler_params=_params("arbitrary"),
        name=name,
    )(a, b)


def _first_row_tile():
    return pl.program_id(1) == 0


def _ws_matmul(a, w, l, col0, ncols, tn, name):
    M, K = a.shape
    tm = 1024
    jb = col0 // tn
    assert col0 % tn == 0 and ncols % tn == 0 and w.dtype == BF16
    return pl.pallas_call(
        _mm_kernel,
        grid=(ncols // tn, M // tm),
        in_specs=[pl.BlockSpec((tm, K), lambda j, i: (i, 0)),
                  pl.BlockSpec((None, K, tn), lambda j, i: (l, 0, jb + j))],
        out_specs=pl.BlockSpec((tm, tn), lambda j, i: (i, j)),
        out_shape=jax.ShapeDtypeStruct((M, ncols), F32),
        compiler_params=_params("arbitrary", "arbitrary"),
        name=name,
    )(a, w)


def _ws_mm_res_kernel(a_ref, w_ref, x_ref, g_ref, o_ref, wb_ref):
    @pl.when(_first_row_tile())
    def _():
        wb_ref[...] = w_ref[...].astype(BF16)

    half = a_ref.shape[0] // 2
    for rows in (slice(0, half), slice(half, 2 * half)):
        acc = jnp.dot(a_ref[rows, :], wb_ref[...], preferred_element_type=F32)
        o_ref[rows, :] = x_ref[rows, :] + g_ref[...] * acc


def _mm_res_kernel(a_ref, w_ref, x_ref, g_ref, o_ref):
    acc = jnp.dot(a_ref[...], w_ref[...], preferred_element_type=F32)
    o_ref[...] = x_ref[...] + g_ref[...] * acc


def _matmul_gated_residual(a, w, l, x, mod, gate_idx, tm, tn, name):
    M, K = a.shape
    N = w.shape[2]
    nj = N // tn
    cast = w.dtype != BF16
    return pl.pallas_call(
        _ws_mm_res_kernel if cast else _mm_res_kernel,
        grid=(nj, M // tm),
        in_specs=[pl.BlockSpec((tm, K), lambda j, i: (i, 0)),
                  pl.BlockSpec((None, K, tn), lambda j, i: (l, 0, j)),
                  pl.BlockSpec((tm, tn), lambda j, i: (i, j)),
                  pl.BlockSpec((1, tn), lambda j, i: (0, gate_idx * nj + j))],
        out_specs=pl.BlockSpec((tm, tn), lambda j, i: (i, j)),
        out_shape=jax.ShapeDtypeStruct((M, N), F32),
        scratch_shapes=[pltpu.VMEM((K, tn), BF16)] if cast else [],
        compiler_params=_params("arbitrary", "arbitrary"),
        name=name,
    )(a, w, x, mod)


def _swiglu_kernel(a_ref, wg_ref, wv_ref, o_ref, wgb_ref, wvb_ref):
    @pl.when(_first_row_tile())
    def _():
        wgb_ref[...] = wg_ref[...].astype(BF16)
        wvb_ref[...] = wv_ref[...].astype(BF16)

    half = a_ref.shape[0] // 2
    for rows in (slice(0, half), slice(half, 2 * half)):
        a = a_ref[rows, :]
        ug = jnp.dot(a, wgb_ref[...], preferred_element_type=F32)
        uv = jnp.dot(a, wvb_ref[...], preferred_element_type=F32)
        o_ref[rows, :] = (ug * _sigmoid(ug) * uv).astype(o_ref.dtype)


def _swiglu_in(a, w, l):
    M, K = a.shape
    H = w.shape[2] // 2
    tm, tn = 2048, 512
    nj = H // tn
    return pl.pallas_call(
        _swiglu_kernel,
        grid=(nj, M // tm),
        in_specs=[pl.BlockSpec((tm, K), lambda j, i: (i, 0)),
                  pl.BlockSpec((None, K, tn), lambda j, i: (l, 0, j)),
                  pl.BlockSpec((None, K, tn), lambda j, i: (l, 0, nj + j))],
        out_specs=pl.BlockSpec((tm, tn), lambda j, i: (i, j)),
        out_shape=jax.ShapeDtypeStruct((M, H), BF16),
        scratch_shapes=[pltpu.VMEM((K, tn), BF16), pltpu.VMEM((K, tn), BF16)],
        compiler_params=_params("arbitrary", "arbitrary"),
        name="ffn_in_swiglu",
    )(a, w, w)


def _merge_kernel(h_ref, g0_ref, g1_ref, g2_ref, g3_ref, y0_ref, y1_ref, y2_ref, y3_ref,
                  wb_ref, o_ref, wbb_ref):
    @pl.when(_first_row_tile())
    def _():
        wbb_ref[...] = wb_ref[...].astype(BF16)

    half = h_ref.shape[0] // 2
    for rows in (slice(0, half), slice(half, 2 * half)):
        h = h_ref[rows, :]
        acc = None
        for n, (g_ref, y_ref) in enumerate(((g0_ref, y0_ref), (g1_ref, y1_ref),
                                            (g2_ref, y2_ref), (g3_ref, y3_ref))):
            logits = jnp.dot(h, g_ref[...], preferred_element_type=F32)
            branch = jnp.dot(y_ref[rows, :], wbb_ref[n], preferred_element_type=F32)
            term = _sigmoid(logits) * branch
            acc = term if acc is None else acc + term
        o_ref[rows, :] = acc.astype(o_ref.dtype)


def _gated_merge(h, w_in, ys, w_branch, l):
    M, K = h.shape
    D = w_branch.shape[3]
    BW = w_branch.shape[2]
    tm, tn = 1024, 512
    nj = D // tn
    assert w_in.dtype == BF16
    gate_specs = [pl.BlockSpec((None, K, tn),
                               functools.partial(lambda j, i, n: (l, 0, n * nj + j), n=n))
                  for n in range(N_BRANCH)]
    y_specs = [pl.BlockSpec((tm, BW), lambda j, i: (i, 0)) for _ in range(N_BRANCH)]
    return pl.pallas_call(
        _merge_kernel,
        grid=(nj, M // tm),
        in_specs=[pl.BlockSpec((tm, K), lambda j, i: (i, 0))] + gate_specs + y_specs
                 + [pl.BlockSpec((None, N_BRANCH, BW, tn), lambda j, i: (l, 0, 0, j))],
        out_specs=pl.BlockSpec((tm, tn), lambda j, i: (i, j)),
        out_shape=jax.ShapeDtypeStruct((M, D), BF16),
        scratch_shapes=[pltpu.VMEM((N_BRANCH, BW, tn), BF16)],
        compiler_params=_params("arbitrary", "arbitrary"),
        name="gated_merge",
    )(h, w_in, w_in, w_in, w_in, *ys, w_branch)


def _lru_kernel(ax_ref, ag_ref, cw_ref, cb_ref, wr_ref, br_ref, wi_ref, bi_ref, lam_ref,
                o_ref, xbuf, hcar):
    i = pl.program_id(0)
    tt = ax_ref.shape[0]

    @pl.when(i == 0)
    def _():
        xbuf[0:SUBLANE, :] = jnp.zeros((SUBLANE, LRU_WIDTH), F32)
        hcar[...] = jnp.zeros_like(hcar)

    xbuf[SUBLANE:SUBLANE + tt, :] = ax_ref[...]
    xa = cb_ref[...]
    for j in range(CONV_WIDTH):
        xa = xa + cw_ref[j:j + 1, :] * xbuf[pl.ds(SUBLANE - (CONV_WIDTH - 1) + j, tt), :]
    xbuf[0:SUBLANE, :] = ax_ref[tt - SUBLANE:tt, :]

    r = _sigmoid(_bdot(xa, wr_ref[...]) + br_ref[...])
    ig = _sigmoid(_bdot(xa, wi_ref[...]) + bi_ref[...])
    log_a = (-LRU_C * _softplus(-lam_ref[...])) * r
    a = jnp.exp(log_a)
    th = jnp.tanh(log_a)
    u = jnp.sqrt(-2.0 * th / (1.0 - th)) * (ig * xa)

    row = lax.broadcasted_iota(jnp.int32, (tt, LRU_WIDTH), 0)
    d = 1
    while d < tt:
        keep = row >= d
        u_s = jnp.where(keep, pltpu.roll(u, d, 0), 0.0)
        a_s = jnp.where(keep, pltpu.roll(a, d, 0), 1.0)
        u = u + a * u_s
        a = a * a_s
        d *= 2
    h = u + a * hcar[...]
    hcar[...] = h[tt - 1:tt, :]

    g = ag_ref[...]
    gelu = 0.5 * g * (1.0 + jnp.tanh(math.sqrt(2.0 / math.pi) * (g + 0.044715 * (g * g * g))))
    o_ref[...] = (h * gelu).astype(o_ref.dtype)


def _lru_branch(p_a, conv_w, conv_b, wr_bd, br, wi_bd, bi, lam):
    T = p_a.shape[0]
    W = LRU_WIDTH
    tt = 256
    row = lambda i: (0, 0)
    return pl.pallas_call(
        _lru_kernel,
        grid=(T // tt,),
        in_specs=[pl.BlockSpec((tt, W), lambda i: (i, 0)),
                  pl.BlockSpec((tt, W), lambda i: (i, 1)),
                  pl.BlockSpec((CONV_WIDTH, W), row),
                  pl.BlockSpec((1, W), row),
                  pl.BlockSpec((W, W), row),
                  pl.BlockSpec((1, W), row),
                  pl.BlockSpec((W, W), row),
                  pl.BlockSpec((1, W), row),
                  pl.BlockSpec((1, W), row)],
        out_specs=pl.BlockSpec((tt, W), lambda i: (i, 0)),
        out_shape=jax.ShapeDtypeStruct((T, W), BF16),
        scratch_shapes=[pltpu.VMEM((tt + SUBLANE, W), F32), pltpu.VMEM((1, W), F32)],
        compiler_params=_params("arbitrary"),
        name="rg_lru",
    )(p_a, p_a, conv_w, conv_b, wr_bd, br, wi_bd, bi, lam)


def _mla_prep_kernel(p_ref, gcq_ref, gckv_ref, wuq_ref, wuk_ref, wuv_ref, gq_ref, gk_ref,
                     cc_ref, sa_ref, sb_ref, q_ref, k_ref, v_ref):
    p = p_ref[...]
    cq = p[:, :MLA_Q_LORA]
    ckv = p[:, MLA_Q_LORA:MLA_Q_LORA + MLA_KV_LORA]
    kr = p[:, MLA_Q_LORA + MLA_KV_LORA:]
    lane = lax.broadcasted_iota(jnp.int32, kr.shape, 1)
    kr = jnp.where(lane < MLA_ROPE_DIM, kr, 0.0)

    def rms(x, g, n):
        return x * lax.rsqrt(jnp.sum(x * x, axis=-1, keepdims=True) * (1.0 / n) + NORM_EPS) * g

    cqn = rms(cq, gcq_ref[...], MLA_Q_LORA)
    ckvn = rms(ckv, gckv_ref[...], MLA_KV_LORA)
    q = _bdot(cqn, wuq_ref[...])
    kn = _bdot(ckvn, wuk_ref[...])
    v_ref[...] = _bdot(ckvn, wuv_ref[...]).astype(v_ref.dtype)
    cc, sa, sb = cc_ref[...], sa_ref[...], sb_ref[...]
    half = MLA_ROPE_DIM // 2

    def rope(x):
        x2 = x[:, LANE:]
        x2 = x2 * cc + pltpu.roll(x2, LANE - half, 1) * sa + pltpu.roll(x2, half, 1) * sb
        return jnp.concatenate([x[:, :LANE], x2], axis=-1)

    scale = MLA_QK_DIM ** -0.5 * math.log2(math.e)
    for h in range(MLA_HEADS):
        qh = rms(q[:, h * MLA_QK_PAD:(h + 1) * MLA_QK_PAD], gq_ref[...], MLA_QK_DIM)
        q_ref[:, h * MLA_QK_PAD:(h + 1) * MLA_QK_PAD] = (rope(qh) * scale).astype(q_ref.dtype)
        kh = jnp.concatenate([kn[:, h * MLA_NOPE_DIM:(h + 1) * MLA_NOPE_DIM], kr], axis=-1)
        kh = rms(kh, gk_ref[...], MLA_QK_DIM)
        k_ref[:, h * MLA_QK_PAD:(h + 1) * MLA_QK_PAD] = rope(kh).astype(k_ref.dtype)


def _mla_prep(p_mla, g_cq, g_ckv, wuq, wuk, wuv, gq, gk, cc, sa, sb):
    T, PW = p_mla.shape
    tt = 512
    QW = MLA_HEADS * MLA_QK_PAD
    VW = MLA_HEADS * MLA_V_DIM
    row = lambda i: (0, 0)
    tab = pl.BlockSpec((tt, LANE), lambda i: (i, 0))
    return pl.pallas_call(
        _mla_prep_kernel,
        grid=(T // tt,),
        in_specs=[pl.BlockSpec((tt, PW), lambda i: (i, 0)),
                  pl.BlockSpec((1, MLA_Q_LORA), row),
                  pl.BlockSpec((1, MLA_KV_LORA), row),
                  pl.BlockSpec((MLA_Q_LORA, QW), row),
                  pl.BlockSpec((MLA_KV_LORA, MLA_HEADS * MLA_NOPE_DIM), row),
                  pl.BlockSpec((MLA_KV_LORA, VW), row),
                  pl.BlockSpec((1, MLA_QK_PAD), row),
                  pl.BlockSpec((1, MLA_QK_PAD), row),
                  tab, tab, tab],
        out_specs=[pl.BlockSpec((tt, QW), lambda i: (i, 0)),
                   pl.BlockSpec((tt, QW), lambda i: (i, 0)),
                   pl.BlockSpec((tt, VW), lambda i: (i, 0))],
        out_shape=[jax.ShapeDtypeStruct((T, QW), BF16),
                   jax.ShapeDtypeStruct((T, QW), BF16),
                   jax.ShapeDtypeStruct((T, VW), BF16)],
        compiler_params=_params("arbitrary"),
        name="mla_prep",
    )(p_mla, g_cq, g_ckv, wuq, wuk, wuv, gq, gk, cc, sa, sb)


ATTN_BLOCK = 512


ATTN_HEADS_PER_STEP = 4
ATTN_ROW_GROUP = 32


def _attn_kernel(q_ref, k_ref, v_ref, o_ref, s_scr, p_scr, m_scr, l_scr, a_scr, acc_scr, *, blk):
    i = pl.program_id(1)
    NH, RG = ATTN_HEADS_PER_STEP, ATTN_ROW_GROUP
    ntile = blk // LANE
    qk = [slice(t * MLA_QK_PAD, (t + 1) * MLA_QK_PAD) for t in range(NH)]
    vd = [slice(t * MLA_V_DIM, (t + 1) * MLA_V_DIM) for t in range(NH)]
    m_scr[...] = jnp.full(m_scr.shape, -jnp.inf, F32)
    l_scr[...] = jnp.zeros(l_scr.shape, F32)
    acc_scr[...] = jnp.zeros(acc_scr.shape, F32)

    def step(j, masked):
        start = pl.multiple_of(j * blk, blk)
        for t in range(NH):
            s_scr[t] = lax.dot_general(q_ref[:, qk[t]], k_ref[pl.ds(start, blk), qk[t]],
                                       (((1,), (1,)), ((), ())), preferred_element_type=F32)
        for t in range(NH):
            for g in range(blk // RG):
                r = slice(g * RG, (g + 1) * RG)
                s = s_scr[t, r, :]
                if masked:
                    row = g * RG + lax.broadcasted_iota(jnp.int32, (RG, blk), 0)
                    col = lax.broadcasted_iota(jnp.int32, (RG, blk), 1)
                    s = jnp.where(col <= row, s, -jnp.inf)
                    s_scr[t, r, :] = s
                m_old = m_scr[t, r, :]
                m_new = jnp.maximum(m_old, jnp.max(s, axis=-1, keepdims=True))
                a_scr[t, r, :] = jnp.exp2(m_old - m_new)
                m_scr[t, r, :] = m_new
            for g in range(blk // RG):
                r = slice(g * RG, (g + 1) * RG)
                m_new = m_scr[t, r, :]
                p = jnp.exp2(s_scr[t, r, :] - jnp.concatenate([m_new] * ntile, axis=1))
                psum = p[:, :LANE]
                for c in range(1, ntile):
                    psum = psum + p[:, c * LANE:(c + 1) * LANE]
                l_scr[t, r, :] = a_scr[t, r, :] * l_scr[t, r, :] + psum
                p_scr[t, r, :] = p.astype(BF16)
        for t in range(NH):
            acc_scr[t] = acc_scr[t] * a_scr[t] + jnp.dot(
                p_scr[t], v_ref[pl.ds(start, blk), vd[t]], preferred_element_type=F32)

    def body(j, carry):
        step(j, False)
        return carry

    lax.fori_loop(0, i, body, 0)
    step(i, True)
    for t in range(NH):
        l = jnp.sum(l_scr[t], axis=-1, keepdims=True)
        o_ref[:, vd[t]] = (acc_scr[t] / l).astype(o_ref.dtype)


def _causal_attention(q, k, v):
    T = q.shape[0]
    blk, NH = ATTN_BLOCK, ATTN_HEADS_PER_STEP
    assert MLA_V_DIM == LANE and MLA_HEADS % NH == 0
    stat = pltpu.VMEM((NH, blk, LANE), F32)
    return pl.pallas_call(
        functools.partial(_attn_kernel, blk=blk),
        grid=(MLA_HEADS // NH, T // blk),
        in_specs=[pl.BlockSpec((blk, NH * MLA_QK_PAD), lambda h, i: (i, h)),
                  pl.BlockSpec((T, NH * MLA_QK_PAD), lambda h, i: (0, h),
                               pipeline_mode=pl.Buffered(1)),
                  pl.BlockSpec((T, NH * MLA_V_DIM), lambda h, i: (0, h),
                               pipeline_mode=pl.Buffered(1))],
        out_specs=pl.BlockSpec((blk, NH * MLA_V_DIM), lambda h, i: (i, h)),
        out_shape=jax.ShapeDtypeStruct((T, MLA_HEADS * MLA_V_DIM), BF16),
        scratch_shapes=[pltpu.VMEM((NH, blk, blk), F32), pltpu.VMEM((NH, blk, blk), BF16),
                        stat, stat, stat, stat],
        compiler_params=_params("arbitrary", "arbitrary"),
        name="mla_attention",
    )(q, k, v)


_RET_LOG_GAMMA = np.log1p(-np.exp(np.linspace(math.log(1.0 / 32), math.log(1.0 / 512),
                                              RET_HEADS, dtype=np.float32))).astype(np.float32)


RET_GROUP = 4


def _ret_kernel(q_ref, k_ref, v_ref, g_ref, cc_ref, ss_ref, gn_ref, o_ref, st_ref):
    i = pl.program_id(0)
    C, Dh, H = RET_CHUNK, RET_HEAD_DIM, RET_HEADS

    @pl.when(i == 0)
    def _():
        st_ref[...] = jnp.zeros_like(st_ref)

    row = lax.broadcasted_iota(jnp.int32, (C, C), 0)
    col = lax.broadcasted_iota(jnp.int32, (C, C), 1)
    rel = (row - col).astype(F32)
    idx = lax.broadcasted_iota(jnp.int32, (C, 1), 0).astype(F32)
    lgs = [float(x) for x in _RET_LOG_GAMMA]
    decay_in = jnp.stack([jnp.where(rel >= 0, jnp.exp(lg * jnp.maximum(rel, 0.0)), 0.0)
                          for lg in lgs])
    q_dec = jnp.stack([jnp.exp(lg * (idx + 1.0)) for lg in lgs])
    k_dec = jnp.stack([jnp.exp(lg * (C - 1.0 - idx)) for lg in lgs])
    heads = [slice(h * Dh, (h + 1) * Dh) for h in range(H)]
    st = st_ref[...]
    for c in range(RET_GROUP):
        rows = slice(c * C, (c + 1) * C)
        cc, ss = cc_ref[rows, :], ss_ref[rows, :]

        def rope(ref):
            x = jnp.stack([ref[rows, sl] for sl in heads])
            return x * cc + pltpu.roll(x, Dh // 2, 2) * ss

        q = rope(q_ref)
        k = rope(k_ref) * (Dh ** -0.5)
        v = jnp.stack([v_ref[rows, sl] for sl in heads])
        inner = _bmm(q, k, _BNT) * decay_in
        o = _bmm(inner, v, _BNN) + _bmm(q * q_dec, st, _BNN)
        kv = _bmm(k * k_dec, v, _BTN)
        st = jnp.stack([math.exp(lg * C) * st[h] + kv[h] for h, lg in enumerate(lgs)])
        oc = o - jnp.mean(o, axis=-1, keepdims=True)
        on = oc * lax.rsqrt(jnp.mean(oc * oc, axis=-1, keepdims=True) + GN_EPS)
        for h, sl in enumerate(heads):
            g = g_ref[rows, sl]
            o_ref[rows, sl] = (g * _sigmoid(g) * (on[h] * gn_ref[:, sl])).astype(o_ref.dtype)
    st_ref[...] = st


def _retention_branch(p_ret, cc, ss, g_norm):
    T = p_ret.shape[0]
    C, W = RET_CHUNK * RET_GROUP, RET_WIDTH
    blk = lambda n: pl.BlockSpec((C, W), functools.partial(lambda i, n: (i, n), n=n))
    return pl.pallas_call(
        _ret_kernel,
        grid=(T // C,),
        in_specs=[blk(0), blk(1), blk(2), blk(3),
                  pl.BlockSpec((C, LANE), lambda i: (i, 0)),
                  pl.BlockSpec((C, LANE), lambda i: (i, 0)),
                  pl.BlockSpec((1, W), lambda i: (0, 0))],
        out_specs=pl.BlockSpec((C, W), lambda i: (i, 0)),
        out_shape=jax.ShapeDtypeStruct((T, W), BF16),
        scratch_shapes=[pltpu.VMEM((RET_HEADS, RET_HEAD_DIM, RET_HEAD_DIM), F32)],
        compiler_params=_params("arbitrary"),
        name="retention",
    )(p_ret, p_ret, p_ret, p_ret, cc, ss, g_norm)


RWKV_PAD_COLS = 3 * RWKV_WIDTH + 3 * LANE


def _split_dot(a, b_exact, terms):
    acc = None
    rem = a
    for _ in range(terms):
        piece = rem.astype(BF16)
        rem = rem - piece.astype(F32)
        d = jnp.dot(piece, b_exact, preferred_element_type=F32)
        acc = d if acc is None else acc + d
    return acc


def _rwkv_prep_kernel(p_ref, mu_ref, w0_ref, a0_ref, wup_ref, aup_ref, gup_ref, kk_ref, ka_ref,
                      rk_ref, ones_ref, tri_ref,
                      kp_ref, rp_ref, kn_ref, bn_ref, knp_ref, bnp_ref, v_ref, g_ref, bonus_ref,
                      pc_ref, buf):
    i = pl.program_id(0)
    tt = p_ref.shape[0]
    W = RWKV_WIDTH
    C = RWKV_CHUNK

    @pl.when(i == 0)
    def _():
        buf[0:SUBLANE, :] = jnp.zeros((SUBLANE, RWKV_PAD_COLS), F32)

    p = p_ref[...]
    buf[SUBLANE:SUBLANE + tt, :] = p
    prev = buf[pl.ds(SUBLANE - 1, tt), :]
    buf[0:SUBLANE, :] = p_ref[tt - SUBLANE:tt, :]
    xs = p + (prev - p) * mu_ref[...]
    r = xs[:, 0:W]
    k = xs[:, W:2 * W]
    v = xs[:, 2 * W:3 * W]
    wd = xs[:, 3 * W:3 * W + LANE]
    ad = xs[:, 3 * W + LANE:3 * W + 2 * LANE]
    gd = xs[:, 3 * W + 2 * LANE:3 * W + 3 * LANE]

    w_log = -_softplus(-(w0_ref[...] + _fdot(jnp.tanh(wd), wup_ref[...]))) - 0.5
    logw = -jnp.exp(w_log)
    a = _sigmoid(a0_ref[...] + _bdot(ad, aup_ref[...]))
    g_ref[...] = _bdot(_sigmoid(gd), gup_ref[...])

    ones_bd = ones_ref[...]
    kk = k * kk_ref[...]
    kk = kk / jnp.maximum(jnp.sqrt(_split_dot(kk * kk, ones_bd, 2)), 1e-12)
    kt = k * (1.0 + (a - 1.0) * ka_ref[...])
    bonus_ref[...] = _split_dot(r * kt * rk_ref[...], ones_bd, 2) * v
    v_ref[...] = v.astype(v_ref.dtype)

    tri = tri_ref[...]
    cum = jnp.concatenate([_split_dot_lhs_exact(tri, logw[c * C:(c + 1) * C], 3)
                           for c in range(tt // C)], axis=0)
    tot = jnp.sum(logw.reshape(tt // C, C, W), axis=1)
    pc_ref[...] = jnp.exp(tot)
    to_end = jnp.exp((tot[:, None, :] - cum.reshape(tt // C, C, W)).reshape(tt, W))
    e_neg = jnp.exp(-cum)
    beta = kk * a
    kp_ref[...] = (kk * jnp.exp(cum - logw)).astype(kp_ref.dtype)
    rp_ref[...] = (r * jnp.exp(cum)).astype(rp_ref.dtype)
    kn_ref[...] = (kt * e_neg).astype(kn_ref.dtype)
    bn_ref[...] = (beta * e_neg).astype(bn_ref.dtype)
    knp_ref[...] = (kt * to_end).astype(knp_ref.dtype)
    bnp_ref[...] = (beta * to_end).astype(bnp_ref.dtype)


def _split_dot_lhs_exact(a_exact, b, terms):
    acc = None
    rem = b
    for _ in range(terms):
        piece = rem.astype(BF16)
        rem = rem - piece.astype(F32)
        d = jnp.dot(a_exact, piece, preferred_element_type=F32)
        acc = d if acc is None else acc + d
    return acc


def _rwkv_prep(p_rw, mu, w0, a0, wup, aup, gup, k_k, k_a, r_k):
    T = p_rw.shape[0]
    W, C = RWKV_WIDTH, RWKV_CHUNK
    tt = 512
    hid = np.arange(W) // RWKV_HEAD_DIM
    ones_bd = jnp.asarray((hid[:, None] == hid[None, :]).astype(np.float32), dtype=BF16)
    tid = np.arange(C)
    tri = jnp.asarray((tid[:, None] >= tid[None, :]).astype(np.float32), dtype=BF16)
    row = lambda i: (0, 0)
    vec = pl.BlockSpec((1, W), row)
    lora = pl.BlockSpec((LANE, W), row)
    out = pl.BlockSpec((tt, W), lambda i: (i, 0))
    lo = jax.ShapeDtypeStruct((T, W), BF16)
    hi = jax.ShapeDtypeStruct((T, W), F32)
    return pl.pallas_call(
        _rwkv_prep_kernel,
        grid=(T // tt,),
        in_specs=[pl.BlockSpec((tt, RWKV_PAD_COLS), lambda i: (i, 0)),
                  pl.BlockSpec((1, RWKV_PAD_COLS), row),
                  vec, vec, lora, lora, lora, vec, vec, vec,
                  pl.BlockSpec((W, W), row),
                  pl.BlockSpec((C, C), row)],
        out_specs=[out] * 9 + [pl.BlockSpec((tt // C, W), lambda i: (i, 0))],
        out_shape=[lo] * 7 + [hi, hi, jax.ShapeDtypeStruct((T // C, W), F32)],
        scratch_shapes=[pltpu.VMEM((tt + SUBLANE, RWKV_PAD_COLS), F32)],
        compiler_params=_params("arbitrary"),
        name="rwkv_prep",
    )(p_rw, mu, w0, a0, wup, aup, gup, k_k, k_a, r_k, ones_bd, tri)


def _bmm(a, b, dims):
    return lax.dot_general(a.astype(BF16), b.astype(BF16), dims, preferred_element_type=F32)


_BNN = (((2,), (1,)), ((0,), (0,)))
_BNT = (((2,), (2,)), ((0,), (0,)))
_BTN = (((1,), (1,)), ((0,), (0,)))
RWKV_GROUP = 4


def _rwkv_rec_kernel(kp_ref, rp_ref, kn_ref, bn_ref, knp_ref, bnp_ref, v_ref, pc_ref, gn_ref,
                     bonus_ref, g_ref, y_ref, s_ref, *, nchunk):
    C = RWKV_CHUNK
    P = 2 * C
    assert P == LANE

    @pl.when(pl.program_id(0) == 0)
    def _():
        s_ref[...] = jnp.zeros_like(s_ref)

    row = lax.broadcasted_iota(jnp.int32, (P, P), 0)
    col = lax.broadcasted_iota(jnp.int32, (P, P), 1)

    def blocks(n):
        return jnp.where((row // n) == (col // n), 1.0, 0.0)

    own = blocks(C)
    own_bf = own.astype(BF16)
    lower = jnp.where(row > col, 1.0, 0.0)
    m_strict = own * lower
    m_incl = own * jnp.where(row >= col, 1.0, 0.0)
    m_base = blocks(16) * lower
    m_l32 = blocks(32) * lower - m_base
    m_l64 = m_strict - m_base - m_l32
    eye = jnp.where(row == col, 1.0, 0.0)

    npair = RWKV_HEADS // 2
    lanes = [slice(p * LANE, (p + 1) * LANE) for p in range(npair)]
    G = RWKV_GROUP

    def group(gi, carry):
        rows = [pl.ds(pl.multiple_of((gi * G + c) * C, C), C) for c in range(G)]

        def load(ref):
            x = jnp.stack([ref[sl, ln] for sl in rows for ln in lanes])
            return jnp.concatenate([x, x], axis=1) * own_bf

        kp, rp, kn, bn, knp, bnp, v = (load(r) for r in (kp_ref, rp_ref, kn_ref, bn_ref,
                                                         knp_ref, bnp_ref, v_ref))
        q2 = jnp.concatenate([kp, rp], axis=1)
        a_kn = _bmm(q2, kn, _BNT)
        a_bn = _bmm(q2, bn, _BNT)
        a_k = a_kn[:, :P] * m_strict
        a_rk = a_kn[:, P:] * m_incl
        a_b = a_bn[:, :P]
        a_rb = a_bn[:, P:] * m_incl

        n1 = -(a_b * m_base)
        inv = eye + n1
        n2 = _bmm(n1, n1, _BNN)
        inv = inv + _bmm(inv, n2, _BNN)
        n4 = _bmm(n2, n2, _BNN)
        inv = inv + _bmm(inv, n4, _BNN)
        n8 = _bmm(n4, n4, _BNN)
        inv = inv + _bmm(inv, n8, _BNN)
        for msk in (m_l32, m_l64):
            inv = inv - _bmm(inv, _bmm(a_b * msk, inv, _BNN), _BNN)
        av = _bmm(jnp.concatenate([a_k, a_rk], axis=1), v, _BNN)
        k2 = jnp.concatenate([knp, bnp], axis=1)

        s = s_ref[...]
        inv_n = 1.0 / RWKV_HEAD_DIM
        for c in range(G):
            b = slice(c * npair, (c + 1) * npair)
            x1 = _bmm(q2[b], s, _BNT)
            u = _bmm(inv[b], x1[:, :P] + av[b, :P], _BNN)
            o = x1[:, P:] + av[b, P:] - _bmm(a_rb[b], u, _BNN)
            vu = jnp.concatenate([v[b], (-u).astype(BF16)], axis=1)
            pc = pc_ref[gi * G + c]
            pc = jnp.stack([pc[:, ln] for ln in lanes])
            s = s * pc + _bmm(vu, k2[b], _BTN)

            oc = (o - jnp.sum(o, axis=-1, keepdims=True) * inv_n) * own
            on = oc * lax.rsqrt(jnp.sum(oc * oc, axis=-1, keepdims=True) * inv_n + GN_EPS)
            on = on[:, :C] + on[:, C:]
            for p, ln in enumerate(lanes):
                y = (on[p] * gn_ref[:, ln] + bonus_ref[rows[c], ln]) * g_ref[rows[c], ln]
                y_ref[rows[c], ln] = y.astype(y_ref.dtype)
        s_ref[...] = s
        return carry

    lax.fori_loop(0, nchunk // G, group, 0)


def _rwkv_recurrence(kp, rp, kn, bn, knp, bnp, v, pc, gn, bonus, g):
    T, W = kp.shape
    C = RWKV_CHUNK
    tb = 512
    nchunk = tb // C
    blk = pl.BlockSpec((tb, W), lambda i: (i, 0))
    return pl.pallas_call(
        functools.partial(_rwkv_rec_kernel, nchunk=nchunk),
        grid=(T // tb,),
        in_specs=[blk] * 7 + [pl.BlockSpec((nchunk, 1, W), lambda i: (i, 0, 0)),
                              pl.BlockSpec((1, W), lambda i: (0, 0)), blk, blk],
        out_specs=blk,
        out_shape=jax.ShapeDtypeStruct((T, W), BF16),
        scratch_shapes=[pltpu.VMEM((RWKV_HEADS // 2, LANE, LANE), F32)],
        compiler_params=_params("arbitrary"),
        name="rwkv_recurrence",
    )(kp, rp, kn, bn, knp, bnp, v, pc, gn, bonus, g)


def _rwkv_branch(p_rw, mu, w0, a0, wup, aup, gup, k_k, k_a, r_k, g_norm):
    kp, rp, kn, bn, knp, bnp, v, g, bonus, pc = _rwkv_prep(p_rw, mu, w0, a0, wup, aup, gup,
                                                          k_k, k_a, r_k)
    pc = pc.reshape(pc.shape[0], 1, pc.shape[1])
    return _rwkv_recurrence(kp, rp, kn, bn, knp, bnp, v, pc, g_norm.reshape(1, -1), bonus, g)


def _pad_cols(w, n):
    return jnp.pad(w, ((0, 0), (0, n - w.shape[1])))


def _block_diag(w):
    n, bi, bj = w.shape
    eye = jnp.eye(n, dtype=w.dtype)
    return (eye[:, None, :, None] * w[:, :, None, :]).reshape(n * bi, n * bj)


def _rope_tables(positions):
    pos = positions.astype(F32).reshape(-1, 1)
    T = pos.shape[0]

    def cs(dim):
        inv = 1.0 / (ROPE_BASE ** (jnp.arange(0, dim, 2, dtype=F32) / dim))
        ang = pos * inv
        return jnp.cos(ang), jnp.sin(ang)

    cm, sm = cs(MLA_ROPE_DIM)
    z32 = jnp.zeros((T, MLA_ROPE_DIM // 2), F32)
    z64 = jnp.zeros((T, LANE - MLA_ROPE_DIM), F32)
    mla = (jnp.concatenate([cm, cm, z64], axis=1),
           jnp.concatenate([-sm, z32, z64], axis=1),
           jnp.concatenate([z32, sm, z64], axis=1))
    cr, sr = cs(RET_HEAD_DIM)
    ret = (jnp.concatenate([cr, cr], axis=1), jnp.concatenate([-sr, sr], axis=1))
    return mla, ret


def _mla_weights(w_uq, w_ukv, g_qn, g_kn):
    wq = w_uq.reshape(MLA_Q_LORA, MLA_HEADS, MLA_QK_DIM)
    wq = jnp.pad(wq, ((0, 0), (0, 0), (0, MLA_QK_PAD - MLA_QK_DIM)))
    wq = wq.reshape(MLA_Q_LORA, MLA_HEADS * MLA_QK_PAD)
    wkv = w_ukv.reshape(MLA_KV_LORA, MLA_HEADS, MLA_NOPE_DIM + MLA_V_DIM)
    wk = wkv[:, :, :MLA_NOPE_DIM].reshape(MLA_KV_LORA, MLA_HEADS * MLA_NOPE_DIM)
    wv = wkv[:, :, MLA_NOPE_DIM:].reshape(MLA_KV_LORA, MLA_HEADS * MLA_V_DIM)
    pad = MLA_QK_PAD - MLA_QK_DIM
    gq = jnp.pad(g_qn, (0, pad)).reshape(1, MLA_QK_PAD)
    gk = jnp.pad(g_kn, (0, pad)).reshape(1, MLA_QK_PAD)
    return wq.astype(BF16), wk.astype(BF16), wv.astype(BF16), gq, gk


def _pad_rows(w, n):
    return jnp.pad(w, ((0, n - w.shape[0]), (0, 0)))


def kernel(x, c, positions, ada_w, ada_b, norm_mix, norm_ffn, w_in, conv_w, conv_b, lru_wr, lru_br, lru_wi, lru_bi, lru_lam, mla_g_cq, mla_g_ckv, mla_w_uq, mla_w_ukv, mla_g_qn, mla_g_kn, ret_g_norm, rwkv_mu, rwkv_w0, rwkv_w_up, rwkv_a0, rwkv_a_up, rwkv_g_up, rwkv_k_k, rwkv_k_a, rwkv_r_k, rwkv_g_norm, w_branch, w_out, ffn_w_in, ffn_w_out):
    B, T, D = x.shape
    assert B == 1 and D == D_MODEL
    depth = ada_w.shape[0]
    xt = x.reshape(T, D)
    mod_all = _modulation(c, ada_w, ada_b)
    (cc_m, sa_m, sb_m), (cc_r, ss_r) = _rope_tables(positions)
    w_in_bf = w_in.astype(BF16)
    ffn_w_out_bf = ffn_w_out.astype(BF16)

    o_a = GATE_COLS
    o_cq = o_a + 2 * LRU_WIDTH
    o_ckv = o_cq + MLA_Q_LORA
    o_kr = o_ckv + MLA_KV_LORA
    o_ret = o_kr + MLA_ROPE_DIM
    o_rw = o_ret + 4 * RET_WIDTH
    o_lora = o_rw + 3 * RWKV_WIDTH
    W = RWKV_WIDTH

    for l in range(depth):
        mod = mod_all[l]
        w_ret = w_in_bf[l, :, o_ret:o_rw]
        dl, al = RWKV_DECAY_LORA, RWKV_AAA_LORA
        w_rw = jnp.concatenate([w_in_bf[l, :, o_rw:o_lora],
                                _pad_cols(w_in_bf[l, :, o_lora:o_lora + dl], LANE),
                                _pad_cols(w_in_bf[l, :, o_lora + dl:o_lora + dl + al], LANE),
                                w_in_bf[l, :, o_lora + dl + al:]], axis=1)
        mu = rwkv_mu[l]
        mu_p = jnp.concatenate([mu[:3 * W], jnp.pad(mu[3 * W:3 * W + dl], (0, LANE - dl)),
                                jnp.pad(mu[3 * W + dl:3 * W + dl + al], (0, LANE - al)),
                                mu[3 * W + dl + al:]]).reshape(1, RWKV_PAD_COLS)

        h = _mod_norm(xt, norm_mix[l].reshape(1, D), mod, 0)
        p_a = _ws_matmul(h, w_in_bf, l, o_a, 2 * LRU_WIDTH, 512, "in_proj_lru")
        p_mla = _ws_matmul(h, w_in_bf, l, o_cq, o_ret - o_cq + MLA_ROPE_DIM, 768, "in_proj_mla")
        p_ret = _matmul(h, w_ret, F32, "in_proj_ret")
        p_rw = _matmul(h, w_rw, F32, "in_proj_rwkv")

        y_a = _lru_branch(p_a, conv_w[l], conv_b[l].reshape(1, -1),
                          _block_diag(lru_wr[l]).astype(BF16), lru_br[l].reshape(1, -1),
                          _block_diag(lru_wi[l]).astype(BF16), lru_bi[l].reshape(1, -1),
                          lru_lam[l].reshape(1, -1))

        wq, wk, wv, gq, gk = _mla_weights(mla_w_uq[l], mla_w_ukv[l], mla_g_qn[l], mla_g_kn[l])
        q, k, v = _mla_prep(p_mla, mla_g_cq[l].reshape(1, -1), mla_g_ckv[l].reshape(1, -1),
                            wq, wk, wv, gq, gk, cc_m, sa_m, sb_m)
        y_b = _causal_attention(q, k, v)

        y_c = _retention_branch(p_ret, cc_r, ss_r, ret_g_norm[l].reshape(1, -1))

        y_d = _rwkv_branch(p_rw, mu_p, rwkv_w0[l].reshape(1, -1), rwkv_a0[l].reshape(1, -1),
                           _pad_rows(rwkv_w_up[l], LANE), _pad_rows(rwkv_a_up[l], LANE),
                           rwkv_g_up[l], rwkv_k_k[l].reshape(1, -1), rwkv_k_a[l].reshape(1, -1),
                           rwkv_r_k[l].reshape(1, -1), rwkv_g_norm[l])

        merged = _gated_merge(h, w_in_bf, (y_a, y_b, y_c, y_d), w_branch, l)
        xt = _matmul_gated_residual(merged, w_out, l, xt, mod, 2, 1024, 1024, "out_proj")

        h2 = _mod_norm(xt, norm_ffn[l].reshape(1, D), mod, 3)
        act = _swiglu_in(h2, ffn_w_in, l)
        xt = _matmul_gated_residual(act, ffn_w_out_bf, l, xt, mod, 5, 512, 1024, "ffn_out")
    return xt.reshape(B, T, D)
```

```python
import functools
import math

import numpy as np
import jax
import jax.numpy as jnp
from jax import lax
from jax.experimental import pallas as pl
from jax.experimental.pallas import tpu as pltpu

F32 = jnp.float32
BF16 = jnp.bfloat16
HIGHEST = lax.Precision.HIGHEST

D_MODEL = 2048
N_BRANCH = 4
BRANCH_WIDTH = D_MODEL // N_BRANCH
NORM_EPS = 1e-6
GN_EPS = 1e-5
ROPE_BASE = 10000.0
LRU_WIDTH = BRANCH_WIDTH
LRU_C = 8.0
CONV_WIDTH = 4
MLA_HEADS = 4
MLA_NOPE_DIM = 128
MLA_ROPE_DIM = 64
MLA_V_DIM = 128
MLA_QK_DIM = MLA_NOPE_DIM + MLA_ROPE_DIM
MLA_QK_PAD = 256
MLA_Q_LORA = 384
MLA_KV_LORA = 256
RET_HEADS = 4
RET_HEAD_DIM = 128
RET_WIDTH = RET_HEADS * RET_HEAD_DIM
RET_CHUNK = 128
RWKV_HEAD_DIM = 64
RWKV_HEADS = 8
RWKV_WIDTH = RWKV_HEADS * RWKV_HEAD_DIM
RWKV_DECAY_LORA = 64
RWKV_AAA_LORA = 64
RWKV_GATE_LORA = 128
RWKV_CHUNK = 64
FFN_HIDDEN = 5632
GATE_COLS = N_BRANCH * D_MODEL
LANE = 128
SUBLANE = 8
VMEM_LIMIT_BYTES = 56 * 1024 * 1024


def _params(*sem):
    return pltpu.CompilerParams(dimension_semantics=sem, vmem_limit_bytes=VMEM_LIMIT_BYTES)


def _sigmoid(x):
    return 1.0 / (1.0 + jnp.exp(-x))


def _softplus(x):
    return jnp.maximum(x, 0.0) + jnp.log(1.0 + jnp.exp(-jnp.abs(x)))


def _bdot(a, b):
    return jnp.dot(a.astype(BF16), b.astype(BF16), preferred_element_type=F32)


def _fdot(a, b):
    return jnp.dot(a, b, preferred_element_type=F32, precision=HIGHEST)


def _mod_kernel(c_ref, w_ref, b_ref, o_ref):
    c = c_ref[...]
    ca = c * _sigmoid(c)
    o_ref[0] = jnp.sum(ca * w_ref[0], axis=0, keepdims=True) + b_ref[0]


def _modulation(c, ada_w, ada_b):
    L, D, N = ada_w.shape
    tn = 1024
    return pl.pallas_call(
        _mod_kernel,
        grid=(L, N // tn),
        in_specs=[pl.BlockSpec((D, 1), lambda l, j: (0, 0)),
                  pl.BlockSpec((1, D, tn), lambda l, j: (l, 0, j)),
                  pl.BlockSpec((1, 1, tn), lambda l, j: (l, 0, j))],
        out_specs=pl.BlockSpec((1, 1, tn), lambda l, j: (l, 0, j)),
        out_shape=jax.ShapeDtypeStruct((L, 1, N), F32),
        compiler_params=_params("arbitrary", "arbitrary"),
        name="adaln_mod",
    )(c.reshape(D, 1), ada_w, ada_b.reshape(L, 1, N))


def _norm_kernel(x_ref, g_ref, sh_ref, sc_ref, o_ref):
    x = x_ref[...]
    ms = jnp.mean(x * x, axis=-1, keepdims=True)
    y = x * lax.rsqrt(ms + NORM_EPS) * g_ref[...]
    o_ref[...] = (y * (1.0 + sc_ref[...]) + sh_ref[...]).astype(o_ref.dtype)


def _mod_norm(x, g, mod, shift_idx):
    T, D = x.shape
    tm = 512
    return pl.pallas_call(
        _norm_kernel,
        grid=(T // tm,),
        in_specs=[pl.BlockSpec((tm, D), lambda i: (i, 0)),
                  pl.BlockSpec((1, D), lambda i: (0, 0)),
                  pl.BlockSpec((1, D), lambda i: (0, shift_idx)),
                  pl.BlockSpec((1, D), lambda i: (0, shift_idx + 1))],
        out_specs=pl.BlockSpec((tm, D), lambda i: (i, 0)),
        out_shape=jax.ShapeDtypeStruct((T, D), BF16),
        compiler_params=_params("arbitrary"),
        name="mod_norm",
    )(x, g, mod, mod)


def _mm_kernel(a_ref, b_ref, o_ref):
    o_ref[...] = jnp.dot(a_ref[...], b_ref[...], preferred_element_type=F32).astype(o_ref.dtype)


def _matmul(a, b, out_dtype, name):
    M, K = a.shape
    N = b.shape[1]
    tm = 1024
    return pl.pallas_call(
        _mm_kernel,
        grid=(M // tm,),
        in_specs=[pl.BlockSpec((tm, K), lambda i: (i, 0)),
                  pl.BlockSpec((K, N), lambda i: (0, 0))],
        out_specs=pl.BlockSpec((tm, N), lambda i: (i, 0)),
        out_shape=jax.ShapeDtypeStruct((M, N), out_dtype),
        compiler_params=_params("arbitrary"),
        name=name,
    )(a, b)


def _first_row_tile():
    return pl.program_id(1) == 0


def _ws_mm_t_kernel(a_ref, wt_ref, o_ref, wb_ref):
    @pl.when(_first_row_tile())
    def _():
        wb_ref[...] = wt_ref[...].T.astype(BF16)

    o_ref[...] = jnp.dot(a_ref[...], wb_ref[...], preferred_element_type=F32)


def _ws_matmul_t(a, wt, l, col0, ncols, tn, name):
    M, K = a.shape
    tm = 1024
    jb = col0 // tn
    assert col0 % tn == 0 and ncols % tn == 0
    return pl.pallas_call(
        _ws_mm_t_kernel,
        grid=(ncols // tn, M // tm),
        in_specs=[pl.BlockSpec((tm, K), lambda j, i: (i, 0)),
                  pl.BlockSpec((None, tn, K), lambda j, i: (l, jb + j, 0))],
        out_specs=pl.BlockSpec((tm, tn), lambda j, i: (i, j)),
        out_shape=jax.ShapeDtypeStruct((M, ncols), F32),
        scratch_shapes=[pltpu.VMEM((K, tn), BF16)],
        compiler_params=_params("arbitrary", "arbitrary"),
        name=name,
    )(a, wt)


def _ws_mm_res_kernel(a_ref, w_ref, x_ref, g_ref, o_ref, wb_ref):
    @pl.when(_first_row_tile())
    def _():
        wb_ref[...] = w_ref[...].astype(BF16)

    half = a_ref.shape[0] // 2
    for rows in (slice(0, half), slice(half, 2 * half)):
        acc = jnp.dot(a_ref[rows, :], wb_ref[...], preferred_element_type=F32)
        o_ref[rows, :] = x_ref[rows, :] + g_ref[...] * acc


def _mm_res_kernel(a_ref, w_ref, x_ref, g_ref, o_ref):
    acc = jnp.dot(a_ref[...], w_ref[...], preferred_element_type=F32)
    o_ref[...] = x_ref[...] + g_ref[...] * acc


def _matmul_gated_residual(a, w, l, x, mod, gate_idx, tm, tn, name):
    M, K = a.shape
    N = w.shape[2]
    nj = N // tn
    cast = w.dtype != BF16
    return pl.pallas_call(
        _ws_mm_res_kernel if cast else _mm_res_kernel,
        grid=(nj, M // tm),
        in_specs=[pl.BlockSpec((tm, K), lambda j, i: (i, 0)),
                  pl.BlockSpec((None, K, tn), lambda j, i: (l, 0, j)),
                  pl.BlockSpec((tm, tn), lambda j, i: (i, j)),
                  pl.BlockSpec((1, tn), lambda j, i: (0, gate_idx * nj + j))],
        out_specs=pl.BlockSpec((tm, tn), lambda j, i: (i, j)),
        out_shape=jax.ShapeDtypeStruct((M, N), F32),
        scratch_shapes=[pltpu.VMEM((K, tn), BF16)] if cast else [],
        compiler_params=_params("arbitrary", "arbitrary"),
        name=name,
    )(a, w, x, mod)


def _swiglu_kernel(a_ref, wg_ref, wv_ref, o_ref, wgb_ref, wvb_ref):
    @pl.when(_first_row_tile())
    def _():
        wgb_ref[...] = wg_ref[...].astype(BF16)
        wvb_ref[...] = wv_ref[...].astype(BF16)

    half = a_ref.shape[0] // 2
    for rows in (slice(0, half), slice(half, 2 * half)):
        a = a_ref[rows, :]
        ug = jnp.dot(a, wgb_ref[...], preferred_element_type=F32)
        uv = jnp.dot(a, wvb_ref[...], preferred_element_type=F32)
        o_ref[rows, :] = (ug * _sigmoid(ug) * uv).astype(o_ref.dtype)


def _swiglu_in(a, w, l):
    M, K = a.shape
    H = w.shape[2] // 2
    tm, tn = 2048, 512
    nj = H // tn
    return pl.pallas_call(
        _swiglu_kernel,
        grid=(nj, M // tm),
        in_specs=[pl.BlockSpec((tm, K), lambda j, i: (i, 0)),
                  pl.BlockSpec((None, K, tn), lambda j, i: (l, 0, j)),
                  pl.BlockSpec((None, K, tn), lambda j, i: (l, 0, nj + j))],
        out_specs=pl.BlockSpec((tm, tn), lambda j, i: (i, j)),
        out_shape=jax.ShapeDtypeStruct((M, H), BF16),
        scratch_shapes=[pltpu.VMEM((K, tn), BF16), pltpu.VMEM((K, tn), BF16)],
        compiler_params=_params("arbitrary", "arbitrary"),
        name="ffn_in_swiglu",
    )(a, w, w)


def _merge_kernel(h_ref, g0_ref, g1_ref, g2_ref, g3_ref, y0_ref, y1_ref, y2_ref, y3_ref,
                  wb_ref, o_ref, gb_ref, wbb_ref):
    @pl.when(_first_row_tile())
    def _():
        for n, gt_ref in enumerate((g0_ref, g1_ref, g2_ref, g3_ref)):
            gb_ref[n] = gt_ref[...].T.astype(BF16)
        wbb_ref[...] = wb_ref[...].astype(BF16)

    half = h_ref.shape[0] // 2
    for rows in (slice(0, half), slice(half, 2 * half)):
        h = h_ref[rows, :]
        acc = None
        for n, y_ref in enumerate((y0_ref, y1_ref, y2_ref, y3_ref)):
            logits = jnp.dot(h, gb_ref[n], preferred_element_type=F32)
            branch = jnp.dot(y_ref[rows, :], wbb_ref[n], preferred_element_type=F32)
            term = _sigmoid(logits) * branch
            acc = term if acc is None else acc + term
        o_ref[rows, :] = acc.astype(o_ref.dtype)


def _gated_merge(h, wt, ys, w_branch, l):
    M, K = h.shape
    D = w_branch.shape[3]
    BW = w_branch.shape[2]
    tm, tn = 1024, 256
    nj = D // tn
    gate_specs = [pl.BlockSpec((None, tn, K),
                               functools.partial(lambda j, i, n: (l, n * nj + j, 0), n=n))
                  for n in range(N_BRANCH)]
    y_specs = [pl.BlockSpec((tm, BW), lambda j, i: (i, 0)) for _ in range(N_BRANCH)]
    return pl.pallas_call(
        _merge_kernel,
        grid=(nj, M // tm),
        in_specs=[pl.BlockSpec((tm, K), lambda j, i: (i, 0))] + gate_specs + y_specs
                 + [pl.BlockSpec((None, N_BRANCH, BW, tn), lambda j, i: (l, 0, 0, j))],
        out_specs=pl.BlockSpec((tm, tn), lambda j, i: (i, j)),
        out_shape=jax.ShapeDtypeStruct((M, D), BF16),
        scratch_shapes=[pltpu.VMEM((N_BRANCH, K, tn), BF16),
                        pltpu.VMEM((N_BRANCH, BW, tn), BF16)],
        compiler_params=_params("arbitrary", "arbitrary"),
        name="gated_merge",
    )(h, wt, wt, wt, wt, *ys, w_branch)


def _lru_kernel(ax_ref, ag_ref, cw_ref, cb_ref, wr_ref, br_ref, wi_ref, bi_ref, lam_ref,
                o_ref, xbuf, hcar):
    i = pl.program_id(0)
    tt = ax_ref.shape[0]

    @pl.when(i == 0)
    def _():
        xbuf[0:SUBLANE, :] = jnp.zeros((SUBLANE, LRU_WIDTH), F32)
        hcar[...] = jnp.zeros_like(hcar)

    xbuf[SUBLANE:SUBLANE + tt, :] = ax_ref[...]
    xa = cb_ref[...]
    for j in range(CONV_WIDTH):
        xa = xa + cw_ref[j:j + 1, :] * xbuf[pl.ds(SUBLANE - (CONV_WIDTH - 1) + j, tt), :]
    xbuf[0:SUBLANE, :] = ax_ref[tt - SUBLANE:tt, :]

    r = _sigmoid(_bdot(xa, wr_ref[...]) + br_ref[...])
    ig = _sigmoid(_bdot(xa, wi_ref[...]) + bi_ref[...])
    log_a = (-LRU_C * _softplus(-lam_ref[...])) * r
    a = jnp.exp(log_a)
    th = jnp.tanh(log_a)
    u = jnp.sqrt(-2.0 * th / (1.0 - th)) * (ig * xa)

    row = lax.broadcasted_iota(jnp.int32, (tt, LRU_WIDTH), 0)
    d = 1
    while d < tt:
        keep = row >= d
        u_s = jnp.where(keep, pltpu.roll(u, d, 0), 0.0)
        a_s = jnp.where(keep, pltpu.roll(a, d, 0), 1.0)
        u = u + a * u_s
        a = a * a_s
        d *= 2
    h = u + a * hcar[...]
    hcar[...] = h[tt - 1:tt, :]

    g = ag_ref[...]
    gelu = 0.5 * g * (1.0 + jnp.tanh(math.sqrt(2.0 / math.pi) * (g + 0.044715 * (g * g * g))))
    o_ref[...] = (h * gelu).astype(o_ref.dtype)


def _lru_branch(p_a, conv_w, conv_b, wr_bd, br, wi_bd, bi, lam):
    T = p_a.shape[0]
    W = LRU_WIDTH
    tt = 256
    row = lambda i: (0, 0)
    return pl.pallas_call(
        _lru_kernel,
        grid=(T // tt,),
        in_specs=[pl.BlockSpec((tt, W), lambda i: (i, 0)),
                  pl.BlockSpec((tt, W), lambda i: (i, 1)),
                  pl.BlockSpec((CONV_WIDTH, W), row),
                  pl.BlockSpec((1, W), row),
                  pl.BlockSpec((W, W), row),
                  pl.BlockSpec((1, W), row),
                  pl.BlockSpec((W, W), row),
                  pl.BlockSpec((1, W), row),
                  pl.BlockSpec((1, W), row)],
        out_specs=pl.BlockSpec((tt, W), lambda i: (i, 0)),
        out_shape=jax.ShapeDtypeStruct((T, W), BF16),
        scratch_shapes=[pltpu.VMEM((tt + SUBLANE, W), F32), pltpu.VMEM((1, W), F32)],
        compiler_params=_params("arbitrary"),
        name="rg_lru",
    )(p_a, p_a, conv_w, conv_b, wr_bd, br, wi_bd, bi, lam)


def _mla_prep_kernel(p_ref, gcq_ref, gckv_ref, wuq_ref, wuk_ref, wuv_ref, gq_ref, gk_ref,
                     cc_ref, sa_ref, sb_ref, q_ref, k_ref, v_ref):
    p = p_ref[...]
    cq = p[:, :MLA_Q_LORA]
    ckv = p[:, MLA_Q_LORA:MLA_Q_LORA + MLA_KV_LORA]
    kr = p[:, MLA_Q_LORA + MLA_KV_LORA:]
    lane = lax.broadcasted_iota(jnp.int32, kr.shape, 1)
    kr = jnp.where(lane < MLA_ROPE_DIM, kr, 0.0)

    def rms(x, g, n):
        return x * lax.rsqrt(jnp.sum(x * x, axis=-1, keepdims=True) * (1.0 / n) + NORM_EPS) * g

    cqn = rms(cq, gcq_ref[...], MLA_Q_LORA)
    ckvn = rms(ckv, gckv_ref[...], MLA_KV_LORA)
    q = _bdot(cqn, wuq_ref[...])
    kn = _bdot(ckvn, wuk_ref[...])
    v_ref[...] = _bdot(ckvn, wuv_ref[...]).astype(v_ref.dtype)
    cc, sa, sb = cc_ref[...], sa_ref[...], sb_ref[...]
    half = MLA_ROPE_DIM // 2

    def rope(x):
        x2 = x[:, LANE:]
        x2 = x2 * cc + pltpu.roll(x2, LANE - half, 1) * sa + pltpu.roll(x2, half, 1) * sb
        return jnp.concatenate([x[:, :LANE], x2], axis=-1)

    scale = MLA_QK_DIM ** -0.5 * math.log2(math.e)
    for h in range(MLA_HEADS):
        qh = rms(q[:, h * MLA_QK_PAD:(h + 1) * MLA_QK_PAD], gq_ref[...], MLA_QK_DIM)
        q_ref[:, h * MLA_QK_PAD:(h + 1) * MLA_QK_PAD] = (rope(qh) * scale).astype(q_ref.dtype)
        kh = jnp.concatenate([kn[:, h * MLA_NOPE_DIM:(h + 1) * MLA_NOPE_DIM], kr], axis=-1)
        kh = rms(kh, gk_ref[...], MLA_QK_DIM)
        k_ref[:, h * MLA_QK_PAD:(h + 1) * MLA_QK_PAD] = rope(kh).astype(k_ref.dtype)


def _mla_prep(p_mla, g_cq, g_ckv, wuq, wuk, wuv, gq, gk, cc, sa, sb):
    T, PW = p_mla.shape
    tt = 512
    QW = MLA_HEADS * MLA_QK_PAD
    VW = MLA_HEADS * MLA_V_DIM
    row = lambda i: (0, 0)
    tab = pl.BlockSpec((tt, LANE), lambda i: (i, 0))
    return pl.pallas_call(
        _mla_prep_kernel,
        grid=(T // tt,),
        in_specs=[pl.BlockSpec((tt, PW), lambda i: (i, 0)),
                  pl.BlockSpec((1, MLA_Q_LORA), row),
                  pl.BlockSpec((1, MLA_KV_LORA), row),
                  pl.BlockSpec((MLA_Q_LORA, QW), row),
                  pl.BlockSpec((MLA_KV_LORA, MLA_HEADS * MLA_NOPE_DIM), row),
                  pl.BlockSpec((MLA_KV_LORA, VW), row),
                  pl.BlockSpec((1, MLA_QK_PAD), row),
                  pl.BlockSpec((1, MLA_QK_PAD), row),
                  tab, tab, tab],
        out_specs=[pl.BlockSpec((tt, QW), lambda i: (i, 0)),
                   pl.BlockSpec((tt, QW), lambda i: (i, 0)),
                   pl.BlockSpec((tt, VW), lambda i: (i, 0))],
        out_shape=[jax.ShapeDtypeStruct((T, QW), BF16),
                   jax.ShapeDtypeStruct((T, QW), BF16),
                   jax.ShapeDtypeStruct((T, VW), BF16)],
        compiler_params=_params("arbitrary"),
        name="mla_prep",
    )(p_mla, g_cq, g_ckv, wuq, wuk, wuv, gq, gk, cc, sa, sb)


ATTN_BLOCK = 512


ATTN_HEADS_PER_STEP = 4
ATTN_ROW_GROUP = 32


def _attn_kernel(q_ref, k_ref, v_ref, o_ref, s_scr, p_scr, m_scr, l_scr, a_scr, acc_scr, *, blk):
    i = pl.program_id(1)
    NH, RG = ATTN_HEADS_PER_STEP, ATTN_ROW_GROUP
    ntile = blk // LANE
    qk = [slice(t * MLA_QK_PAD, (t + 1) * MLA_QK_PAD) for t in range(NH)]
    vd = [slice(t * MLA_V_DIM, (t + 1) * MLA_V_DIM) for t in range(NH)]
    m_scr[...] = jnp.full(m_scr.shape, -jnp.inf, F32)
    l_scr[...] = jnp.zeros(l_scr.shape, F32)
    acc_scr[...] = jnp.zeros(acc_scr.shape, F32)

    def step(j, masked):
        start = pl.multiple_of(j * blk, blk)
        for t in range(NH):
            s_scr[t] = lax.dot_general(q_ref[:, qk[t]], k_ref[pl.ds(start, blk), qk[t]],
                                       (((1,), (1,)), ((), ())), preferred_element_type=F32)
        for t in range(NH):
            for g in range(blk // RG):
                r = slice(g * RG, (g + 1) * RG)
                s = s_scr[t, r, :]
                if masked:
                    row = g * RG + lax.broadcasted_iota(jnp.int32, (RG, blk), 0)
                    col = lax.broadcasted_iota(jnp.int32, (RG, blk), 1)
                    s = jnp.where(col <= row, s, -jnp.inf)
                    s_scr[t, r, :] = s
                m_old = m_scr[t, r, :]
                m_new = jnp.maximum(m_old, jnp.max(s, axis=-1, keepdims=True))
                a_scr[t, r, :] = jnp.exp2(m_old - m_new)
                m_scr[t, r, :] = m_new
            for g in range(blk // RG):
                r = slice(g * RG, (g + 1) * RG)
                m_new = m_scr[t, r, :]
                p = jnp.exp2(s_scr[t, r, :] - jnp.concatenate([m_new] * ntile, axis=1))
                psum = p[:, :LANE]
                for c in range(1, ntile):
                    psum = psum + p[:, c * LANE:(c + 1) * LANE]
                l_scr[t, r, :] = a_scr[t, r, :] * l_scr[t, r, :] + psum
                p_scr[t, r, :] = p.astype(BF16)
        for t in range(NH):
            acc_scr[t] = acc_scr[t] * a_scr[t] + jnp.dot(
                p_scr[t], v_ref[pl.ds(start, blk), vd[t]], preferred_element_type=F32)

    def body(j, carry):
        step(j, False)
        return carry

    lax.fori_loop(0, i, body, 0)
    step(i, True)
    for t in range(NH):
        l = jnp.sum(l_scr[t], axis=-1, keepdims=True)
        o_ref[:, vd[t]] = (acc_scr[t] / l).astype(o_ref.dtype)


def _causal_attention(q, k, v):
    T = q.shape[0]
    blk, NH = ATTN_BLOCK, ATTN_HEADS_PER_STEP
    assert MLA_V_DIM == LANE and MLA_HEADS % NH == 0
    stat = pltpu.VMEM((NH, blk, LANE), F32)
    return pl.pallas_call(
        functools.partial(_attn_kernel, blk=blk),
        grid=(MLA_HEADS // NH, T // blk),
        in_specs=[pl.BlockSpec((blk, NH * MLA_QK_PAD), lambda h, i: (i, h)),
                  pl.BlockSpec((T, NH * MLA_QK_PAD), lambda h, i: (0, h),
                               pipeline_mode=pl.Buffered(1)),
                  pl.BlockSpec((T, NH * MLA_V_DIM), lambda h, i: (0, h),
                               pipeline_mode=pl.Buffered(1))],
        out_specs=pl.BlockSpec((blk, NH * MLA_V_DIM), lambda h, i: (i, h)),
        out_shape=jax.ShapeDtypeStruct((T, MLA_HEADS * MLA_V_DIM), BF16),
        scratch_shapes=[pltpu.VMEM((NH, blk, blk), F32), pltpu.VMEM((NH, blk, blk), BF16),
                        stat, stat, stat, stat],
        compiler_params=_params("arbitrary", "arbitrary"),
        name="mla_attention",
    )(q, k, v)


_RET_LOG_GAMMA = np.log1p(-np.exp(np.linspace(math.log(1.0 / 32), math.log(1.0 / 512),
                                              RET_HEADS, dtype=np.float32))).astype(np.float32)


RET_GROUP = 4


def _ret_kernel(q_ref, k_ref, v_ref, g_ref, cc_ref, ss_ref, gn_ref, o_ref, st_ref):
    i = pl.program_id(0)
    C, Dh, H = RET_CHUNK, RET_HEAD_DIM, RET_HEADS

    @pl.when(i == 0)
    def _():
        st_ref[...] = jnp.zeros_like(st_ref)

    row = lax.broadcasted_iota(jnp.int32, (C, C), 0)
    col = lax.broadcasted_iota(jnp.int32, (C, C), 1)
    rel = (row - col).astype(F32)
    idx = lax.broadcasted_iota(jnp.int32, (C, 1), 0).astype(F32)
    lgs = [float(x) for x in _RET_LOG_GAMMA]
    decay_in = jnp.stack([jnp.where(rel >= 0, jnp.exp(lg * jnp.maximum(rel, 0.0)), 0.0)
                          for lg in lgs])
    q_dec = jnp.stack([jnp.exp(lg * (idx + 1.0)) for lg in lgs])
    k_dec = jnp.stack([jnp.exp(lg * (C - 1.0 - idx)) for lg in lgs])
    heads = [slice(h * Dh, (h + 1) * Dh) for h in range(H)]
    st = st_ref[...]
    for c in range(RET_GROUP):
        rows = slice(c * C, (c + 1) * C)
        cc, ss = cc_ref[rows, :], ss_ref[rows, :]

        def rope(ref):
            x = jnp.stack([ref[rows, sl] for sl in heads])
            return x * cc + pltpu.roll(x, Dh // 2, 2) * ss

        q = rope(q_ref)
        k = rope(k_ref) * (Dh ** -0.5)
        v = jnp.stack([v_ref[rows, sl] for sl in heads])
        inner = _bmm(q, k, _BNT) * decay_in
        o = _bmm(inner, v, _BNN) + _bmm(q * q_dec, st, _BNN)
        kv = _bmm(k * k_dec, v, _BTN)
        st = jnp.stack([math.exp(lg * C) * st[h] + kv[h] for h, lg in enumerate(lgs)])
        oc = o - jnp.mean(o, axis=-1, keepdims=True)
        on = oc * lax.rsqrt(jnp.mean(oc * oc, axis=-1, keepdims=True) + GN_EPS)
        for h, sl in enumerate(heads):
            g = g_ref[rows, sl]
            o_ref[rows, sl] = (g * _sigmoid(g) * (on[h] * gn_ref[:, sl])).astype(o_ref.dtype)
    st_ref[...] = st


def _retention_branch(p_ret, cc, ss, g_norm):
    T = p_ret.shape[0]
    C, W = RET_CHUNK * RET_GROUP, RET_WIDTH
    blk = lambda n: pl.BlockSpec((C, W), functools.partial(lambda i, n: (i, n), n=n))
    return pl.pallas_call(
        _ret_kernel,
        grid=(T // C,),
        in_specs=[blk(0), blk(1), blk(2), blk(3),
                  pl.BlockSpec((C, LANE), lambda i: (i, 0)),
                  pl.BlockSpec((C, LANE), lambda i: (i, 0)),
                  pl.BlockSpec((1, W), lambda i: (0, 0))],
        out_specs=pl.BlockSpec((C, W), lambda i: (i, 0)),
        out_shape=jax.ShapeDtypeStruct((T, W), BF16),
        scratch_shapes=[pltpu.VMEM((RET_HEADS, RET_HEAD_DIM, RET_HEAD_DIM), F32)],
        compiler_params=_params("arbitrary"),
        name="retention",
    )(p_ret, p_ret, p_ret, p_ret, cc, ss, g_norm)


RWKV_PAD_COLS = 3 * RWKV_WIDTH + 3 * LANE


def _split_dot(a, b_exact, terms):
    acc = None
    rem = a
    for _ in range(terms):
        piece = rem.astype(BF16)
        rem = rem - piece.astype(F32)
        d = jnp.dot(piece, b_exact, preferred_element_type=F32)
        acc = d if acc is None else acc + d
    return acc


def _rwkv_prep_kernel(p_ref, mu_ref, w0_ref, a0_ref, wup_ref, aup_ref, gup_ref, kk_ref, ka_ref,
                      rk_ref, ones_ref, tri_ref,
                      kp_ref, rp_ref, kn_ref, bn_ref, knp_ref, bnp_ref, v_ref, g_ref, bonus_ref,
                      pc_ref, buf):
    i = pl.program_id(0)
    tt = p_ref.shape[0]
    W = RWKV_WIDTH
    C = RWKV_CHUNK

    @pl.when(i == 0)
    def _():
        buf[0:SUBLANE, :] = jnp.zeros((SUBLANE, RWKV_PAD_COLS), F32)

    p = p_ref[...]
    buf[SUBLANE:SUBLANE + tt, :] = p
    prev = buf[pl.ds(SUBLANE - 1, tt), :]
    buf[0:SUBLANE, :] = p_ref[tt - SUBLANE:tt, :]
    xs = p + (prev - p) * mu_ref[...]
    r = xs[:, 0:W]
    k = xs[:, W:2 * W]
    v = xs[:, 2 * W:3 * W]
    wd = xs[:, 3 * W:3 * W + LANE]
    ad = xs[:, 3 * W + LANE:3 * W + 2 * LANE]
    gd = xs[:, 3 * W + 2 * LANE:3 * W + 3 * LANE]

    w_log = -_softplus(-(w0_ref[...] + _fdot(jnp.tanh(wd), wup_ref[...]))) - 0.5
    logw = -jnp.exp(w_log)
    a = _sigmoid(a0_ref[...] + _bdot(ad, aup_ref[...]))
    g_ref[...] = _bdot(_sigmoid(gd), gup_ref[...])

    ones_bd = ones_ref[...]
    kk = k * kk_ref[...]
    kk = kk / jnp.maximum(jnp.sqrt(_split_dot(kk * kk, ones_bd, 2)), 1e-12)
    kt = k * (1.0 + (a - 1.0) * ka_ref[...])
    bonus_ref[...] = _split_dot(r * kt * rk_ref[...], ones_bd, 2) * v
    v_ref[...] = v.astype(v_ref.dtype)

    tri = tri_ref[...]
    cum = jnp.concatenate([_split_dot_lhs_exact(tri, logw[c * C:(c + 1) * C], 3)
                           for c in range(tt // C)], axis=0)
    tot = jnp.sum(logw.reshape(tt // C, C, W), axis=1)
    pc_ref[...] = jnp.exp(tot)
    to_end = jnp.exp((tot[:, None, :] - cum.reshape(tt // C, C, W)).reshape(tt, W))
    e_neg = jnp.exp(-cum)
    beta = kk * a
    kp_ref[...] = (kk * jnp.exp(cum - logw)).astype(kp_ref.dtype)
    rp_ref[...] = (r * jnp.exp(cum)).astype(rp_ref.dtype)
    kn_ref[...] = (kt * e_neg).astype(kn_ref.dtype)
    bn_ref[...] = (beta * e_neg).astype(bn_ref.dtype)
    knp_ref[...] = (kt * to_end).astype(knp_ref.dtype)
    bnp_ref[...] = (beta * to_end).astype(bnp_ref.dtype)


def _split_dot_lhs_exact(a_exact, b, terms):
    acc = None
    rem = b
    for _ in range(terms):
        piece = rem.astype(BF16)
        rem = rem - piece.astype(F32)
        d = jnp.dot(a_exact, piece, preferred_element_type=F32)
        acc = d if acc is None else acc + d
    return acc


def _rwkv_prep(p_rw, mu, w0, a0, wup, aup, gup, k_k, k_a, r_k):
    T = p_rw.shape[0]
    W, C = RWKV_WIDTH, RWKV_CHUNK
    tt = 512
    hid = np.arange(W) // RWKV_HEAD_DIM
    ones_bd = jnp.asarray((hid[:, None] == hid[None, :]).astype(np.float32), dtype=BF16)
    tid = np.arange(C)
    tri = jnp.asarray((tid[:, None] >= tid[None, :]).astype(np.float32), dtype=BF16)
    row = lambda i: (0, 0)
    vec = pl.BlockSpec((1, W), row)
    lora = pl.BlockSpec((LANE, W), row)
    out = pl.BlockSpec((tt, W), lambda i: (i, 0))
    lo = jax.ShapeDtypeStruct((T, W), BF16)
    hi = jax.ShapeDtypeStruct((T, W), F32)
    return pl.pallas_call(
        _rwkv_prep_kernel,
        grid=(T // tt,),
        in_specs=[pl.BlockSpec((tt, RWKV_PAD_COLS), lambda i: (i, 0)),
                  pl.BlockSpec((1, RWKV_PAD_COLS), row),
                  vec, vec, lora, lora, lora, vec, vec, vec,
                  pl.BlockSpec((W, W), row),
                  pl.BlockSpec((C, C), row)],
        out_specs=[out] * 9 + [pl.BlockSpec((tt // C, W), lambda i: (i, 0))],
        out_shape=[lo] * 7 + [hi, hi, jax.ShapeDtypeStruct((T // C, W), F32)],
        scratch_shapes=[pltpu.VMEM((tt + SUBLANE, RWKV_PAD_COLS), F32)],
        compiler_params=_params("arbitrary"),
        name="rwkv_prep",
    )(p_rw, mu, w0, a0, wup, aup, gup, k_k, k_a, r_k, ones_bd, tri)


def _bmm(a, b, dims):
    return lax.dot_general(a.astype(BF16), b.astype(BF16), dims, preferred_element_type=F32)


_BNN = (((2,), (1,)), ((0,), (0,)))
_BNT = (((2,), (2,)), ((0,), (0,)))
_BTN = (((1,), (1,)), ((0,), (0,)))
RWKV_GROUP = 4


def _rwkv_rec_kernel(kp_ref, rp_ref, kn_ref, bn_ref, knp_ref, bnp_ref, v_ref, pc_ref, gn_ref,
                     bonus_ref, g_ref, y_ref, s_ref, *, nchunk):
    C = RWKV_CHUNK
    P = 2 * C
    assert P == LANE

    @pl.when(pl.program_id(0) == 0)
    def _():
        s_ref[...] = jnp.zeros_like(s_ref)

    row = lax.broadcasted_iota(jnp.int32, (P, P), 0)
    col = lax.broadcasted_iota(jnp.int32, (P, P), 1)

    def blocks(n):
        return jnp.where((row // n) == (col // n), 1.0, 0.0)

    own = blocks(C)
    own_bf = own.astype(BF16)
    lower = jnp.where(row > col, 1.0, 0.0)
    m_strict = own * lower
    m_incl = own * jnp.where(row >= col, 1.0, 0.0)
    m_base = blocks(16) * lower
    m_l32 = blocks(32) * lower - m_base
    m_l64 = m_strict - m_base - m_l32
    eye = jnp.where(row == col, 1.0, 0.0)

    npair = RWKV_HEADS // 2
    lanes = [slice(p * LANE, (p + 1) * LANE) for p in range(npair)]
    G = RWKV_GROUP

    def group(gi, carry):
        rows = [pl.ds(pl.multiple_of((gi * G + c) * C, C), C) for c in range(G)]

        def load(ref):
            x = jnp.stack([ref[sl, ln] for sl in rows for ln in lanes])
            return jnp.concatenate([x, x], axis=1) * own_bf

        kp, rp, kn, bn, knp, bnp, v = (load(r) for r in (kp_ref, rp_ref, kn_ref, bn_ref,
                                                         knp_ref, bnp_ref, v_ref))
        q2 = jnp.concatenate([kp, rp], axis=1)
        a_kn = _bmm(q2, kn, _BNT)
        a_bn = _bmm(q2, bn, _BNT)
        a_k = a_kn[:, :P] * m_strict
        a_rk = a_kn[:, P:] * m_incl
        a_b = a_bn[:, :P]
        a_rb = a_bn[:, P:] * m_incl

        n1 = -(a_b * m_base)
        inv = eye + n1
        n2 = _bmm(n1, n1, _BNN)
        inv = inv + _bmm(inv, n2, _BNN)
        n4 = _bmm(n2, n2, _BNN)
        inv = inv + _bmm(inv, n4, _BNN)
        n8 = _bmm(n4, n4, _BNN)
        inv = inv + _bmm(inv, n8, _BNN)
        for msk in (m_l32, m_l64):
            inv = inv - _bmm(inv, _bmm(a_b * msk, inv, _BNN), _BNN)
        av = _bmm(jnp.concatenate([a_k, a_rk], axis=1), v, _BNN)
        k2 = jnp.concatenate([knp, bnp], axis=1)

        s = s_ref[...]
        inv_n = 1.0 / RWKV_HEAD_DIM
        for c in range(G):
            b = slice(c * npair, (c + 1) * npair)
            x1 = _bmm(q2[b], s, _BNT)
            u = _bmm(inv[b], x1[:, :P] + av[b, :P], _BNN)
            o = x1[:, P:] + av[b, P:] - _bmm(a_rb[b], u, _BNN)
            vu = jnp.concatenate([v[b], (-u).astype(BF16)], axis=1)
            pc = pc_ref[gi * G + c]
            pc = jnp.stack([pc[:, ln] for ln in lanes])
            s = s * pc + _bmm(vu, k2[b], _BTN)

            oc = (o - jnp.sum(o, axis=-1, keepdims=True) * inv_n) * own
            on = oc * lax.rsqrt(jnp.sum(oc * oc, axis=-1, keepdims=True) * inv_n + GN_EPS)
            on = on[:, :C] + on[:, C:]
            for p, ln in enumerate(lanes):
                y = (on[p] * gn_ref[:, ln] + bonus_ref[rows[c], ln]) * g_ref[rows[c], ln]
                y_ref[rows[c], ln] = y.astype(y_ref.dtype)
        s_ref[...] = s
        return carry

    lax.fori_loop(0, nchunk // G, group, 0)


def _rwkv_recurrence(kp, rp, kn, bn, knp, bnp, v, pc, gn, bonus, g):
    T, W = kp.shape
    C = RWKV_CHUNK
    tb = 512
    nchunk = tb // C
    blk = pl.BlockSpec((tb, W), lambda i: (i, 0))
    return pl.pallas_call(
        functools.partial(_rwkv_rec_kernel, nchunk=nchunk),
        grid=(T // tb,),
        in_specs=[blk] * 7 + [pl.BlockSpec((nchunk, 1, W), lambda i: (i, 0, 0)),
                              pl.BlockSpec((1, W), lambda i: (0, 0)), blk, blk],
        out_specs=blk,
        out_shape=jax.ShapeDtypeStruct((T, W), BF16),
        scratch_shapes=[pltpu.VMEM((RWKV_HEADS // 2, LANE, LANE), F32)],
        compiler_params=_params("arbitrary"),
        name="rwkv_recurrence",
    )(kp, rp, kn, bn, knp, bnp, v, pc, gn, bonus, g)


def _rwkv_branch(p_rw, mu, w0, a0, wup, aup, gup, k_k, k_a, r_k, g_norm):
    kp, rp, kn, bn, knp, bnp, v, g, bonus, pc = _rwkv_prep(p_rw, mu, w0, a0, wup, aup, gup,
                                                          k_k, k_a, r_k)
    pc = pc.reshape(pc.shape[0], 1, pc.shape[1])
    return _rwkv_recurrence(kp, rp, kn, bn, knp, bnp, v, pc, g_norm.reshape(1, -1), bonus, g)


def _pad_cols(w, n):
    return jnp.pad(w, ((0, 0), (0, n - w.shape[1])))


def _block_diag(w):
    n, bi, bj = w.shape
    eye = jnp.eye(n, dtype=w.dtype)
    return (eye[:, None, :, None] * w[:, :, None, :]).reshape(n * bi, n * bj)


def _rope_tables(positions):
    pos = positions.astype(F32).reshape(-1, 1)
    T = pos.shape[0]

    def cs(dim):
        inv = 1.0 / (ROPE_BASE ** (jnp.arange(0, dim, 2, dtype=F32) / dim))
        ang = pos * inv
        return jnp.cos(ang), jnp.sin(ang)

    cm, sm = cs(MLA_ROPE_DIM)
    z32 = jnp.zeros((T, MLA_ROPE_DIM // 2), F32)
    z64 = jnp.zeros((T, LANE - MLA_ROPE_DIM), F32)
    mla = (jnp.concatenate([cm, cm, z64], axis=1),
           jnp.concatenate([-sm, z32, z64], axis=1),
           jnp.concatenate([z32, sm, z64], axis=1))
    cr, sr = cs(RET_HEAD_DIM)
    ret = (jnp.concatenate([cr, cr], axis=1), jnp.concatenate([-sr, sr], axis=1))
    return mla, ret


def _mla_weights(w_uq, w_ukv, g_qn, g_kn):
    wq = w_uq.reshape(MLA_Q_LORA, MLA_HEADS, MLA_QK_DIM)
    wq = jnp.pad(wq, ((0, 0), (0, 0), (0, MLA_QK_PAD - MLA_QK_DIM)))
    wq = wq.reshape(MLA_Q_LORA, MLA_HEADS * MLA_QK_PAD)
    wkv = w_ukv.reshape(MLA_KV_LORA, MLA_HEADS, MLA_NOPE_DIM + MLA_V_DIM)
    wk = wkv[:, :, :MLA_NOPE_DIM].reshape(MLA_KV_LORA, MLA_HEADS * MLA_NOPE_DIM)
    wv = wkv[:, :, MLA_NOPE_DIM:].reshape(MLA_KV_LORA, MLA_HEADS * MLA_V_DIM)
    pad = MLA_QK_PAD - MLA_QK_DIM
    gq = jnp.pad(g_qn, (0, pad)).reshape(1, MLA_QK_PAD)
    gk = jnp.pad(g_kn, (0, pad)).reshape(1, MLA_QK_PAD)
    return wq.astype(BF16), wk.astype(BF16), wv.astype(BF16), gq, gk


def _pad_rows(w, n):
    return jnp.pad(w, ((0, n - w.shape[0]), (0, 0)))


def kernel(x, c, positions, ada_w, ada_b, norm_mix, norm_ffn, w_in, conv_w, conv_b, lru_wr, lru_br, lru_wi, lru_bi, lru_lam, mla_g_cq, mla_g_ckv, mla_w_uq, mla_w_ukv, mla_g_qn, mla_g_kn, ret_g_norm, rwkv_mu, rwkv_w0, rwkv_w_up, rwkv_a0, rwkv_a_up, rwkv_g_up, rwkv_k_k, rwkv_k_a, rwkv_r_k, rwkv_g_norm, w_branch, w_out, ffn_w_in, ffn_w_out):
    B, T, D = x.shape
    assert B == 1 and D == D_MODEL
    depth = ada_w.shape[0]
    xt = x.reshape(T, D)
    mod_all = _modulation(c, ada_w, ada_b)
    (cc_m, sa_m, sb_m), (cc_r, ss_r) = _rope_tables(positions)
    w_in_t = jnp.swapaxes(w_in, 1, 2)
    ffn_w_out_bf = ffn_w_out.astype(BF16)

    o_a = GATE_COLS
    o_cq = o_a + 2 * LRU_WIDTH
    o_ckv = o_cq + MLA_Q_LORA
    o_kr = o_ckv + MLA_KV_LORA
    o_ret = o_kr + MLA_ROPE_DIM
    o_rw = o_ret + 4 * RET_WIDTH
    o_lora = o_rw + 3 * RWKV_WIDTH
    W = RWKV_WIDTH

    for l in range(depth):
        mod = mod_all[l]
        dl, al = RWKV_DECAY_LORA, RWKV_AAA_LORA
        w_ret_t = w_in_t[l:l + 1, o_ret:o_rw]
        w_rw_t = jnp.concatenate([w_in_t[l, o_rw:o_lora],
                                  _pad_rows(w_in_t[l, o_lora:o_lora + dl], LANE),
                                  _pad_rows(w_in_t[l, o_lora + dl:o_lora + dl + al], LANE),
                                  w_in_t[l, o_lora + dl + al:]], axis=0)[None]
        mu = rwkv_mu[l]
        mu_p = jnp.concatenate([mu[:3 * W], jnp.pad(mu[3 * W:3 * W + dl], (0, LANE - dl)),
                                jnp.pad(mu[3 * W + dl:3 * W + dl + al], (0, LANE - al)),
                                mu[3 * W + dl + al:]]).reshape(1, RWKV_PAD_COLS)

        h = _mod_norm(xt, norm_mix[l].reshape(1, D), mod, 0)
        p_a = _ws_matmul_t(h, w_in_t, l, o_a, 2 * LRU_WIDTH, 512, "in_proj_lru")
        p_mla = _ws_matmul_t(h, w_in_t, l, o_cq, o_ret - o_cq + MLA_ROPE_DIM, 768, "in_proj_mla")
        p_ret = _ws_matmul_t(h, w_ret_t, 0, 0, 4 * RET_WIDTH, 512, "in_proj_ret")
        p_rw = _ws_matmul_t(h, w_rw_t, 0, 0, RWKV_PAD_COLS, 640, "in_proj_rwkv")

        y_a = _lru_branch(p_a, conv_w[l], conv_b[l].reshape(1, -1),
                          _block_diag(lru_wr[l]).astype(BF16), lru_br[l].reshape(1, -1),
                          _block_diag(lru_wi[l]).astype(BF16), lru_bi[l].reshape(1, -1),
                          lru_lam[l].reshape(1, -1))

        wq, wk, wv, gq, gk = _mla_weights(mla_w_uq[l], mla_w_ukv[l], mla_g_qn[l], mla_g_kn[l])
        q, k, v = _mla_prep(p_mla, mla_g_cq[l].reshape(1, -1), mla_g_ckv[l].reshape(1, -1),
                            wq, wk, wv, gq, gk, cc_m, sa_m, sb_m)
        y_b = _causal_attention(q, k, v)

        y_c = _retention_branch(p_ret, cc_r, ss_r, ret_g_norm[l].reshape(1, -1))

        y_d = _rwkv_branch(p_rw, mu_p, rwkv_w0[l].reshape(1, -1), rwkv_a0[l].reshape(1, -1),
                           _pad_rows(rwkv_w_up[l], LANE), _pad_rows(rwkv_a_up[l], LANE),
                           rwkv_g_up[l], rwkv_k_k[l].reshape(1, -1), rwkv_k_a[l].reshape(1, -1),
                           rwkv_r_k[l].reshape(1, -1), rwkv_g_norm[l])

        merged = _gated_merge(h, w_in_t, (y_a, y_b, y_c, y_d), w_branch, l)
        xt = _matmul_gated_residual(merged, w_out, l, xt, mod, 2, 1024, 1024, "out_proj")

        h2 = _mod_norm(xt, norm_ffn[l].reshape(1, D), mod, 3)
        act = _swiglu_in(h2, ffn_w_in, l)
        xt = _matmul_gated_residual(act, ffn_w_out_bf, l, xt, mod, 5, 512, 1024, "ffn_out")
    return xt.reshape(B, T, D)
```

```python
import functools
import math

import numpy as np
import jax
import jax.numpy as jnp
from jax import lax
from jax.experimental import pallas as pl
from jax.experimental.pallas import tpu as pltpu

F32 = jnp.float32
BF16 = jnp.bfloat16
HIGHEST = lax.Precision.HIGHEST

D_MODEL = 2048
N_BRANCH = 4
BRANCH_WIDTH = D_MODEL // N_BRANCH
NORM_EPS = 1e-6
GN_EPS = 1e-5
ROPE_BASE = 10000.0
LRU_WIDTH = BRANCH_WIDTH
LRU_C = 8.0
CONV_WIDTH = 4
MLA_HEADS = 4
MLA_NOPE_DIM = 128
MLA_ROPE_DIM = 64
MLA_V_DIM = 128
MLA_QK_DIM = MLA_NOPE_DIM + MLA_ROPE_DIM
MLA_QK_PAD = 256
MLA_Q_LORA = 384
MLA_KV_LORA = 256
RET_HEADS = 4
RET_HEAD_DIM = 128
RET_WIDTH = RET_HEADS * RET_HEAD_DIM
RET_CHUNK = 128
RWKV_HEAD_DIM = 64
RWKV_HEADS = 8
RWKV_WIDTH = RWKV_HEADS * RWKV_HEAD_DIM
RWKV_DECAY_LORA = 64
RWKV_AAA_LORA = 64
RWKV_GATE_LORA = 128
RWKV_CHUNK = 64
FFN_HIDDEN = 5632
GATE_COLS = N_BRANCH * D_MODEL
LANE = 128
SUBLANE = 8
VMEM_LIMIT_BYTES = 56 * 1024 * 1024


def _params(*sem):
    return pltpu.CompilerParams(dimension_semantics=sem, vmem_limit_bytes=VMEM_LIMIT_BYTES)


def _sigmoid(x):
    return 1.0 / (1.0 + jnp.exp(-x))


def _softplus(x):
    return jnp.maximum(x, 0.0) + jnp.log(1.0 + jnp.exp(-jnp.abs(x)))


def _bdot(a, b):
    return jnp.dot(a.astype(BF16), b.astype(BF16), preferred_element_type=F32)


def _fdot(a, b):
    return jnp.dot(a, b, preferred_element_type=F32, precision=HIGHEST)


def _mod_kernel(c_ref, w_ref, b_ref, o_ref):
    c = c_ref[...]
    ca = c * _sigmoid(c)
    o_ref[0] = jnp.sum(ca * w_ref[0], axis=0, keepdims=True) + b_ref[0]


def _modulation(c, ada_w, ada_b):
    L, D, N = ada_w.shape
    tn = 1024
    return pl.pallas_call(
        _mod_kernel,
        grid=(L, N // tn),
        in_specs=[pl.BlockSpec((D, 1), lambda l, j: (0, 0)),
                  pl.BlockSpec((1, D, tn), lambda l, j: (l, 0, j)),
                  pl.BlockSpec((1, 1, tn), lambda l, j: (l, 0, j))],
        out_specs=pl.BlockSpec((1, 1, tn), lambda l, j: (l, 0, j)),
        out_shape=jax.ShapeDtypeStruct((L, 1, N), F32),
        compiler_params=_params("arbitrary", "arbitrary"),
        name="adaln_mod",
    )(c.reshape(D, 1), ada_w, ada_b.reshape(L, 1, N))


def _norm_kernel(x_ref, g_ref, sh_ref, sc_ref, o_ref):
    x = x_ref[...]
    ms = jnp.mean(x * x, axis=-1, keepdims=True)
    y = x * lax.rsqrt(ms + NORM_EPS) * g_ref[...]
    o_ref[...] = (y * (1.0 + sc_ref[...]) + sh_ref[...]).astype(o_ref.dtype)


def _mod_norm(x, g, mod, shift_idx):
    T, D = x.shape
    tm = 512
    return pl.pallas_call(
        _norm_kernel,
        grid=(T // tm,),
        in_specs=[pl.BlockSpec((tm, D), lambda i: (i, 0)),
                  pl.BlockSpec((1, D), lambda i: (0, 0)),
                  pl.BlockSpec((1, D), lambda i: (0, shift_idx)),
                  pl.BlockSpec((1, D), lambda i: (0, shift_idx + 1))],
        out_specs=pl.BlockSpec((tm, D), lambda i: (i, 0)),
        out_shape=jax.ShapeDtypeStruct((T, D), BF16),
        compiler_params=_params("arbitrary"),
        name="mod_norm",
    )(x, g, mod, mod)


_CONTRACT_LAST = (((1,), (1,)), ((), ()))


def _mm_nt_kernel(a_ref, wt_ref, o_ref):
    o_ref[...] = lax.dot_general(a_ref[...], wt_ref[...], _CONTRACT_LAST,
                                 preferred_element_type=F32).astype(o_ref.dtype)


def _matmul_nt(a, wt, name):
    M, K = a.shape
    N = wt.shape[0]
    tm = 1024
    return pl.pallas_call(
        _mm_nt_kernel,
        grid=(M // tm,),
        in_specs=[pl.BlockSpec((tm, K), lambda i: (i, 0)),
                  pl.BlockSpec((N, K), lambda i: (0, 0))],
        out_specs=pl.BlockSpec((tm, N), lambda i: (i, 0)),
        out_shape=jax.ShapeDtypeStruct((M, N), F32),
        compiler_params=_params("arbitrary"),
        name=name,
    )(a, wt)


def _first_row_tile():
    return pl.program_id(1) == 0


def _ws_matmul_nt(a, wt, l, col0, ncols, tn, name):
    M, K = a.shape
    tm = 1024
    jb = col0 // tn
    assert col0 % tn == 0 and ncols % tn == 0 and wt.dtype == BF16
    return pl.pallas_call(
        _mm_nt_kernel,
        grid=(ncols // tn, M // tm),
        in_specs=[pl.BlockSpec((tm, K), lambda j, i: (i, 0)),
                  pl.BlockSpec((None, tn, K), lambda j, i: (l, jb + j, 0))],
        out_specs=pl.BlockSpec((tm, tn), lambda j, i: (i, j)),
        out_shape=jax.ShapeDtypeStruct((M, ncols), F32),
        compiler_params=_params("arbitrary", "arbitrary"),
        name=name,
    )(a, wt)


def _ws_mm_res_kernel(a_ref, w_ref, x_ref, g_ref, o_ref, wb_ref):
    @pl.when(_first_row_tile())
    def _():
        wb_ref[...] = w_ref[...].astype(BF16)

    half = a_ref.shape[0] // 2
    for rows in (slice(0, half), slice(half, 2 * half)):
        acc = jnp.dot(a_ref[rows, :], wb_ref[...], preferred_element_type=F32)
        o_ref[rows, :] = x_ref[rows, :] + g_ref[...] * acc


def _mm_res_kernel(a_ref, w_ref, x_ref, g_ref, o_ref):
    acc = jnp.dot(a_ref[...], w_ref[...], preferred_element_type=F32)
    o_ref[...] = x_ref[...] + g_ref[...] * acc


def _matmul_gated_residual(a, w, l, x, mod, gate_idx, tm, tn, name):
    M, K = a.shape
    N = w.shape[2]
    nj = N // tn
    cast = w.dtype != BF16
    return pl.pallas_call(
        _ws_mm_res_kernel if cast else _mm_res_kernel,
        grid=(nj, M // tm),
        in_specs=[pl.BlockSpec((tm, K), lambda j, i: (i, 0)),
                  pl.BlockSpec((None, K, tn), lambda j, i: (l, 0, j)),
                  pl.BlockSpec((tm, tn), lambda j, i: (i, j)),
                  pl.BlockSpec((1, tn), lambda j, i: (0, gate_idx * nj + j))],
        out_specs=pl.BlockSpec((tm, tn), lambda j, i: (i, j)),
        out_shape=jax.ShapeDtypeStruct((M, N), F32),
        scratch_shapes=[pltpu.VMEM((K, tn), BF16)] if cast else [],
        compiler_params=_params("arbitrary", "arbitrary"),
        name=name,
    )(a, w, x, mod)


def _swiglu_kernel(a_ref, wg_ref, wv_ref, o_ref, wgb_ref, wvb_ref):
    @pl.when(_first_row_tile())
    def _():
        wgb_ref[...] = wg_ref[...].astype(BF16)
        wvb_ref[...] = wv_ref[...].astype(BF16)

    half = a_ref.shape[0] // 2
    for rows in (slice(0, half), slice(half, 2 * half)):
        a = a_ref[rows, :]
        ug = jnp.dot(a, wgb_ref[...], preferred_element_type=F32)
        uv = jnp.dot(a, wvb_ref[...], preferred_element_type=F32)
        o_ref[rows, :] = (ug * _sigmoid(ug) * uv).astype(o_ref.dtype)


def _swiglu_in(a, w, l):
    M, K = a.shape
    H = w.shape[2] // 2
    tm, tn = 2048, 512
    nj = H // tn
    return pl.pallas_call(
        _swiglu_kernel,
        grid=(nj, M // tm),
        in_specs=[pl.BlockSpec((tm, K), lambda j, i: (i, 0)),
                  pl.BlockSpec((None, K, tn), lambda j, i: (l, 0, j)),
                  pl.BlockSpec((None, K, tn), lambda j, i: (l, 0, nj + j))],
        out_specs=pl.BlockSpec((tm, tn), lambda j, i: (i, j)),
        out_shape=jax.ShapeDtypeStruct((M, H), BF16),
        scratch_shapes=[pltpu.VMEM((K, tn), BF16), pltpu.VMEM((K, tn), BF16)],
        compiler_params=_params("arbitrary", "arbitrary"),
        name="ffn_in_swiglu",
    )(a, w, w)


def _merge_kernel(h_ref, g0_ref, g1_ref, g2_ref, g3_ref, y0_ref, y1_ref, y2_ref, y3_ref,
                  wb_ref, o_ref, wbb_ref):
    @pl.when(_first_row_tile())
    def _():
        wbb_ref[...] = wb_ref[...].astype(BF16)

    half = h_ref.shape[0] // 2
    for rows in (slice(0, half), slice(half, 2 * half)):
        h = h_ref[rows, :]
        acc = None
        for n, (gt_ref, y_ref) in enumerate(((g0_ref, y0_ref), (g1_ref, y1_ref),
                                             (g2_ref, y2_ref), (g3_ref, y3_ref))):
            logits = lax.dot_general(h, gt_ref[...], _CONTRACT_LAST,
                                     preferred_element_type=F32)
            branch = jnp.dot(y_ref[rows, :], wbb_ref[n], preferred_element_type=F32)
            term = _sigmoid(logits) * branch
            acc = term if acc is None else acc + term
        o_ref[rows, :] = acc.astype(o_ref.dtype)


def _gated_merge(h, wt, ys, w_branch, l):
    M, K = h.shape
    D = w_branch.shape[3]
    BW = w_branch.shape[2]
    tm, tn = 1024, 512
    nj = D // tn
    assert wt.dtype == BF16
    gate_specs = [pl.BlockSpec((None, tn, K),
                               functools.partial(lambda j, i, n: (l, n * nj + j, 0), n=n))
                  for n in range(N_BRANCH)]
    y_specs = [pl.BlockSpec((tm, BW), lambda j, i: (i, 0)) for _ in range(N_BRANCH)]
    return pl.pallas_call(
        _merge_kernel,
        grid=(nj, M // tm),
        in_specs=[pl.BlockSpec((tm, K), lambda j, i: (i, 0))] + gate_specs + y_specs
                 + [pl.BlockSpec((None, N_BRANCH, BW, tn), lambda j, i: (l, 0, 0, j))],
        out_specs=pl.BlockSpec((tm, tn), lambda j, i: (i, j)),
        out_shape=jax.ShapeDtypeStruct((M, D), BF16),
        scratch_shapes=[pltpu.VMEM((N_BRANCH, BW, tn), BF16)],
        compiler_params=_params("arbitrary", "arbitrary"),
        name="gated_merge",
    )(h, wt, wt, wt, wt, *ys, w_branch)


def _lru_kernel(ax_ref, ag_ref, cw_ref, cb_ref, wr_ref, br_ref, wi_ref, bi_ref, lam_ref,
                o_ref, xbuf, hcar):
    i = pl.program_id(0)
    tt = ax_ref.shape[0]

    @pl.when(i == 0)
    def _():
        xbuf[0:SUBLANE, :] = jnp.zeros((SUBLANE, LRU_WIDTH), F32)
        hcar[...] = jnp.zeros_like(hcar)

    xbuf[SUBLANE:SUBLANE + tt, :] = ax_ref[...]
    xa = cb_ref[...]
    for j in range(CONV_WIDTH):
        xa = xa + cw_ref[j:j + 1, :] * xbuf[pl.ds(SUBLANE - (CONV_WIDTH - 1) + j, tt), :]
    xbuf[0:SUBLANE, :] = ax_ref[tt - SUBLANE:tt, :]

    r = _sigmoid(_bdot(xa, wr_ref[...]) + br_ref[...])
    ig = _sigmoid(_bdot(xa, wi_ref[...]) + bi_ref[...])
    log_a = (-LRU_C * _softplus(-lam_ref[...])) * r
    a = jnp.exp(log_a)
    th = jnp.tanh(log_a)
    u = jnp.sqrt(-2.0 * th / (1.0 - th)) * (ig * xa)

    row = lax.broadcasted_iota(jnp.int32, (tt, LRU_WIDTH), 0)
    d = 1
    while d < tt:
        keep = row >= d
        u_s = jnp.where(keep, pltpu.roll(u, d, 0), 0.0)
        a_s = jnp.where(keep, pltpu.roll(a, d, 0), 1.0)
        u = u + a * u_s
        a = a * a_s
        d *= 2
    h = u + a * hcar[...]
    hcar[...] = h[tt - 1:tt, :]

    g = ag_ref[...]
    gelu = 0.5 * g * (1.0 + jnp.tanh(math.sqrt(2.0 / math.pi) * (g + 0.044715 * (g * g * g))))
    o_ref[...] = (h * gelu).astype(o_ref.dtype)


def _lru_branch(p_a, conv_w, conv_b, wr_bd, br, wi_bd, bi, lam):
    T = p_a.shape[0]
    W = LRU_WIDTH
    tt = 256
    row = lambda i: (0, 0)
    return pl.pallas_call(
        _lru_kernel,
        grid=(T // tt,),
        in_specs=[pl.BlockSpec((tt, W), lambda i: (i, 0)),
                  pl.BlockSpec((tt, W), lambda i: (i, 1)),
                  pl.BlockSpec((CONV_WIDTH, W), row),
                  pl.BlockSpec((1, W), row),
                  pl.BlockSpec((W, W), row),
                  pl.BlockSpec((1, W), row),
                  pl.BlockSpec((W, W), row),
                  pl.BlockSpec((1, W), row),
                  pl.BlockSpec((1, W), row)],
        out_specs=pl.BlockSpec((tt, W), lambda i: (i, 0)),
        out_shape=jax.ShapeDtypeStruct((T, W), BF16),
        scratch_shapes=[pltpu.VMEM((tt + SUBLANE, W), F32), pltpu.VMEM((1, W), F32)],
        compiler_params=_params("arbitrary"),
        name="rg_lru",
    )(p_a, p_a, conv_w, conv_b, wr_bd, br, wi_bd, bi, lam)


def _mla_prep_kernel(p_ref, gcq_ref, gckv_ref, wuq_ref, wuk_ref, wuv_ref, gq_ref, gk_ref,
                     cc_ref, sa_ref, sb_ref, q_ref, k_ref, v_ref):
    p = p_ref[...]
    cq = p[:, :MLA_Q_LORA]
    ckv = p[:, MLA_Q_LORA:MLA_Q_LORA + MLA_KV_LORA]
    kr = p[:, MLA_Q_LORA + MLA_KV_LORA:]
    lane = lax.broadcasted_iota(jnp.int32, kr.shape, 1)
    kr = jnp.where(lane < MLA_ROPE_DIM, kr, 0.0)

    def rms(x, g, n):
        return x * lax.rsqrt(jnp.sum(x * x, axis=-1, keepdims=True) * (1.0 / n) + NORM_EPS) * g

    cqn = rms(cq, gcq_ref[...], MLA_Q_LORA)
    ckvn = rms(ckv, gckv_ref[...], MLA_KV_LORA)
    q = _bdot(cqn, wuq_ref[...])
    kn = _bdot(ckvn, wuk_ref[...])
    v_ref[...] = _bdot(ckvn, wuv_ref[...]).astype(v_ref.dtype)
    cc, sa, sb = cc_ref[...], sa_ref[...], sb_ref[...]
    half = MLA_ROPE_DIM // 2

    def rope(x):
        x2 = x[:, LANE:]
        x2 = x2 * cc + pltpu.roll(x2, LANE - half, 1) * sa + pltpu.roll(x2, half, 1) * sb
        return jnp.concatenate([x[:, :LANE], x2], axis=-1)

    scale = MLA_QK_DIM ** -0.5 * math.log2(math.e)
    for h in range(MLA_HEADS):
        qh = rms(q[:, h * MLA_QK_PAD:(h + 1) * MLA_QK_PAD], gq_ref[...], MLA_QK_DIM)
        q_ref[:, h * MLA_QK_PAD:(h + 1) * MLA_QK_PAD] = (rope(qh) * scale).astype(q_ref.dtype)
        kh = jnp.concatenate([kn[:, h * MLA_NOPE_DIM:(h + 1) * MLA_NOPE_DIM], kr], axis=-1)
        kh = rms(kh, gk_ref[...], MLA_QK_DIM)
        k_ref[:, h * MLA_QK_PAD:(h + 1) * MLA_QK_PAD] = rope(kh).astype(k_ref.dtype)


def _mla_prep(p_mla, g_cq, g_ckv, wuq, wuk, wuv, gq, gk, cc, sa, sb):
    T, PW = p_mla.shape
    tt = 512
    QW = MLA_HEADS * MLA_QK_PAD
    VW = MLA_HEADS * MLA_V_DIM
    row = lambda i: (0, 0)
    tab = pl.BlockSpec((tt, LANE), lambda i: (i, 0))
    return pl.pallas_call(
        _mla_prep_kernel,
        grid=(T // tt,),
        in_specs=[pl.BlockSpec((tt, PW), lambda i: (i, 0)),
                  pl.BlockSpec((1, MLA_Q_LORA), row),
                  pl.BlockSpec((1, MLA_KV_LORA), row),
                  pl.BlockSpec((MLA_Q_LORA, QW), row),
                  pl.BlockSpec((MLA_KV_LORA, MLA_HEADS * MLA_NOPE_DIM), row),
                  pl.BlockSpec((MLA_KV_LORA, VW), row),
                  pl.BlockSpec((1, MLA_QK_PAD), row),
                  pl.BlockSpec((1, MLA_QK_PAD), row),
                  tab, tab, tab],
        out_specs=[pl.BlockSpec((tt, QW), lambda i: (i, 0)),
                   pl.BlockSpec((tt, QW), lambda i: (i, 0)),
                   pl.BlockSpec((tt, VW), lambda i: (i, 0))],
        out_shape=[jax.ShapeDtypeStruct((T, QW), BF16),
                   jax.ShapeDtypeStruct((T, QW), BF16),
                   jax.ShapeDtypeStruct((T, VW), BF16)],
        compiler_params=_params("arbitrary"),
        name="mla_prep",
    )(p_mla, g_cq, g_ckv, wuq, wuk, wuv, gq, gk, cc, sa, sb)


ATTN_BLOCK = 512


ATTN_HEADS_PER_STEP = 4
ATTN_ROW_GROUP = 32


def _attn_kernel(q_ref, k_ref, v_ref, o_ref, s_scr, p_scr, m_scr, l_scr, a_scr, acc_scr, *, blk):
    i = pl.program_id(1)
    NH, RG = ATTN_HEADS_PER_STEP, ATTN_ROW_GROUP
    ntile = blk // LANE
    qk = [slice(t * MLA_QK_PAD, (t + 1) * MLA_QK_PAD) for t in range(NH)]
    vd = [slice(t * MLA_V_DIM, (t + 1) * MLA_V_DIM) for t in range(NH)]
    m_scr[...] = jnp.full(m_scr.shape, -jnp.inf, F32)
    l_scr[...] = jnp.zeros(l_scr.shape, F32)
    acc_scr[...] = jnp.zeros(acc_scr.shape, F32)

    def step(j, masked):
        start = pl.multiple_of(j * blk, blk)
        for t in range(NH):
            s_scr[t] = lax.dot_general(q_ref[:, qk[t]], k_ref[pl.ds(start, blk), qk[t]],
                                       (((1,), (1,)), ((), ())), preferred_element_type=F32)
        for t in range(NH):
            for g in range(blk // RG):
                r = slice(g * RG, (g + 1) * RG)
                s = s_scr[t, r, :]
                if masked:
                    row = g * RG + lax.broadcasted_iota(jnp.int32, (RG, blk), 0)
                    col = lax.broadcasted_iota(jnp.int32, (RG, blk), 1)
                    s = jnp.where(col <= row, s, -jnp.inf)
                    s_scr[t, r, :] = s
                m_old = m_scr[t, r, :]
                m_new = jnp.maximum(m_old, jnp.max(s, axis=-1, keepdims=True))
                a_scr[t, r, :] = jnp.exp2(m_old - m_new)
                m_scr[t, r, :] = m_new
            for g in range(blk // RG):
                r = slice(g * RG, (g + 1) * RG)
                m_new = m_scr[t, r, :]
                p = jnp.exp2(s_scr[t, r, :] - jnp.concatenate([m_new] * ntile, axis=1))
                psum = p[:, :LANE]
                for c in range(1, ntile):
                    psum = psum + p[:, c * LANE:(c + 1) * LANE]
                l_scr[t, r, :] = a_scr[t, r, :] * l_scr[t, r, :] + psum
                p_scr[t, r, :] = p.astype(BF16)
        for t in range(NH):
            acc_scr[t] = acc_scr[t] * a_scr[t] + jnp.dot(
                p_scr[t], v_ref[pl.ds(start, blk), vd[t]], preferred_element_type=F32)

    def body(j, carry):
        step(j, False)
        return carry

    lax.fori_loop(0, i, body, 0)
    step(i, True)
    for t in range(NH):
        l = jnp.sum(l_scr[t], axis=-1, keepdims=True)
        o_ref[:, vd[t]] = (acc_scr[t] / l).astype(o_ref.dtype)


def _causal_attention(q, k, v):
    T = q.shape[0]
    blk, NH = ATTN_BLOCK, ATTN_HEADS_PER_STEP
    assert MLA_V_DIM == LANE and MLA_HEADS % NH == 0
    stat = pltpu.VMEM((NH, blk, LANE), F32)
    return pl.pallas_call(
        functools.partial(_attn_kernel, blk=blk),
        grid=(MLA_HEADS // NH, T // blk),
        in_specs=[pl.BlockSpec((blk, NH * MLA_QK_PAD), lambda h, i: (i, h)),
                  pl.BlockSpec((T, NH * MLA_QK_PAD), lambda h, i: (0, h),
                               pipeline_mode=pl.Buffered(1)),
                  pl.BlockSpec((T, NH * MLA_V_DIM), lambda h, i: (0, h),
                               pipeline_mode=pl.Buffered(1))],
        out_specs=pl.BlockSpec((blk, NH * MLA_V_DIM), lambda h, i: (i, h)),
        out_shape=jax.ShapeDtypeStruct((T, MLA_HEADS * MLA_V_DIM), BF16),
        scratch_shapes=[pltpu.VMEM((NH, blk, blk), F32), pltpu.VMEM((NH, blk, blk), BF16),
                        stat, stat, stat, stat],
        compiler_params=_params("arbitrary", "arbitrary"),
        name="mla_attention",
    )(q, k, v)


_RET_LOG_GAMMA = np.log1p(-np.exp(np.linspace(math.log(1.0 / 32), math.log(1.0 / 512),
                                              RET_HEADS, dtype=np.float32))).astype(np.float32)


RET_GROUP = 4


def _ret_kernel(q_ref, k_ref, v_ref, g_ref, cc_ref, ss_ref, gn_ref, o_ref, st_ref):
    i = pl.program_id(0)
    C, Dh, H = RET_CHUNK, RET_HEAD_DIM, RET_HEADS

    @pl.when(i == 0)
    def _():
        st_ref[...] = jnp.zeros_like(st_ref)

    row = lax.broadcasted_iota(jnp.int32, (C, C), 0)
    col = lax.broadcasted_iota(jnp.int32, (C, C), 1)
    rel = (row - col).astype(F32)
    idx = lax.broadcasted_iota(jnp.int32, (C, 1), 0).astype(F32)
    lgs = [float(x) for x in _RET_LOG_GAMMA]
    decay_in = jnp.stack([jnp.where(rel >= 0, jnp.exp(lg * jnp.maximum(rel, 0.0)), 0.0)
                          for lg in lgs])
    q_dec = jnp.stack([jnp.exp(lg * (idx + 1.0)) for lg in lgs])
    k_dec = jnp.stack([jnp.exp(lg * (C - 1.0 - idx)) for lg in lgs])
    heads = [slice(h * Dh, (h + 1) * Dh) for h in range(H)]
    st = st_ref[...]
    for c in range(RET_GROUP):
        rows = slice(c * C, (c + 1) * C)
        cc, ss = cc_ref[rows, :], ss_ref[rows, :]

        def rope(ref):
            x = jnp.stack([ref[rows, sl] for sl in heads])
            return x * cc + pltpu.roll(x, Dh // 2, 2) * ss

        q = rope(q_ref)
        k = rope(k_ref) * (Dh ** -0.5)
        v = jnp.stack([v_ref[rows, sl] for sl in heads])
        inner = _bmm(q, k, _BNT) * decay_in
        o = _bmm(inner, v, _BNN) + _bmm(q * q_dec, st, _BNN)
        kv = _bmm(k * k_dec, v, _BTN)
        st = jnp.stack([math.exp(lg * C) * st[h] + kv[h] for h, lg in enumerate(lgs)])
        oc = o - jnp.mean(o, axis=-1, keepdims=True)
        on = oc * lax.rsqrt(jnp.mean(oc * oc, axis=-1, keepdims=True) + GN_EPS)
        for h, sl in enumerate(heads):
            g = g_ref[rows, sl]
            o_ref[rows, sl] = (g * _sigmoid(g) * (on[h] * gn_ref[:, sl])).astype(o_ref.dtype)
    st_ref[...] = st


def _retention_branch(p_ret, cc, ss, g_norm):
    T = p_ret.shape[0]
    C, W = RET_CHUNK * RET_GROUP, RET_WIDTH
    blk = lambda n: pl.BlockSpec((C, W), functools.partial(lambda i, n: (i, n), n=n))
    return pl.pallas_call(
        _ret_kernel,
        grid=(T // C,),
        in_specs=[blk(0), blk(1), blk(2), blk(3),
                  pl.BlockSpec((C, LANE), lambda i: (i, 0)),
                  pl.BlockSpec((C, LANE), lambda i: (i, 0)),
                  pl.BlockSpec((1, W), lambda i: (0, 0))],
        out_specs=pl.BlockSpec((C, W), lambda i: (i, 0)),
        out_shape=jax.ShapeDtypeStruct((T, W), BF16),
        scratch_shapes=[pltpu.VMEM((RET_HEADS, RET_HEAD_DIM, RET_HEAD_DIM), F32)],
        compiler_params=_params("arbitrary"),
        name="retention",
    )(p_ret, p_ret, p_ret, p_ret, cc, ss, g_norm)


RWKV_PAD_COLS = 3 * RWKV_WIDTH + 3 * LANE


def _split_dot(a, b_exact, terms):
    acc = None
    rem = a
    for _ in range(terms):
        piece = rem.astype(BF16)
        rem = rem - piece.astype(F32)
        d = jnp.dot(piece, b_exact, preferred_element_type=F32)
        acc = d if acc is None else acc + d
    return acc


def _rwkv_prep_kernel(p_ref, mu_ref, w0_ref, a0_ref, wup_ref, aup_ref, gup_ref, kk_ref, ka_ref,
                      rk_ref, ones_ref, tri_ref,
                      kp_ref, rp_ref, kn_ref, bn_ref, knp_ref, bnp_ref, v_ref, g_ref, bonus_ref,
                      pc_ref, buf):
    i = pl.program_id(0)
    tt = p_ref.shape[0]
    W = RWKV_WIDTH
    C = RWKV_CHUNK

    @pl.when(i == 0)
    def _():
        buf[0:SUBLANE, :] = jnp.zeros((SUBLANE, RWKV_PAD_COLS), F32)

    p = p_ref[...]
    buf[SUBLANE:SUBLANE + tt, :] = p
    prev = buf[pl.ds(SUBLANE - 1, tt), :]
    buf[0:SUBLANE, :] = p_ref[tt - SUBLANE:tt, :]
    xs = p + (prev - p) * mu_ref[...]
    r = xs[:, 0:W]
    k = xs[:, W:2 * W]
    v = xs[:, 2 * W:3 * W]
    wd = xs[:, 3 * W:3 * W + LANE]
    ad = xs[:, 3 * W + LANE:3 * W + 2 * LANE]
    gd = xs[:, 3 * W + 2 * LANE:3 * W + 3 * LANE]

    w_log = -_softplus(-(w0_ref[...] + _fdot(jnp.tanh(wd), wup_ref[...]))) - 0.5
    logw = -jnp.exp(w_log)
    a = _sigmoid(a0_ref[...] + _bdot(ad, aup_ref[...]))
    g_ref[...] = _bdot(_sigmoid(gd), gup_ref[...])

    ones_bd = ones_ref[...]
    kk = k * kk_ref[...]
    kk = kk / jnp.maximum(jnp.sqrt(_split_dot(kk * kk, ones_bd, 2)), 1e-12)
    kt = k * (1.0 + (a - 1.0) * ka_ref[...])
    bonus_ref[...] = _split_dot(r * kt * rk_ref[...], ones_bd, 2) * v
    v_ref[...] = v.astype(v_ref.dtype)

    tri = tri_ref[...]
    cum = jnp.concatenate([_split_dot_lhs_exact(tri, logw[c * C:(c + 1) * C], 3)
                           for c in range(tt // C)], axis=0)
    tot = jnp.sum(logw.reshape(tt // C, C, W), axis=1)
    pc_ref[...] = jnp.exp(tot)
    to_end = jnp.exp((tot[:, None, :] - cum.reshape(tt // C, C, W)).reshape(tt, W))
    e_neg = jnp.exp(-cum)
    beta = kk * a
    kp_ref[...] = (kk * jnp.exp(cum - logw)).astype(kp_ref.dtype)
    rp_ref[...] = (r * jnp.exp(cum)).astype(rp_ref.dtype)
    kn_ref[...] = (kt * e_neg).astype(kn_ref.dtype)
    bn_ref[...] = (beta * e_neg).astype(bn_ref.dtype)
    knp_ref[...] = (kt * to_end).astype(knp_ref.dtype)
    bnp_ref[...] = (beta * to_end).astype(bnp_ref.dtype)


def _split_dot_lhs_exact(a_exact, b, terms):
    acc = None
    rem = b
    for _ in range(terms):
        piece = rem.astype(BF16)
        rem = rem - piece.astype(F32)
        d = jnp.dot(a_exact, piece, preferred_element_type=F32)
        acc = d if acc is None else acc + d
    return acc


def _rwkv_prep(p_rw, mu, w0, a0, wup, aup, gup, k_k, k_a, r_k):
    T = p_rw.shape[0]
    W, C = RWKV_WIDTH, RWKV_CHUNK
    tt = 512
    hid = np.arange(W) // RWKV_HEAD_DIM
    ones_bd = jnp.asarray((hid[:, None] == hid[None, :]).astype(np.float32), dtype=BF16)
    tid = np.arange(C)
    tri = jnp.asarray((tid[:, None] >= tid[None, :]).astype(np.float32), dtype=BF16)
    row = lambda i: (0, 0)
    vec = pl.BlockSpec((1, W), row)
    lora = pl.BlockSpec((LANE, W), row)
    out = pl.BlockSpec((tt, W), lambda i: (i, 0))
    lo = jax.ShapeDtypeStruct((T, W), BF16)
    hi = jax.ShapeDtypeStruct((T, W), F32)
    return pl.pallas_call(
        _rwkv_prep_kernel,
        grid=(T // tt,),
        in_specs=[pl.BlockSpec((tt, RWKV_PAD_COLS), lambda i: (i, 0)),
                  pl.BlockSpec((1, RWKV_PAD_COLS), row),
                  vec, vec, lora, lora, lora, vec, vec, vec,
                  pl.BlockSpec((W, W), row),
                  pl.BlockSpec((C, C), row)],
        out_specs=[out] * 9 + [pl.BlockSpec((tt // C, W), lambda i: (i, 0))],
        out_shape=[lo] * 7 + [hi, hi, jax.ShapeDtypeStruct((T // C, W), F32)],
        scratch_shapes=[pltpu.VMEM((tt + SUBLANE, RWKV_PAD_COLS), F32)],
        compiler_params=_params("arbitrary"),
        name="rwkv_prep",
    )(p_rw, mu, w0, a0, wup, aup, gup, k_k, k_a, r_k, ones_bd, tri)


def _bmm(a, b, dims):
    return lax.dot_general(a.astype(BF16), b.astype(BF16), dims, preferred_element_type=F32)


_BNN = (((2,), (1,)), ((0,), (0,)))
_BNT = (((2,), (2,)), ((0,), (0,)))
_BTN = (((1,), (1,)), ((0,), (0,)))
RWKV_GROUP = 4


def _rwkv_rec_kernel(kp_ref, rp_ref, kn_ref, bn_ref, knp_ref, bnp_ref, v_ref, pc_ref, gn_ref,
                     bonus_ref, g_ref, y_ref, s_ref, *, nchunk):
    C = RWKV_CHUNK
    P = 2 * C
    assert P == LANE

    @pl.when(pl.program_id(0) == 0)
    def _():
        s_ref[...] = jnp.zeros_like(s_ref)

    row = lax.broadcasted_iota(jnp.int32, (P, P), 0)
    col = lax.broadcasted_iota(jnp.int32, (P, P), 1)

    def blocks(n):
        return jnp.where((row // n) == (col // n), 1.0, 0.0)

    own = blocks(C)
    own_bf = own.astype(BF16)
    lower = jnp.where(row > col, 1.0, 0.0)
    m_strict = own * lower
    m_incl = own * jnp.where(row >= col, 1.0, 0.0)
    m_base = blocks(16) * lower
    m_l32 = blocks(32) * lower - m_base
    m_l64 = m_strict - m_base - m_l32
    eye = jnp.where(row == col, 1.0, 0.0)

    npair = RWKV_HEADS // 2
    lanes = [slice(p * LANE, (p + 1) * LANE) for p in range(npair)]
    G = RWKV_GROUP

    def group(gi, carry):
        rows = [pl.ds(pl.multiple_of((gi * G + c) * C, C), C) for c in range(G)]

        def load(ref):
            x = jnp.stack([ref[sl, ln] for sl in rows for ln in lanes])
            return jnp.concatenate([x, x], axis=1) * own_bf

        kp, rp, kn, bn, knp, bnp, v = (load(r) for r in (kp_ref, rp_ref, kn_ref, bn_ref,
                                                         knp_ref, bnp_ref, v_ref))
        q2 = jnp.concatenate([kp, rp], axis=1)
        a_kn = _bmm(q2, kn, _BNT)
        a_bn = _bmm(q2, bn, _BNT)
        a_k = a_kn[:, :P] * m_strict
        a_rk = a_kn[:, P:] * m_incl
        a_b = a_bn[:, :P]
        a_rb = a_bn[:, P:] * m_incl

        n1 = -(a_b * m_base)
        inv = eye + n1
        n2 = _bmm(n1, n1, _BNN)
        inv = inv + _bmm(inv, n2, _BNN)
        n4 = _bmm(n2, n2, _BNN)
        inv = inv + _bmm(inv, n4, _BNN)
        n8 = _bmm(n4, n4, _BNN)
        inv = inv + _bmm(inv, n8, _BNN)
        for msk in (m_l32, m_l64):
            inv = inv - _bmm(inv, _bmm(a_b * msk, inv, _BNN), _BNN)
        av = _bmm(jnp.concatenate([a_k, a_rk], axis=1), v, _BNN)
        k2 = jnp.concatenate([knp, bnp], axis=1)

        s = s_ref[...]
        inv_n = 1.0 / RWKV_HEAD_DIM
        for c in range(G):
            b = slice(c * npair, (c + 1) * npair)
            x1 = _bmm(q2[b], s, _BNT)
            u = _bmm(inv[b], x1[:, :P] + av[b, :P], _BNN)
            o = x1[:, P:] + av[b, P:] - _bmm(a_rb[b], u, _BNN)
            vu = jnp.concatenate([v[b], (-u).astype(BF16)], axis=1)
            pc = pc_ref[gi * G + c]
            pc = jnp.stack([pc[:, ln] for ln in lanes])
            s = s * pc + _bmm(vu, k2[b], _BTN)

            oc = (o - jnp.sum(o, axis=-1, keepdims=True) * inv_n) * own
            on = oc * lax.rsqrt(jnp.sum(oc * oc, axis=-1, keepdims=True) * inv_n + GN_EPS)
            on = on[:, :C] + on[:, C:]
            for p, ln in enumerate(lanes):
                y = (on[p] * gn_ref[:, ln] + bonus_ref[rows[c], ln]) * g_ref[rows[c], ln]
                y_ref[rows[c], ln] = y.astype(y_ref.dtype)
        s_ref[...] = s
        return carry

    lax.fori_loop(0, nchunk // G, group, 0)


def _rwkv_recurrence(kp, rp, kn, bn, knp, bnp, v, pc, gn, bonus, g):
    T, W = kp.shape
    C = RWKV_CHUNK
    tb = 512
    nchunk = tb // C
    blk = pl.BlockSpec((tb, W), lambda i: (i, 0))
    return pl.pallas_call(
        functools.partial(_rwkv_rec_kernel, nchunk=nchunk),
        grid=(T // tb,),
        in_specs=[blk] * 7 + [pl.BlockSpec((nchunk, 1, W), lambda i: (i, 0, 0)),
                              pl.BlockSpec((1, W), lambda i: (0, 0)), blk, blk],
        out_specs=blk,
        out_shape=jax.ShapeDtypeStruct((T, W), BF16),
        scratch_shapes=[pltpu.VMEM((RWKV_HEADS // 2, LANE, LANE), F32)],
        compiler_params=_params("arbitrary"),
        name="rwkv_recurrence",
    )(kp, rp, kn, bn, knp, bnp, v, pc, gn, bonus, g)


def _rwkv_branch(p_rw, mu, w0, a0, wup, aup, gup, k_k, k_a, r_k, g_norm):
    kp, rp, kn, bn, knp, bnp, v, g, bonus, pc = _rwkv_prep(p_rw, mu, w0, a0, wup, aup, gup,
                                                          k_k, k_a, r_k)
    pc = pc.reshape(pc.shape[0], 1, pc.shape[1])
    return _rwkv_recurrence(kp, rp, kn, bn, knp, bnp, v, pc, g_norm.reshape(1, -1), bonus, g)


def _pad_cols(w, n):
    return jnp.pad(w, ((0, 0), (0, n - w.shape[1])))


def _block_diag(w):
    n, bi, bj = w.shape
    eye = jnp.eye(n, dtype=w.dtype)
    return (eye[:, None, :, None] * w[:, :, None, :]).reshape(n * bi, n * bj)


def _rope_tables(positions):
    pos = positions.astype(F32).reshape(-1, 1)
    T = pos.shape[0]

    def cs(dim):
        inv = 1.0 / (ROPE_BASE ** (jnp.arange(0, dim, 2, dtype=F32) / dim))
        ang = pos * inv
        return jnp.cos(ang), jnp.sin(ang)

    cm, sm = cs(MLA_ROPE_DIM)
    z32 = jnp.zeros((T, MLA_ROPE_DIM // 2), F32)
    z64 = jnp.zeros((T, LANE - MLA_ROPE_DIM), F32)
    mla = (jnp.concatenate([cm, cm, z64], axis=1),
           jnp.concatenate([-sm, z32, z64], axis=1),
           jnp.concatenate([z32, sm, z64], axis=1))
    cr, sr = cs(RET_HEAD_DIM)
    ret = (jnp.concatenate([cr, cr], axis=1), jnp.concatenate([-sr, sr], axis=1))
    return mla, ret


def _mla_weights(w_uq, w_ukv, g_qn, g_kn):
    wq = w_uq.reshape(MLA_Q_LORA, MLA_HEADS, MLA_QK_DIM)
    wq = jnp.pad(wq, ((0, 0), (0, 0), (0, MLA_QK_PAD - MLA_QK_DIM)))
    wq = wq.reshape(MLA_Q_LORA, MLA_HEADS * MLA_QK_PAD)
    wkv = w_ukv.reshape(MLA_KV_LORA, MLA_HEADS, MLA_NOPE_DIM + MLA_V_DIM)
    wk = wkv[:, :, :MLA_NOPE_DIM].reshape(MLA_KV_LORA, MLA_HEADS * MLA_NOPE_DIM)
    wv = wkv[:, :, MLA_NOPE_DIM:].reshape(MLA_KV_LORA, MLA_HEADS * MLA_V_DIM)
    pad = MLA_QK_PAD - MLA_QK_DIM
    gq = jnp.pad(g_qn, (0, pad)).reshape(1, MLA_QK_PAD)
    gk = jnp.pad(g_kn, (0, pad)).reshape(1, MLA_QK_PAD)
    return wq.astype(BF16), wk.astype(BF16), wv.astype(BF16), gq, gk


def _pad_rows(w, n):
    return jnp.pad(w, ((0, n - w.shape[0]), (0, 0)))


def kernel(x, c, positions, ada_w, ada_b, norm_mix, norm_ffn, w_in, conv_w, conv_b, lru_wr, lru_br, lru_wi, lru_bi, lru_lam, mla_g_cq, mla_g_ckv, mla_w_uq, mla_w_ukv, mla_g_qn, mla_g_kn, ret_g_norm, rwkv_mu, rwkv_w0, rwkv_w_up, rwkv_a0, rwkv_a_up, rwkv_g_up, rwkv_k_k, rwkv_k_a, rwkv_r_k, rwkv_g_norm, w_branch, w_out, ffn_w_in, ffn_w_out):
    B, T, D = x.shape
    assert B == 1 and D == D_MODEL
    depth = ada_w.shape[0]
    xt = x.reshape(T, D)
    mod_all = _modulation(c, ada_w, ada_b)
    (cc_m, sa_m, sb_m), (cc_r, ss_r) = _rope_tables(positions)
    w_in_t = jnp.swapaxes(w_in, 1, 2).astype(BF16)
    ffn_w_out_bf = ffn_w_out.astype(BF16)

    o_a = GATE_COLS
    o_cq = o_a + 2 * LRU_WIDTH
    o_ckv = o_cq + MLA_Q_LORA
    o_kr = o_ckv + MLA_KV_LORA
    o_ret = o_kr + MLA_ROPE_DIM
    o_rw = o_ret + 4 * RET_WIDTH
    o_lora = o_rw + 3 * RWKV_WIDTH
    W = RWKV_WIDTH

    for l in range(depth):
        mod = mod_all[l]
        dl, al = RWKV_DECAY_LORA, RWKV_AAA_LORA
        w_ret_t = w_in_t[l, o_ret:o_rw]
        w_rw_t = jnp.concatenate([w_in_t[l, o_rw:o_lora],
                                  _pad_rows(w_in_t[l, o_lora:o_lora + dl], LANE),
                                  _pad_rows(w_in_t[l, o_lora + dl:o_lora + dl + al], LANE),
                                  w_in_t[l, o_lora + dl + al:]], axis=0)
        mu = rwkv_mu[l]
        mu_p = jnp.concatenate([mu[:3 * W], jnp.pad(mu[3 * W:3 * W + dl], (0, LANE - dl)),
                                jnp.pad(mu[3 * W + dl:3 * W + dl + al], (0, LANE - al)),
                                mu[3 * W + dl + al:]]).reshape(1, RWKV_PAD_COLS)

        h = _mod_norm(xt, norm_mix[l].reshape(1, D), mod, 0)
        p_a = _ws_matmul_nt(h, w_in_t, l, o_a, 2 * LRU_WIDTH, 512, "in_proj_lru")
        p_mla = _ws_matmul_nt(h, w_in_t, l, o_cq, o_ret - o_cq + MLA_ROPE_DIM, 768, "in_proj_mla")
        p_ret = _matmul_nt(h, w_ret_t, "in_proj_ret")
        p_rw = _matmul_nt(h, w_rw_t, "in_proj_rwkv")

        y_a = _lru_branch(p_a, conv_w[l], conv_b[l].reshape(1, -1),
                          _block_diag(lru_wr[l]).astype(BF16), lru_br[l].reshape(1, -1),
                          _block_diag(lru_wi[l]).astype(BF16), lru_bi[l].reshape(1, -1),
                          lru_lam[l].reshape(1, -1))

        wq, wk, wv, gq, gk = _mla_weights(mla_w_uq[l], mla_w_ukv[l], mla_g_qn[l], mla_g_kn[l])
        q, k, v = _mla_prep(p_mla, mla_g_cq[l].reshape(1, -1), mla_g_ckv[l].reshape(1, -1),
                            wq, wk, wv, gq, gk, cc_m, sa_m, sb_m)
        y_b = _causal_attention(q, k, v)

        y_c = _retention_branch(p_ret, cc_r, ss_r, ret_g_norm[l].reshape(1, -1))

        y_d = _rwkv_branch(p_rw, mu_p, rwkv_w0[l].reshape(1, -1), rwkv_a0[l].reshape(1, -1),
                           _pad_rows(rwkv_w_up[l], LANE), _pad_rows(rwkv_a_up[l], LANE),
                           rwkv_g_up[l], rwkv_k_k[l].reshape(1, -1), rwkv_k_a[l].reshape(1, -1),
                           rwkv_r_k[l].reshape(1, -1), rwkv_g_norm[l])

        merged = _gated_merge(h, w_in_t, (y_a, y_b, y_c, y_d), w_branch, l)
        xt = _matmul_gated_residual(merged, w_out, l, xt, mod, 2, 1024, 1024, "out_proj")

        h2 = _mod_norm(xt, norm_ffn[l].reshape(1, D), mod, 3)
        act = _swiglu_in(h2, ffn_w_in, l)
        xt = _matmul_gated_residual(act, ffn_w_out_bf, l, xt, mod, 5, 512, 1024, "ffn_out")
    return xt.reshape(B, T, D)
```

```python
import functools
import math

import numpy as np
import jax
import jax.numpy as jnp
from jax import lax
from jax.experimental import pallas as pl
from jax.experimental.pallas import tpu as pltpu

F32 = jnp.float32
BF16 = jnp.bfloat16
HIGHEST = lax.Precision.HIGHEST

D_MODEL = 2048
N_BRANCH = 4
BRANCH_WIDTH = D_MODEL // N_BRANCH
NORM_EPS = 1e-6
GN_EPS = 1e-5
ROPE_BASE = 10000.0
LRU_WIDTH = BRANCH_WIDTH
LRU_C = 8.0
CONV_WIDTH = 4
MLA_HEADS = 4
MLA_NOPE_DIM = 128
MLA_ROPE_DIM = 64
MLA_V_DIM = 128
MLA_QK_DIM = MLA_NOPE_DIM + MLA_ROPE_DIM
MLA_QK_PAD = 256
MLA_Q_LORA = 384
MLA_KV_LORA = 256
RET_HEADS = 4
RET_HEAD_DIM = 128
RET_WIDTH = RET_HEADS * RET_HEAD_DIM
RET_CHUNK = 128
RWKV_HEAD_DIM = 64
RWKV_HEADS = 8
RWKV_WIDTH = RWKV_HEADS * RWKV_HEAD_DIM
RWKV_DECAY_LORA = 64
RWKV_AAA_LORA = 64
RWKV_GATE_LORA = 128
RWKV_CHUNK = 64
FFN_HIDDEN = 5632
GATE_COLS = N_BRANCH * D_MODEL
LANE = 128
SUBLANE = 8
VMEM_LIMIT_BYTES = 56 * 1024 * 1024


def _params(*sem):
    return pltpu.CompilerParams(dimension_semantics=sem, vmem_limit_bytes=VMEM_LIMIT_BYTES)


def _sigmoid(x):
    return 1.0 / (1.0 + jnp.exp(-x))


def _softplus(x):
    return jnp.maximum(x, 0.0) + jnp.log(1.0 + jnp.exp(-jnp.abs(x)))


def _bdot(a, b):
    return jnp.dot(a.astype(BF16), b.astype(BF16), preferred_element_type=F32)


def _fdot(a, b):
    return jnp.dot(a, b, preferred_element_type=F32, precision=HIGHEST)


def _mod_kernel(c_ref, w_ref, b_ref, o_ref):
    c = c_ref[...]
    ca = c * _sigmoid(c)
    o_ref[0] = jnp.sum(ca * w_ref[0], axis=0, keepdims=True) + b_ref[0]


def _modulation(c, ada_w, ada_b):
    L, D, N = ada_w.shape
    tn = 1024
    return pl.pallas_call(
        _mod_kernel,
        grid=(L, N // tn),
        in_specs=[pl.BlockSpec((D, 1), lambda l, j: (0, 0)),
                  pl.BlockSpec((1, D, tn), lambda l, j: (l, 0, j)),
                  pl.BlockSpec((1, 1, tn), lambda l, j: (l, 0, j))],
        out_specs=pl.BlockSpec((1, 1, tn), lambda l, j: (l, 0, j)),
        out_shape=jax.ShapeDtypeStruct((L, 1, N), F32),
        compiler_params=_params("arbitrary", "arbitrary"),
        name="adaln_mod",
    )(c.reshape(D, 1), ada_w, ada_b.reshape(L, 1, N))


def _norm_kernel(x_ref, g_ref, sh_ref, sc_ref, o_ref):
    x = x_ref[...]
    ms = jnp.mean(x * x, axis=-1, keepdims=True)
    y = x * lax.rsqrt(ms + NORM_EPS) * g_ref[...]
    o_ref[...] = (y * (1.0 + sc_ref[...]) + sh_ref[...]).astype(o_ref.dtype)


def _mod_norm(x, g, mod, shift_idx):
    T, D = x.shape
    tm = 512
    return pl.pallas_call(
        _norm_kernel,
        grid=(T // tm,),
        in_specs=[pl.BlockSpec((tm, D), lambda i: (i, 0)),
                  pl.BlockSpec((1, D), lambda i: (0, 0)),
                  pl.BlockSpec((1, D), lambda i: (0, shift_idx)),
                  pl.BlockSpec((1, D), lambda i: (0, shift_idx + 1))],
        out_specs=pl.BlockSpec((tm, D), lambda i: (i, 0)),
        out_shape=jax.ShapeDtypeStruct((T, D), BF16),
        compiler_params=_params("arbitrary"),
        name="mod_norm",
    )(x, g, mod, mod)


_CONTRACT_LAST = (((1,), (1,)), ((), ()))


def _mm_nt_kernel(a_ref, wt_ref, o_ref):
    o_ref[...] = lax.dot_general(a_ref[...], wt_ref[...], _CONTRACT_LAST,
                                 preferred_element_type=F32).astype(o_ref.dtype)


def _matmul_nt(a, wt, name):
    M, K = a.shape
    N = wt.shape[0]
    tm = 1024
    return pl.pallas_call(
        _mm_nt_kernel,
        grid=(M // tm,),
        in_specs=[pl.BlockSpec((tm, K), lambda i: (i, 0)),
                  pl.BlockSpec((N, K), lambda i: (0, 0))],
        out_specs=pl.BlockSpec((tm, N), lambda i: (i, 0)),
        out_shape=jax.ShapeDtypeStruct((M, N), F32),
        compiler_params=_params("arbitrary"),
        name=name,
    )(a, wt)


def _first_row_tile():
    return pl.program_id(1) == 0


def _ws_matmul_nt(a, wt, l, col0, ncols, tn, name):
    M, K = a.shape
    tm = 1024
    jb = col0 // tn
    assert col0 % tn == 0 and ncols % tn == 0 and wt.dtype == BF16
    return pl.pallas_call(
        _mm_nt_kernel,
        grid=(ncols // tn, M // tm),
        in_specs=[pl.BlockSpec((tm, K), lambda j, i: (i, 0)),
                  pl.BlockSpec((None, tn, K), lambda j, i: (l, jb + j, 0))],
        out_specs=pl.BlockSpec((tm, tn), lambda j, i: (i, j)),
        out_shape=jax.ShapeDtypeStruct((M, ncols), F32),
        compiler_params=_params("arbitrary", "arbitrary"),
        name=name,
    )(a, wt)


def _ws_mm_res_kernel(a_ref, w_ref, x_ref, g_ref, o_ref, wb_ref):
    @pl.when(_first_row_tile())
    def _():
        wb_ref[...] = w_ref[...].astype(BF16)

    half = a_ref.shape[0] // 2
    for rows in (slice(0, half), slice(half, 2 * half)):
        acc = jnp.dot(a_ref[rows, :], wb_ref[...], preferred_element_type=F32)
        o_ref[rows, :] = x_ref[rows, :] + g_ref[...] * acc


def _mm_res_kernel(a_ref, w_ref, x_ref, g_ref, o_ref):
    acc = jnp.dot(a_ref[...], w_ref[...], preferred_element_type=F32)
    o_ref[...] = x_ref[...] + g_ref[...] * acc


def _matmul_gated_residual(a, w, l, x, mod, gate_idx, tm, tn, name):
    M, K = a.shape
    N = w.shape[2]
    nj = N // tn
    cast = w.dtype != BF16
    return pl.pallas_call(
        _ws_mm_res_kernel if cast else _mm_res_kernel,
        grid=(nj, M // tm),
        in_specs=[pl.BlockSpec((tm, K), lambda j, i: (i, 0)),
                  pl.BlockSpec((None, K, tn), lambda j, i: (l, 0, j)),
                  pl.BlockSpec((tm, tn), lambda j, i: (i, j)),
                  pl.BlockSpec((1, tn), lambda j, i: (0, gate_idx * nj + j))],
        out_specs=pl.BlockSpec((tm, tn), lambda j, i: (i, j)),
        out_shape=jax.ShapeDtypeStruct((M, N), F32),
        scratch_shapes=[pltpu.VMEM((K, tn), BF16)] if cast else [],
        compiler_params=_params("arbitrary", "arbitrary"),
        name=name,
    )(a, w, x, mod)


def _swiglu_kernel(a_ref, wg_ref, wv_ref, o_ref, wgb_ref, wvb_ref):
    @pl.when(_first_row_tile())
    def _():
        wgb_ref[...] = wg_ref[...].astype(BF16)
        wvb_ref[...] = wv_ref[...].astype(BF16)

    half = a_ref.shape[0] // 2
    for rows in (slice(0, half), slice(half, 2 * half)):
        a = a_ref[rows, :]
        ug = jnp.dot(a, wgb_ref[...], preferred_element_type=F32)
        uv = jnp.dot(a, wvb_ref[...], preferred_element_type=F32)
        o_ref[rows, :] = (ug * _sigmoid(ug) * uv).astype(o_ref.dtype)


def _swiglu_in(a, w, l):
    M, K = a.shape
    H = w.shape[2] // 2
    tm, tn = 2048, 512
    nj = H // tn
    return pl.pallas_call(
        _swiglu_kernel,
        grid=(nj, M // tm),
        in_specs=[pl.BlockSpec((tm, K), lambda j, i: (i, 0)),
                  pl.BlockSpec((None, K, tn), lambda j, i: (l, 0, j)),
                  pl.BlockSpec((None, K, tn), lambda j, i: (l, 0, nj + j))],
        out_specs=pl.BlockSpec((tm, tn), lambda j, i: (i, j)),
        out_shape=jax.ShapeDtypeStruct((M, H), BF16),
        scratch_shapes=[pltpu.VMEM((K, tn), BF16), pltpu.VMEM((K, tn), BF16)],
        compiler_params=_params("arbitrary", "arbitrary"),
        name="ffn_in_swiglu",
    )(a, w, w)


def _merge_kernel(h_ref, g0_ref, g1_ref, g2_ref, g3_ref, y0_ref, y1_ref, y2_ref, y3_ref,
                  wb_ref, o_ref, wbb_ref):
    @pl.when(_first_row_tile())
    def _():
        wbb_ref[...] = wb_ref[...].astype(BF16)

    half = h_ref.shape[0] // 2
    for rows in (slice(0, half), slice(half, 2 * half)):
        h = h_ref[rows, :]
        acc = None
        for n, (gt_ref, y_ref) in enumerate(((g0_ref, y0_ref), (g1_ref, y1_ref),
                                             (g2_ref, y2_ref), (g3_ref, y3_ref))):
            logits = lax.dot_general(h, gt_ref[...], _CONTRACT_LAST,
                                     preferred_element_type=F32)
            branch = jnp.dot(y_ref[rows, :], wbb_ref[n], preferred_element_type=F32)
            term = _sigmoid(logits) * branch
            acc = term if acc is None else acc + term
        o_ref[rows, :] = acc.astype(o_ref.dtype)


def _gated_merge(h, wt, ys, w_branch, l):
    M, K = h.shape
    D = w_branch.shape[3]
    BW = w_branch.shape[2]
    tm, tn = 1024, 512
    nj = D // tn
    assert wt.dtype == BF16
    gate_specs = [pl.BlockSpec((None, tn, K),
                               functools.partial(lambda j, i, n: (l, n * nj + j, 0), n=n))
                  for n in range(N_BRANCH)]
    y_specs = [pl.BlockSpec((tm, BW), lambda j, i: (i, 0)) for _ in range(N_BRANCH)]
    return pl.pallas_call(
        _merge_kernel,
        grid=(nj, M // tm),
        in_specs=[pl.BlockSpec((tm, K), lambda j, i: (i, 0))] + gate_specs + y_specs
                 + [pl.BlockSpec((None, N_BRANCH, BW, tn), lambda j, i: (l, 0, 0, j))],
        out_specs=pl.BlockSpec((tm, tn), lambda j, i: (i, j)),
        out_shape=jax.ShapeDtypeStruct((M, D), BF16),
        scratch_shapes=[pltpu.VMEM((N_BRANCH, BW, tn), BF16)],
        compiler_params=_params("arbitrary", "arbitrary"),
        name="gated_merge",
    )(h, wt, wt, wt, wt, *ys, w_branch)


def _lru_kernel(ax_ref, ag_ref, cw_ref, cb_ref, wr_ref, br_ref, wi_ref, bi_ref, lam_ref,
                o_ref, xbuf, hcar):
    i = pl.program_id(0)
    tt = ax_ref.shape[0]

    @pl.when(i == 0)
    def _():
        xbuf[0:SUBLANE, :] = jnp.zeros((SUBLANE, LRU_WIDTH), F32)
        hcar[...] = jnp.zeros_like(hcar)

    xbuf[SUBLANE:SUBLANE + tt, :] = ax_ref[...]
    xa = cb_ref[...]
    for j in range(CONV_WIDTH):
        xa = xa + cw_ref[j:j + 1, :] * xbuf[pl.ds(SUBLANE - (CONV_WIDTH - 1) + j, tt), :]
    xbuf[0:SUBLANE, :] = ax_ref[tt - SUBLANE:tt, :]

    r = _sigmoid(_bdot(xa, wr_ref[...]) + br_ref[...])
    ig = _sigmoid(_bdot(xa, wi_ref[...]) + bi_ref[...])
    log_a = (-LRU_C * _softplus(-lam_ref[...])) * r
    a = jnp.exp(log_a)
    th = jnp.tanh(log_a)
    u = jnp.sqrt(-2.0 * th / (1.0 - th)) * (ig * xa)

    row = lax.broadcasted_iota(jnp.int32, (tt, LRU_WIDTH), 0)
    d = 1
    while d < tt:
        keep = row >= d
        u_s = jnp.where(keep, pltpu.roll(u, d, 0), 0.0)
        a_s = jnp.where(keep, pltpu.roll(a, d, 0), 1.0)
        u = u + a * u_s
        a = a * a_s
        d *= 2
    h = u + a * hcar[...]
    hcar[...] = h[tt - 1:tt, :]

    g = ag_ref[...]
    gelu = 0.5 * g * (1.0 + jnp.tanh(math.sqrt(2.0 / math.pi) * (g + 0.044715 * (g * g * g))))
    o_ref[...] = (h * gelu).astype(o_ref.dtype)


def _lru_branch(p_a, conv_w, conv_b, wr_bd, br, wi_bd, bi, lam):
    T = p_a.shape[0]
    W = LRU_WIDTH
    tt = 256
    row = lambda i: (0, 0)
    return pl.pallas_call(
        _lru_kernel,
        grid=(T // tt,),
        in_specs=[pl.BlockSpec((tt, W), lambda i: (i, 0)),
                  pl.BlockSpec((tt, W), lambda i: (i, 1)),
                  pl.BlockSpec((CONV_WIDTH, W), row),
                  pl.BlockSpec((1, W), row),
                  pl.BlockSpec((W, W), row),
                  pl.BlockSpec((1, W), row),
                  pl.BlockSpec((W, W), row),
                  pl.BlockSpec((1, W), row),
                  pl.BlockSpec((1, W), row)],
        out_specs=pl.BlockSpec((tt, W), lambda i: (i, 0)),
        out_shape=jax.ShapeDtypeStruct((T, W), BF16),
        scratch_shapes=[pltpu.VMEM((tt + SUBLANE, W), F32), pltpu.VMEM((1, W), F32)],
        compiler_params=_params("arbitrary"),
        name="rg_lru",
    )(p_a, p_a, conv_w, conv_b, wr_bd, br, wi_bd, bi, lam)


def _mla_prep_kernel(p_ref, gcq_ref, gckv_ref, wuq_ref, wuk_ref, wuv_ref, gq_ref, gk_ref,
                     cc_ref, sa_ref, sb_ref, q_ref, k_ref, v_ref):
    p = p_ref[...]
    cq = p[:, :MLA_Q_LORA]
    ckv = p[:, MLA_Q_LORA:MLA_Q_LORA + MLA_KV_LORA]
    kr = p[:, MLA_Q_LORA + MLA_KV_LORA:]
    lane = lax.broadcasted_iota(jnp.int32, kr.shape, 1)
    kr = jnp.where(lane < MLA_ROPE_DIM, kr, 0.0)

    def rms(x, g, n):
        return x * lax.rsqrt(jnp.sum(x * x, axis=-1, keepdims=True) * (1.0 / n) + NORM_EPS) * g

    cqn = rms(cq, gcq_ref[...], MLA_Q_LORA)
    ckvn = rms(ckv, gckv_ref[...], MLA_KV_LORA)
    q = _bdot(cqn, wuq_ref[...])
    kn = _bdot(ckvn, wuk_ref[...])
    v_ref[...] = _bdot(ckvn, wuv_ref[...]).astype(v_ref.dtype)
    cc, sa, sb = cc_ref[...], sa_ref[...], sb_ref[...]
    half = MLA_ROPE_DIM // 2

    def rope(x):
        x2 = x[:, LANE:]
        x2 = x2 * cc + pltpu.roll(x2, LANE - half, 1) * sa + pltpu.roll(x2, half, 1) * sb
        return jnp.concatenate([x[:, :LANE], x2], axis=-1)

    scale = MLA_QK_DIM ** -0.5 * math.log2(math.e)
    for h in range(MLA_HEADS):
        qh = rms(q[:, h * MLA_QK_PAD:(h + 1) * MLA_QK_PAD], gq_ref[...], MLA_QK_DIM)
        q_ref[:, h * MLA_QK_PAD:(h + 1) * MLA_QK_PAD] = (rope(qh) * scale).astype(q_ref.dtype)
        kh = jnp.concatenate([kn[:, h * MLA_NOPE_DIM:(h + 1) * MLA_NOPE_DIM], kr], axis=-1)
        kh = rms(kh, gk_ref[...], MLA_QK_DIM)
        k_ref[:, h * MLA_QK_PAD:(h + 1) * MLA_QK_PAD] = rope(kh).astype(k_ref.dtype)


def _mla_prep(p_mla, g_cq, g_ckv, wuq, wuk, wuv, gq, gk, cc, sa, sb):
    T, PW = p_mla.shape
    tt = 512
    QW = MLA_HEADS * MLA_QK_PAD
    VW = MLA_HEADS * MLA_V_DIM
    row = lambda i: (0, 0)
    tab = pl.BlockSpec((tt, LANE), lambda i: (i, 0))
    return pl.pallas_call(
        _mla_prep_kernel,
        grid=(T // tt,),
        in_specs=[pl.BlockSpec((tt, PW), lambda i: (i, 0)),
                  pl.BlockSpec((1, MLA_Q_LORA), row),
                  pl.BlockSpec((1, MLA_KV_LORA), row),
                  pl.BlockSpec((MLA_Q_LORA, QW), row),
                  pl.BlockSpec((MLA_KV_LORA, MLA_HEADS * MLA_NOPE_DIM), row),
                  pl.BlockSpec((MLA_KV_LORA, VW), row),
                  pl.BlockSpec((1, MLA_QK_PAD), row),
                  pl.BlockSpec((1, MLA_QK_PAD), row),
                  tab, tab, tab],
        out_specs=[pl.BlockSpec((tt, QW), lambda i: (i, 0)),
                   pl.BlockSpec((tt, QW), lambda i: (i, 0)),
                   pl.BlockSpec((tt, VW), lambda i: (i, 0))],
        out_shape=[jax.ShapeDtypeStruct((T, QW), BF16),
                   jax.ShapeDtypeStruct((T, QW), BF16),
                   jax.ShapeDtypeStruct((T, VW), BF16)],
        compiler_params=_params("arbitrary"),
        name="mla_prep",
    )(p_mla, g_cq, g_ckv, wuq, wuk, wuv, gq, gk, cc, sa, sb)


ATTN_BLOCK = 512


ATTN_HEADS_PER_STEP = 4
ATTN_ROW_GROUP = 32


def _attn_kernel(q_ref, k_ref, v_ref, o_ref, s_scr, p_scr, m_scr, l_scr, a_scr, acc_scr, *, blk):
    i = pl.program_id(1)
    NH, RG = ATTN_HEADS_PER_STEP, ATTN_ROW_GROUP
    ntile = blk // LANE
    qk = [slice(t * MLA_QK_PAD, (t + 1) * MLA_QK_PAD) for t in range(NH)]
    vd = [slice(t * MLA_V_DIM, (t + 1) * MLA_V_DIM) for t in range(NH)]
    m_scr[...] = jnp.full(m_scr.shape, -jnp.inf, F32)
    l_scr[...] = jnp.zeros(l_scr.shape, F32)
    acc_scr[...] = jnp.zeros(acc_scr.shape, F32)

    def step(j, masked):
        start = pl.multiple_of(j * blk, blk)
        for t in range(NH):
            s_scr[t] = lax.dot_general(q_ref[:, qk[t]], k_ref[pl.ds(start, blk), qk[t]],
                                       (((1,), (1,)), ((), ())), preferred_element_type=F32)
        for t in range(NH):
            for g in range(blk // RG):
                r = slice(g * RG, (g + 1) * RG)
                s = s_scr[t, r, :]
                if masked:
                    row = g * RG + lax.broadcasted_iota(jnp.int32, (RG, blk), 0)
                    col = lax.broadcasted_iota(jnp.int32, (RG, blk), 1)
                    s = jnp.where(col <= row, s, -jnp.inf)
                    s_scr[t, r, :] = s
                m_old = m_scr[t, r, :]
                m_new = jnp.maximum(m_old, jnp.max(s, axis=-1, keepdims=True))
                a_scr[t, r, :] = jnp.exp2(m_old - m_new)
                m_scr[t, r, :] = m_new
            for g in range(blk // RG):
                r = slice(g * RG, (g + 1) * RG)
                m_new = m_scr[t, r, :]
                p = jnp.exp2(s_scr[t, r, :] - jnp.concatenate([m_new] * ntile, axis=1))
                psum = p[:, :LANE]
                for c in range(1, ntile):
                    psum = psum + p[:, c * LANE:(c + 1) * LANE]
                l_scr[t, r, :] = a_scr[t, r, :] * l_scr[t, r, :] + psum
                p_scr[t, r, :] = p.astype(BF16)
        for t in range(NH):
            acc_scr[t] = acc_scr[t] * a_scr[t] + jnp.dot(
                p_scr[t], v_ref[pl.ds(start, blk), vd[t]], preferred_element_type=F32)

    def body(j, carry):
        step(j, False)
        return carry

    lax.fori_loop(0, i, body, 0)
    step(i, True)
    for t in range(NH):
        l = jnp.sum(l_scr[t], axis=-1, keepdims=True)
        o_ref[:, vd[t]] = (acc_scr[t] / l).astype(o_ref.dtype)


def _causal_attention(q, k, v):
    T = q.shape[0]
    blk, NH = ATTN_BLOCK, ATTN_HEADS_PER_STEP
    assert MLA_V_DIM == LANE and MLA_HEADS % NH == 0
    stat = pltpu.VMEM((NH, blk, LANE), F32)
    return pl.pallas_call(
        functools.partial(_attn_kernel, blk=blk),
        grid=(MLA_HEADS // NH, T // blk),
        in_specs=[pl.BlockSpec((blk, NH * MLA_QK_PAD), lambda h, i: (i, h)),
                  pl.BlockSpec((T, NH * MLA_QK_PAD), lambda h, i: (0, h),
                               pipeline_mode=pl.Buffered(1)),
                  pl.BlockSpec((T, NH * MLA_V_DIM), lambda h, i: (0, h),
                               pipeline_mode=pl.Buffered(1))],
        out_specs=pl.BlockSpec((blk, NH * MLA_V_DIM), lambda h, i: (i, h)),
        out_shape=jax.ShapeDtypeStruct((T, MLA_HEADS * MLA_V_DIM), BF16),
        scratch_shapes=[pltpu.VMEM((NH, blk, blk), F32), pltpu.VMEM((NH, blk, blk), BF16),
                        stat, stat, stat, stat],
        compiler_params=_params("arbitrary", "arbitrary"),
        name="mla_attention",
    )(q, k, v)


_RET_LOG_GAMMA = np.log1p(-np.exp(np.linspace(math.log(1.0 / 32), math.log(1.0 / 512),
                                              RET_HEADS, dtype=np.float32))).astype(np.float32)


RET_GROUP = 4


def _ret_kernel(q_ref, k_ref, v_ref, g_ref, cc_ref, ss_ref, gn_ref, o_ref, st_ref):
    i = pl.program_id(0)
    C, Dh, H = RET_CHUNK, RET_HEAD_DIM, RET_HEADS

    @pl.when(i == 0)
    def _():
        st_ref[...] = jnp.zeros_like(st_ref)

    row = lax.broadcasted_iota(jnp.int32, (C, C), 0)
    col = lax.broadcasted_iota(jnp.int32, (C, C), 1)
    rel = (row - col).astype(F32)
    idx = lax.broadcasted_iota(jnp.int32, (C, 1), 0).astype(F32)
    lgs = [float(x) for x in _RET_LOG_GAMMA]
    decay_in = jnp.stack([jnp.where(rel >= 0, jnp.exp(lg * jnp.maximum(rel, 0.0)), 0.0)
                          for lg in lgs])
    q_dec = jnp.stack([jnp.exp(lg * (idx + 1.0)) for lg in lgs])
    k_dec = jnp.stack([jnp.exp(lg * (C - 1.0 - idx)) for lg in lgs])
    heads = [slice(h * Dh, (h + 1) * Dh) for h in range(H)]
    st = st_ref[...]
    for c in range(RET_GROUP):
        rows = slice(c * C, (c + 1) * C)
        cc, ss = cc_ref[rows, :], ss_ref[rows, :]

        def rope(ref):
            x = jnp.stack([ref[rows, sl] for sl in heads])
            return x * cc + pltpu.roll(x, Dh // 2, 2) * ss

        q = rope(q_ref)
        k = rope(k_ref) * (Dh ** -0.5)
        v = jnp.stack([v_ref[rows, sl] for sl in heads])
        inner = _bmm(q, k, _BNT) * decay_in
        o = _bmm(inner, v, _BNN) + _bmm(q * q_dec, st, _BNN)
        kv = _bmm(k * k_dec, v, _BTN)
        st = jnp.stack([math.exp(lg * C) * st[h] + kv[h] for h, lg in enumerate(lgs)])
        oc = o - jnp.mean(o, axis=-1, keepdims=True)
        on = oc * lax.rsqrt(jnp.mean(oc * oc, axis=-1, keepdims=True) + GN_EPS)
        for h, sl in enumerate(heads):
            g = g_ref[rows, sl]
            o_ref[rows, sl] = (g * _sigmoid(g) * (on[h] * gn_ref[:, sl])).astype(o_ref.dtype)
    st_ref[...] = st


def _retention_branch(p_ret, cc, ss, g_norm):
    T = p_ret.shape[0]
    C, W = RET_CHUNK * RET_GROUP, RET_WIDTH
    blk = lambda n: pl.BlockSpec((C, W), functools.partial(lambda i, n: (i, n), n=n))
    return pl.pallas_call(
        _ret_kernel,
        grid=(T // C,),
        in_specs=[blk(0), blk(1), blk(2), blk(3),
                  pl.BlockSpec((C, LANE), lambda i: (i, 0)),
                  pl.BlockSpec((C, LANE), lambda i: (i, 0)),
                  pl.BlockSpec((1, W), lambda i: (0, 0))],
        out_specs=pl.BlockSpec((C, W), lambda i: (i, 0)),
        out_shape=jax.ShapeDtypeStruct((T, W), BF16),
        scratch_shapes=[pltpu.VMEM((RET_HEADS, RET_HEAD_DIM, RET_HEAD_DIM), F32)],
        compiler_params=_params("arbitrary"),
        name="retention",
    )(p_ret, p_ret, p_ret, p_ret, cc, ss, g_norm)


RWKV_PAD_COLS = 3 * RWKV_WIDTH + 2 * LANE
assert RWKV_DECAY_LORA + RWKV_AAA_LORA == LANE and RWKV_GATE_LORA == LANE


def _split_dot(a, b_exact, terms):
    acc = None
    rem = a
    for _ in range(terms):
        piece = rem.astype(BF16)
        rem = rem - piece.astype(F32)
        d = jnp.dot(piece, b_exact, preferred_element_type=F32)
        acc = d if acc is None else acc + d
    return acc


def _rwkv_prep_kernel(p_ref, mu_ref, w0_ref, a0_ref, wup_ref, aup_ref, gup_ref, kk_ref, ka_ref,
                      rk_ref, ones_ref, tri_ref,
                      kp_ref, rp_ref, kn_ref, bn_ref, knp_ref, bnp_ref, v_ref, g_ref, bonus_ref,
                      pc_ref, buf):
    i = pl.program_id(0)
    tt = p_ref.shape[0]
    W = RWKV_WIDTH
    C = RWKV_CHUNK

    @pl.when(i == 0)
    def _():
        buf[0:SUBLANE, :] = jnp.zeros((SUBLANE, RWKV_PAD_COLS), F32)

    p = p_ref[...]
    buf[SUBLANE:SUBLANE + tt, :] = p
    prev = buf[pl.ds(SUBLANE - 1, tt), :]
    buf[0:SUBLANE, :] = p_ref[tt - SUBLANE:tt, :]
    xs = p + (prev - p) * mu_ref[...]
    r = xs[:, 0:W]
    k = xs[:, W:2 * W]
    v = xs[:, 2 * W:3 * W]
    lora = xs[:, 3 * W:3 * W + LANE]
    gd = xs[:, 3 * W + LANE:3 * W + 2 * LANE]

    w_log = -_softplus(-(w0_ref[...] + _fdot(jnp.tanh(lora), wup_ref[...]))) - 0.5
    logw = -jnp.exp(w_log)
    a = _sigmoid(a0_ref[...] + _bdot(lora, aup_ref[...]))
    g_ref[...] = _bdot(_sigmoid(gd), gup_ref[...])

    ones_bd = ones_ref[...]
    kk = k * kk_ref[...]
    kk = kk / jnp.maximum(jnp.sqrt(_split_dot(kk * kk, ones_bd, 2)), 1e-12)
    kt = k * (1.0 + (a - 1.0) * ka_ref[...])
    bonus_ref[...] = _split_dot(r * kt * rk_ref[...], ones_bd, 2) * v
    v_ref[...] = v.astype(v_ref.dtype)

    tri = tri_ref[...]
    cum = jnp.concatenate([_split_dot_lhs_exact(tri, logw[c * C:(c + 1) * C], 3)
                           for c in range(tt // C)], axis=0)
    tot = jnp.sum(logw.reshape(tt // C, C, W), axis=1)
    pc_ref[...] = jnp.exp(tot)
    to_end = jnp.exp((tot[:, None, :] - cum.reshape(tt // C, C, W)).reshape(tt, W))
    e_neg = jnp.exp(-cum)
    beta = kk * a
    kp_ref[...] = (kk * jnp.exp(cum - logw)).astype(kp_ref.dtype)
    rp_ref[...] = (r * jnp.exp(cum)).astype(rp_ref.dtype)
    kn_ref[...] = (kt * e_neg).astype(kn_ref.dtype)
    bn_ref[...] = (beta * e_neg).astype(bn_ref.dtype)
    knp_ref[...] = (kt * to_end).astype(knp_ref.dtype)
    bnp_ref[...] = (beta * to_end).astype(bnp_ref.dtype)


def _split_dot_lhs_exact(a_exact, b, terms):
    acc = None
    rem = b
    for _ in range(terms):
        piece = rem.astype(BF16)
        rem = rem - piece.astype(F32)
        d = jnp.dot(a_exact, piece, preferred_element_type=F32)
        acc = d if acc is None else acc + d
    return acc


def _rwkv_prep(p_rw, mu, w0, a0, wup, aup, gup, k_k, k_a, r_k):
    T = p_rw.shape[0]
    W, C = RWKV_WIDTH, RWKV_CHUNK
    tt = 512
    hid = np.arange(W) // RWKV_HEAD_DIM
    ones_bd = jnp.asarray((hid[:, None] == hid[None, :]).astype(np.float32), dtype=BF16)
    tid = np.arange(C)
    tri = jnp.asarray((tid[:, None] >= tid[None, :]).astype(np.float32), dtype=BF16)
    row = lambda i: (0, 0)
    vec = pl.BlockSpec((1, W), row)
    lora = pl.BlockSpec((LANE, W), row)
    out = pl.BlockSpec((tt, W), lambda i: (i, 0))
    lo = jax.ShapeDtypeStruct((T, W), BF16)
    hi = jax.ShapeDtypeStruct((T, W), F32)
    return pl.pallas_call(
        _rwkv_prep_kernel,
        grid=(T // tt,),
        in_specs=[pl.BlockSpec((tt, RWKV_PAD_COLS), lambda i: (i, 0)),
                  pl.BlockSpec((1, RWKV_PAD_COLS), row),
                  vec, vec, lora, lora, lora, vec, vec, vec,
                  pl.BlockSpec((W, W), row),
                  pl.BlockSpec((C, C), row)],
        out_specs=[out] * 9 + [pl.BlockSpec((tt // C, W), lambda i: (i, 0))],
        out_shape=[lo] * 7 + [hi, hi, jax.ShapeDtypeStruct((T // C, W), F32)],
        scratch_shapes=[pltpu.VMEM((tt + SUBLANE, RWKV_PAD_COLS), F32)],
        compiler_params=_params("arbitrary"),
        name="rwkv_prep",
    )(p_rw, mu, w0, a0, wup, aup, gup, k_k, k_a, r_k, ones_bd, tri)


def _bmm(a, b, dims):
    return lax.dot_general(a.astype(BF16), b.astype(BF16), dims, preferred_element_type=F32)


_BNN = (((2,), (1,)), ((0,), (0,)))
_BNT = (((2,), (2,)), ((0,), (0,)))
_BTN = (((1,), (1,)), ((0,), (0,)))
RWKV_GROUP = 4


def _rwkv_rec_kernel(kp_ref, rp_ref, kn_ref, bn_ref, knp_ref, bnp_ref, v_ref, pc_ref, gn_ref,
                     bonus_ref, g_ref, y_ref, s_ref, *, nchunk):
    C = RWKV_CHUNK
    P = 2 * C
    assert P == LANE

    @pl.when(pl.program_id(0) == 0)
    def _():
        s_ref[...] = jnp.zeros_like(s_ref)

    row = lax.broadcasted_iota(jnp.int32, (P, P), 0)
    col = lax.broadcasted_iota(jnp.int32, (P, P), 1)

    def blocks(n):
        return jnp.where((row // n) == (col // n), 1.0, 0.0)

    own = blocks(C)
    own_bf = own.astype(BF16)
    lower = jnp.where(row > col, 1.0, 0.0)
    m_strict = own * lower
    m_incl = own * jnp.where(row >= col, 1.0, 0.0)
    m_base = blocks(16) * lower
    m_l32 = blocks(32) * lower - m_base
    m_l64 = m_strict - m_base - m_l32
    eye = jnp.where(row == col, 1.0, 0.0)

    npair = RWKV_HEADS // 2
    lanes = [slice(p * LANE, (p + 1) * LANE) for p in range(npair)]
    G = RWKV_GROUP

    def group(gi, carry):
        rows = [pl.ds(pl.multiple_of((gi * G + c) * C, C), C) for c in range(G)]

        def load(ref):
            x = jnp.stack([ref[sl, ln] for sl in rows for ln in lanes])
            return jnp.concatenate([x, x], axis=1) * own_bf

        kp, rp, kn, bn, knp, bnp, v = (load(r) for r in (kp_ref, rp_ref, kn_ref, bn_ref,
                                                         knp_ref, bnp_ref, v_ref))
        q2 = jnp.concatenate([kp, rp], axis=1)
        a_kn = _bmm(q2, kn, _BNT)
        a_bn = _bmm(q2, bn, _BNT)
        a_k = a_kn[:, :P] * m_strict
        a_rk = a_kn[:, P:] * m_incl
        a_b = a_bn[:, :P]
        a_rb = a_bn[:, P:] * m_incl

        n1 = -(a_b * m_base)
        inv = eye + n1
        n2 = _bmm(n1, n1, _BNN)
        inv = inv + _bmm(inv, n2, _BNN)
        n4 = _bmm(n2, n2, _BNN)
        inv = inv + _bmm(inv, n4, _BNN)
        n8 = _bmm(n4, n4, _BNN)
        inv = inv + _bmm(inv, n8, _BNN)
        for msk in (m_l32, m_l64):
            inv = inv - _bmm(inv, _bmm(a_b * msk, inv, _BNN), _BNN)
        av = _bmm(jnp.concatenate([a_k, a_rk], axis=1), v, _BNN)
        k2 = jnp.concatenate([knp, bnp], axis=1)

        s = s_ref[...]
        inv_n = 1.0 / RWKV_HEAD_DIM
        for c in range(G):
            b = slice(c * npair, (c + 1) * npair)
            x1 = _bmm(q2[b], s, _BNT)
            u = _bmm(inv[b], x1[:, :P] + av[b, :P], _BNN)
            o = x1[:, P:] + av[b, P:] - _bmm(a_rb[b], u, _BNN)
            vu = jnp.concatenate([v[b], (-u).astype(BF16)], axis=1)
            pc = pc_ref[gi * G + c]
            pc = jnp.stack([pc[:, ln] for ln in lanes])
            s = s * pc + _bmm(vu, k2[b], _BTN)

            oc = (o - jnp.sum(o, axis=-1, keepdims=True) * inv_n) * own
            on = oc * lax.rsqrt(jnp.sum(oc * oc, axis=-1, keepdims=True) * inv_n + GN_EPS)
            on = on[:, :C] + on[:, C:]
            for p, ln in enumerate(lanes):
                y = (on[p] * gn_ref[:, ln] + bonus_ref[rows[c], ln]) * g_ref[rows[c], ln]
                y_ref[rows[c], ln] = y.astype(y_ref.dtype)
        s_ref[...] = s
        return carry

    lax.fori_loop(0, nchunk // G, group, 0)


def _rwkv_recurrence(kp, rp, kn, bn, knp, bnp, v, pc, gn, bonus, g):
    T, W = kp.shape
    C = RWKV_CHUNK
    tb = 512
    nchunk = tb // C
    blk = pl.BlockSpec((tb, W), lambda i: (i, 0))
    return pl.pallas_call(
        functools.partial(_rwkv_rec_kernel, nchunk=nchunk),
        grid=(T // tb,),
        in_specs=[blk] * 7 + [pl.BlockSpec((nchunk, 1, W), lambda i: (i, 0, 0)),
                              pl.BlockSpec((1, W), lambda i: (0, 0)), blk, blk],
        out_specs=blk,
        out_shape=jax.ShapeDtypeStruct((T, W), BF16),
        scratch_shapes=[pltpu.VMEM((RWKV_HEADS // 2, LANE, LANE), F32)],
        compiler_params=_params("arbitrary"),
        name="rwkv_recurrence",
    )(kp, rp, kn, bn, knp, bnp, v, pc, gn, bonus, g)


def _rwkv_branch(p_rw, mu, w0, a0, wup, aup, gup, k_k, k_a, r_k, g_norm):
    kp, rp, kn, bn, knp, bnp, v, g, bonus, pc = _rwkv_prep(p_rw, mu, w0, a0, wup, aup, gup,
                                                          k_k, k_a, r_k)
    pc = pc.reshape(pc.shape[0], 1, pc.shape[1])
    return _rwkv_recurrence(kp, rp, kn, bn, knp, bnp, v, pc, g_norm.reshape(1, -1), bonus, g)


def _block_diag(w):
    n, bi, bj = w.shape
    eye = jnp.eye(n, dtype=w.dtype)
    return (eye[:, None, :, None] * w[:, :, None, :]).reshape(n * bi, n * bj)


def _rope_tables(positions):
    pos = positions.astype(F32).reshape(-1, 1)
    T = pos.shape[0]

    def cs(dim):
        inv = 1.0 / (ROPE_BASE ** (jnp.arange(0, dim, 2, dtype=F32) / dim))
        ang = pos * inv
        return jnp.cos(ang), jnp.sin(ang)

    cm, sm = cs(MLA_ROPE_DIM)
    z32 = jnp.zeros((T, MLA_ROPE_DIM // 2), F32)
    z64 = jnp.zeros((T, LANE - MLA_ROPE_DIM), F32)
    mla = (jnp.concatenate([cm, cm, z64], axis=1),
           jnp.concatenate([-sm, z32, z64], axis=1),
           jnp.concatenate([z32, sm, z64], axis=1))
    cr, sr = cs(RET_HEAD_DIM)
    ret = (jnp.concatenate([cr, cr], axis=1), jnp.concatenate([-sr, sr], axis=1))
    return mla, ret


def _mla_weights(w_uq, w_ukv, g_qn, g_kn):
    wq = w_uq.reshape(MLA_Q_LORA, MLA_HEADS, MLA_QK_DIM)
    wq = jnp.pad(wq, ((0, 0), (0, 0), (0, MLA_QK_PAD - MLA_QK_DIM)))
    wq = wq.reshape(MLA_Q_LORA, MLA_HEADS * MLA_QK_PAD)
    wkv = w_ukv.reshape(MLA_KV_LORA, MLA_HEADS, MLA_NOPE_DIM + MLA_V_DIM)
    wk = wkv[:, :, :MLA_NOPE_DIM].reshape(MLA_KV_LORA, MLA_HEADS * MLA_NOPE_DIM)
    wv = wkv[:, :, MLA_NOPE_DIM:].reshape(MLA_KV_LORA, MLA_HEADS * MLA_V_DIM)
    pad = MLA_QK_PAD - MLA_QK_DIM
    gq = jnp.pad(g_qn, (0, pad)).reshape(1, MLA_QK_PAD)
    gk = jnp.pad(g_kn, (0, pad)).reshape(1, MLA_QK_PAD)
    return wq.astype(BF16), wk.astype(BF16), wv.astype(BF16), gq, gk


def kernel(x, c, positions, ada_w, ada_b, norm_mix, norm_ffn, w_in, conv_w, conv_b, lru_wr, lru_br, lru_wi, lru_bi, lru_lam, mla_g_cq, mla_g_ckv, mla_w_uq, mla_w_ukv, mla_g_qn, mla_g_kn, ret_g_norm, rwkv_mu, rwkv_w0, rwkv_w_up, rwkv_a0, rwkv_a_up, rwkv_g_up, rwkv_k_k, rwkv_k_a, rwkv_r_k, rwkv_g_norm, w_branch, w_out, ffn_w_in, ffn_w_out):
    B, T, D = x.shape
    assert B == 1 and D == D_MODEL
    depth = ada_w.shape[0]
    xt = x.reshape(T, D)
    mod_all = _modulation(c, ada_w, ada_b)
    (cc_m, sa_m, sb_m), (cc_r, ss_r) = _rope_tables(positions)
    w_in_t = jnp.swapaxes(w_in, 1, 2).astype(BF16)
    ffn_w_out_bf = ffn_w_out.astype(BF16)

    o_a = GATE_COLS
    o_cq = o_a + 2 * LRU_WIDTH
    o_ckv = o_cq + MLA_Q_LORA
    o_kr = o_ckv + MLA_KV_LORA
    o_ret = o_kr + MLA_ROPE_DIM
    o_rw = o_ret + 4 * RET_WIDTH

    for l in range(depth):
        mod = mod_all[l]
        dl, al = RWKV_DECAY_LORA, RWKV_AAA_LORA
        w_ret_t = w_in_t[l, o_ret:o_rw]
        w_rw_t = w_in_t[l, o_rw:]
        w_up_p = jnp.pad(rwkv_w_up[l], ((0, al), (0, 0)))
        a_up_p = jnp.pad(rwkv_a_up[l], ((dl, 0), (0, 0)))

        h = _mod_norm(xt, norm_mix[l].reshape(1, D), mod, 0)
        p_a = _ws_matmul_nt(h, w_in_t, l, o_a, 2 * LRU_WIDTH, 512, "in_proj_lru")
        p_mla = _ws_matmul_nt(h, w_in_t, l, o_cq, o_ret - o_cq + MLA_ROPE_DIM, 768, "in_proj_mla")
        p_ret = _matmul_nt(h, w_ret_t, "in_proj_ret")
        p_rw = _matmul_nt(h, w_rw_t, "in_proj_rwkv")

        y_a = _lru_branch(p_a, conv_w[l], conv_b[l].reshape(1, -1),
                          _block_diag(lru_wr[l]).astype(BF16), lru_br[l].reshape(1, -1),
                          _block_diag(lru_wi[l]).astype(BF16), lru_bi[l].reshape(1, -1),
                          lru_lam[l].reshape(1, -1))

        wq, wk, wv, gq, gk = _mla_weights(mla_w_uq[l], mla_w_ukv[l], mla_g_qn[l], mla_g_kn[l])
        q, k, v = _mla_prep(p_mla, mla_g_cq[l].reshape(1, -1), mla_g_ckv[l].reshape(1, -1),
                            wq, wk, wv, gq, gk, cc_m, sa_m, sb_m)
        y_b = _causal_attention(q, k, v)

        y_c = _retention_branch(p_ret, cc_r, ss_r, ret_g_norm[l].reshape(1, -1))

        y_d = _rwkv_branch(p_rw, rwkv_mu[l].reshape(1, -1), rwkv_w0[l].reshape(1, -1),
                           rwkv_a0[l].reshape(1, -1), w_up_p, a_up_p,
                           rwkv_g_up[l], rwkv_k_k[l].reshape(1, -1), rwkv_k_a[l].reshape(1, -1),
                           rwkv_r_k[l].reshape(1, -1), rwkv_g_norm[l])

        merged = _gated_merge(h, w_in_t, (y_a, y_b, y_c, y_d), w_branch, l)
        xt = _matmul_gated_residual(merged, w_out, l, xt, mod, 2, 1024, 1024, "out_proj")

        h2 = _mod_norm(xt, norm_ffn[l].reshape(1, D), mod, 3)
        act = _swiglu_in(h2, ffn_w_in, l)
        xt = _matmul_gated_residual(act, ffn_w_out_bf, l, xt, mod, 5, 512, 1024, "ffn_out")
    return xt.reshape(B, T, D)
```

```python
import functools
import math

import numpy as np
import jax
import jax.numpy as jnp
from jax import lax
from jax.experimental import pallas as pl
from jax.experimental.pallas import tpu as pltpu

F32 = jnp.float32
BF16 = jnp.bfloat16
HIGHEST = lax.Precision.HIGHEST

D_MODEL = 2048
N_BRANCH = 4
BRANCH_WIDTH = D_MODEL // N_BRANCH
NORM_EPS = 1e-6
GN_EPS = 1e-5
ROPE_BASE = 10000.0
LRU_WIDTH = BRANCH_WIDTH
LRU_C = 8.0
CONV_WIDTH = 4
MLA_HEADS = 4
MLA_NOPE_DIM = 128
MLA_ROPE_DIM = 64
MLA_V_DIM = 128
MLA_QK_DIM = MLA_NOPE_DIM + MLA_ROPE_DIM
MLA_QK_PAD = 256
MLA_Q_LORA = 384
MLA_KV_LORA = 256
RET_HEADS = 4
RET_HEAD_DIM = 128
RET_WIDTH = RET_HEADS * RET_HEAD_DIM
RET_CHUNK = 128
RWKV_HEAD_DIM = 64
RWKV_HEADS = 8
RWKV_WIDTH = RWKV_HEADS * RWKV_HEAD_DIM
RWKV_DECAY_LORA = 64
RWKV_AAA_LORA = 64
RWKV_GATE_LORA = 128
RWKV_CHUNK = 64
FFN_HIDDEN = 5632
GATE_COLS = N_BRANCH * D_MODEL
LANE = 128
SUBLANE = 8
VMEM_LIMIT_BYTES = 56 * 1024 * 1024


def _params(*sem):
    return pltpu.CompilerParams(dimension_semantics=sem, vmem_limit_bytes=VMEM_LIMIT_BYTES)


def _sigmoid(x):
    return 1.0 / (1.0 + jnp.exp(-x))


def _softplus(x):
    return jnp.maximum(x, 0.0) + jnp.log(1.0 + jnp.exp(-jnp.abs(x)))


def _bdot(a, b):
    return jnp.dot(a.astype(BF16), b.astype(BF16), preferred_element_type=F32)


def _fdot(a, b):
    return jnp.dot(a, b, preferred_element_type=F32, precision=HIGHEST)


def _mod_kernel(c_ref, w_ref, b_ref, o_ref):
    c = c_ref[...]
    ca = c * _sigmoid(c)
    o_ref[0] = jnp.sum(ca * w_ref[0], axis=0, keepdims=True) + b_ref[0]


def _modulation(c, ada_w, ada_b):
    L, D, N = ada_w.shape
    tn = 1024
    return pl.pallas_call(
        _mod_kernel,
        grid=(L, N // tn),
        in_specs=[pl.BlockSpec((D, 1), lambda l, j: (0, 0)),
                  pl.BlockSpec((1, D, tn), lambda l, j: (l, 0, j)),
                  pl.BlockSpec((1, 1, tn), lambda l, j: (l, 0, j))],
        out_specs=pl.BlockSpec((1, 1, tn), lambda l, j: (l, 0, j)),
        out_shape=jax.ShapeDtypeStruct((L, 1, N), F32),
        compiler_params=_params("arbitrary", "arbitrary"),
        name="adaln_mod",
    )(c.reshape(D, 1), ada_w, ada_b.reshape(L, 1, N))


NORM_ROW_GROUP = 16


def _norm_kernel(x_ref, g_ref, sh_ref, sc_ref, o_ref):
    gain = g_ref[...] * (1.0 + sc_ref[...])
    shift = sh_ref[...]

    def rows(i, carry):
        r = pl.ds(pl.multiple_of(i * NORM_ROW_GROUP, NORM_ROW_GROUP), NORM_ROW_GROUP)
        x = x_ref[r, :]
        ms = jnp.mean(x * x, axis=-1, keepdims=True)
        o_ref[r, :] = (x * lax.rsqrt(ms + NORM_EPS) * gain + shift).astype(o_ref.dtype)
        return carry

    lax.fori_loop(0, x_ref.shape[0] // NORM_ROW_GROUP, rows, 0, unroll=4)


def _mod_norm(x, g, mod, shift_idx):
    T, D = x.shape
    tm = 1024
    return pl.pallas_call(
        _norm_kernel,
        grid=(T // tm,),
        in_specs=[pl.BlockSpec((tm, D), lambda i: (i, 0)),
                  pl.BlockSpec((1, D), lambda i: (0, 0)),
                  pl.BlockSpec((1, D), lambda i: (0, shift_idx)),
                  pl.BlockSpec((1, D), lambda i: (0, shift_idx + 1))],
        out_specs=pl.BlockSpec((tm, D), lambda i: (i, 0)),
        out_shape=jax.ShapeDtypeStruct((T, D), BF16),
        compiler_params=_params("arbitrary"),
        name="mod_norm",
    )(x, g, mod, mod)


_CONTRACT_LAST = (((1,), (1,)), ((), ()))


def _mm_nt_kernel(a_ref, wt_ref, o_ref):
    o_ref[...] = lax.dot_general(a_ref[...], wt_ref[...], _CONTRACT_LAST,
                                 preferred_element_type=F32).astype(o_ref.dtype)


def _matmul_nt(a, wt, name):
    M, K = a.shape
    N = wt.shape[0]
    tm = 1024
    return pl.pallas_call(
        _mm_nt_kernel,
        grid=(M // tm,),
        in_specs=[pl.BlockSpec((tm, K), lambda i: (i, 0)),
                  pl.BlockSpec((N, K), lambda i: (0, 0))],
        out_specs=pl.BlockSpec((tm, N), lambda i: (i, 0)),
        out_shape=jax.ShapeDtypeStruct((M, N), F32),
        compiler_params=_params("arbitrary"),
        name=name,
    )(a, wt)


def _first_row_tile():
    return pl.program_id(1) == 0


def _ws_matmul_nt(a, wt, l, col0, ncols, tn, name):
    M, K = a.shape
    tm = 1024
    jb = col0 // tn
    assert col0 % tn == 0 and ncols % tn == 0 and wt.dtype == BF16
    return pl.pallas_call(
        _mm_nt_kernel,
        grid=(ncols // tn, M // tm),
        in_specs=[pl.BlockSpec((tm, K), lambda j, i: (i, 0)),
                  pl.BlockSpec((None, tn, K), lambda j, i: (l, jb + j, 0))],
        out_specs=pl.BlockSpec((tm, tn), lambda j, i: (i, j)),
        out_shape=jax.ShapeDtypeStruct((M, ncols), F32),
        compiler_params=_params("arbitrary", "arbitrary"),
        name=name,
    )(a, wt)


def _ws_mm_res_kernel(a_ref, w_ref, x_ref, g_ref, o_ref, wb_ref):
    @pl.when(_first_row_tile())
    def _():
        wb_ref[...] = w_ref[...].astype(BF16)

    half = a_ref.shape[0] // 2
    for rows in (slice(0, half), slice(half, 2 * half)):
        acc = jnp.dot(a_ref[rows, :], wb_ref[...], preferred_element_type=F32)
        o_ref[rows, :] = x_ref[rows, :] + g_ref[...] * acc


def _mm_res_kernel(a_ref, w_ref, x_ref, g_ref, o_ref):
    acc = jnp.dot(a_ref[...], w_ref[...], preferred_element_type=F32)
    o_ref[...] = x_ref[...] + g_ref[...] * acc


def _matmul_gated_residual(a, w, l, x, mod, gate_idx, tm, tn, name):
    M, K = a.shape
    N = w.shape[2]
    nj = N // tn
    cast = w.dtype != BF16
    return pl.pallas_call(
        _ws_mm_res_kernel if cast else _mm_res_kernel,
        grid=(nj, M // tm),
        in_specs=[pl.BlockSpec((tm, K), lambda j, i: (i, 0)),
                  pl.BlockSpec((None, K, tn), lambda j, i: (l, 0, j)),
                  pl.BlockSpec((tm, tn), lambda j, i: (i, j)),
                  pl.BlockSpec((1, tn), lambda j, i: (0, gate_idx * nj + j))],
        out_specs=pl.BlockSpec((tm, tn), lambda j, i: (i, j)),
        out_shape=jax.ShapeDtypeStruct((M, N), F32),
        scratch_shapes=[pltpu.VMEM((K, tn), BF16)] if cast else [],
        compiler_params=_params("arbitrary", "arbitrary"),
        name=name,
    )(a, w, x, mod)


def _swiglu_kernel(a_ref, wg_ref, wv_ref, o_ref, wgb_ref, wvb_ref):
    @pl.when(_first_row_tile())
    def _():
        wgb_ref[...] = wg_ref[...].astype(BF16)
        wvb_ref[...] = wv_ref[...].astype(BF16)

    half = a_ref.shape[0] // 2
    for rows in (slice(0, half), slice(half, 2 * half)):
        a = a_ref[rows, :]
        ug = jnp.dot(a, wgb_ref[...], preferred_element_type=F32)
        uv = jnp.dot(a, wvb_ref[...], preferred_element_type=F32)
        o_ref[rows, :] = (ug * _sigmoid(ug) * uv).astype(o_ref.dtype)


def _swiglu_in(a, w, l):
    M, K = a.shape
    H = w.shape[2] // 2
    tm, tn = 2048, 512
    nj = H // tn
    return pl.pallas_call(
        _swiglu_kernel,
        grid=(nj, M // tm),
        in_specs=[pl.BlockSpec((tm, K), lambda j, i: (i, 0)),
                  pl.BlockSpec((None, K, tn), lambda j, i: (l, 0, j)),
                  pl.BlockSpec((None, K, tn), lambda j, i: (l, 0, nj + j))],
        out_specs=pl.BlockSpec((tm, tn), lambda j, i: (i, j)),
        out_shape=jax.ShapeDtypeStruct((M, H), BF16),
        scratch_shapes=[pltpu.VMEM((K, tn), BF16), pltpu.VMEM((K, tn), BF16)],
        compiler_params=_params("arbitrary", "arbitrary"),
        name="ffn_in_swiglu",
    )(a, w, w)


def _merge_kernel(h_ref, g0_ref, g1_ref, g2_ref, g3_ref, y0_ref, y1_ref, y2_ref, y3_ref,
                  wb_ref, o_ref, wbb_ref):
    @pl.when(_first_row_tile())
    def _():
        wbb_ref[...] = wb_ref[...].astype(BF16)

    half = h_ref.shape[0] // 2
    for rows in (slice(0, half), slice(half, 2 * half)):
        h = h_ref[rows, :]
        acc = None
        for n, (gt_ref, y_ref) in enumerate(((g0_ref, y0_ref), (g1_ref, y1_ref),
                                             (g2_ref, y2_ref), (g3_ref, y3_ref))):
            logits = lax.dot_general(h, gt_ref[...], _CONTRACT_LAST,
                                     preferred_element_type=F32)
            branch = jnp.dot(y_ref[rows, :], wbb_ref[n], preferred_element_type=F32)
            term = _sigmoid(logits) * branch
            acc = term if acc is None else acc + term
        o_ref[rows, :] = acc.astype(o_ref.dtype)


def _gated_merge(h, wt, ys, w_branch, l):
    M, K = h.shape
    D = w_branch.shape[3]
    BW = w_branch.shape[2]
    tm, tn = 1024, 512
    nj = D // tn
    assert wt.dtype == BF16
    gate_specs = [pl.BlockSpec((None, tn, K),
                               functools.partial(lambda j, i, n: (l, n * nj + j, 0), n=n))
                  for n in range(N_BRANCH)]
    y_specs = [pl.BlockSpec((tm, BW), lambda j, i: (i, 0)) for _ in range(N_BRANCH)]
    return pl.pallas_call(
        _merge_kernel,
        grid=(nj, M // tm),
        in_specs=[pl.BlockSpec((tm, K), lambda j, i: (i, 0))] + gate_specs + y_specs
                 + [pl.BlockSpec((None, N_BRANCH, BW, tn), lambda j, i: (l, 0, 0, j))],
        out_specs=pl.BlockSpec((tm, tn), lambda j, i: (i, j)),
        out_shape=jax.ShapeDtypeStruct((M, D), BF16),
        scratch_shapes=[pltpu.VMEM((N_BRANCH, BW, tn), BF16)],
        compiler_params=_params("arbitrary", "arbitrary"),
        name="gated_merge",
    )(h, wt, wt, wt, wt, *ys, w_branch)


def _lru_kernel(ax_ref, ag_ref, cw_ref, cb_ref, wr_ref, br_ref, wi_ref, bi_ref, lam_ref,
                o_ref, xbuf, hcar):
    i = pl.program_id(0)
    tt = ax_ref.shape[0]

    @pl.when(i == 0)
    def _():
        xbuf[0:SUBLANE, :] = jnp.zeros((SUBLANE, LRU_WIDTH), F32)
        hcar[...] = jnp.zeros_like(hcar)

    xbuf[SUBLANE:SUBLANE + tt, :] = ax_ref[...]
    xa = cb_ref[...]
    for j in range(CONV_WIDTH):
        xa = xa + cw_ref[j:j + 1, :] * xbuf[pl.ds(SUBLANE - (CONV_WIDTH - 1) + j, tt), :]
    xbuf[0:SUBLANE, :] = ax_ref[tt - SUBLANE:tt, :]

    r = _sigmoid(_bdot(xa, wr_ref[...]) + br_ref[...])
    ig = _sigmoid(_bdot(xa, wi_ref[...]) + bi_ref[...])
    log_a = (-LRU_C * _softplus(-lam_ref[...])) * r
    a = jnp.exp(log_a)
    th = jnp.tanh(log_a)
    u = jnp.sqrt(-2.0 * th / (1.0 - th)) * (ig * xa)

    row = lax.broadcasted_iota(jnp.int32, (tt, LRU_WIDTH), 0)
    d = 1
    while d < tt:
        keep = row >= d
        u_s = jnp.where(keep, pltpu.roll(u, d, 0), 0.0)
        a_s = jnp.where(keep, pltpu.roll(a, d, 0), 1.0)
        u = u + a * u_s
        a = a * a_s
        d *= 2
    h = u + a * hcar[...]
    hcar[...] = h[tt - 1:tt, :]

    g = ag_ref[...]
    gelu = 0.5 * g * (1.0 + jnp.tanh(math.sqrt(2.0 / math.pi) * (g + 0.044715 * (g * g * g))))
    o_ref[...] = (h * gelu).astype(o_ref.dtype)


def _lru_branch(p_a, conv_w, conv_b, wr_bd, br, wi_bd, bi, lam):
    T = p_a.shape[0]
    W = LRU_WIDTH
    tt = 256
    row = lambda i: (0, 0)
    return pl.pallas_call(
        _lru_kernel,
        grid=(T // tt,),
        in_specs=[pl.BlockSpec((tt, W), lambda i: (i, 0)),
                  pl.BlockSpec((tt, W), lambda i: (i, 1)),
                  pl.BlockSpec((CONV_WIDTH, W), row),
                  pl.BlockSpec((1, W), row),
                  pl.BlockSpec((W, W), row),
                  pl.BlockSpec((1, W), row),
                  pl.BlockSpec((W, W), row),
                  pl.BlockSpec((1, W), row),
                  pl.BlockSpec((1, W), row)],
        out_specs=pl.BlockSpec((tt, W), lambda i: (i, 0)),
        out_shape=jax.ShapeDtypeStruct((T, W), BF16),
        scratch_shapes=[pltpu.VMEM((tt + SUBLANE, W), F32), pltpu.VMEM((1, W), F32)],
        compiler_params=_params("arbitrary"),
        name="rg_lru",
    )(p_a, p_a, conv_w, conv_b, wr_bd, br, wi_bd, bi, lam)


def _mla_prep_kernel(p_ref, gcq_ref, gckv_ref, wuq_ref, wuk_ref, wuv_ref, gq_ref, gk_ref,
                     cc_ref, sa_ref, sb_ref, q_ref, k_ref, v_ref):
    p = p_ref[...]
    cq = p[:, :MLA_Q_LORA]
    ckv = p[:, MLA_Q_LORA:MLA_Q_LORA + MLA_KV_LORA]
    kr = p[:, MLA_Q_LORA + MLA_KV_LORA:]
    lane = lax.broadcasted_iota(jnp.int32, kr.shape, 1)
    kr = jnp.where(lane < MLA_ROPE_DIM, kr, 0.0)

    def rms(x, g, n):
        return x * lax.rsqrt(jnp.sum(x * x, axis=-1, keepdims=True) * (1.0 / n) + NORM_EPS) * g

    cqn = rms(cq, gcq_ref[...], MLA_Q_LORA)
    ckvn = rms(ckv, gckv_ref[...], MLA_KV_LORA)
    q = _bdot(cqn, wuq_ref[...])
    kn = _bdot(ckvn, wuk_ref[...])
    v_ref[...] = _bdot(ckvn, wuv_ref[...]).astype(v_ref.dtype)
    cc, sa, sb = cc_ref[...], sa_ref[...], sb_ref[...]
    half = MLA_ROPE_DIM // 2

    def rope(x):
        x2 = x[:, LANE:]
        x2 = x2 * cc + pltpu.roll(x2, LANE - half, 1) * sa + pltpu.roll(x2, half, 1) * sb
        return jnp.concatenate([x[:, :LANE], x2], axis=-1)

    scale = MLA_QK_DIM ** -0.5 * math.log2(math.e)
    for h in range(MLA_HEADS):
        qh = rms(q[:, h * MLA_QK_PAD:(h + 1) * MLA_QK_PAD], gq_ref[...], MLA_QK_DIM)
        q_ref[:, h * MLA_QK_PAD:(h + 1) * MLA_QK_PAD] = (rope(qh) * scale).astype(q_ref.dtype)
        kh = jnp.concatenate([kn[:, h * MLA_NOPE_DIM:(h + 1) * MLA_NOPE_DIM], kr], axis=-1)
        kh = rms(kh, gk_ref[...], MLA_QK_DIM)
        k_ref[:, h * MLA_QK_PAD:(h + 1) * MLA_QK_PAD] = rope(kh).astype(k_ref.dtype)


def _mla_prep(p_mla, g_cq, g_ckv, wuq, wuk, wuv, gq, gk, cc, sa, sb):
    T, PW = p_mla.shape
    tt = 512
    QW = MLA_HEADS * MLA_QK_PAD
    VW = MLA_HEADS * MLA_V_DIM
    row = lambda i: (0, 0)
    tab = pl.BlockSpec((tt, LANE), lambda i: (i, 0))
    return pl.pallas_call(
        _mla_prep_kernel,
        grid=(T // tt,),
        in_specs=[pl.BlockSpec((tt, PW), lambda i: (i, 0)),
                  pl.BlockSpec((1, MLA_Q_LORA), row),
                  pl.BlockSpec((1, MLA_KV_LORA), row),
                  pl.BlockSpec((MLA_Q_LORA, QW), row),
                  pl.BlockSpec((MLA_KV_LORA, MLA_HEADS * MLA_NOPE_DIM), row),
                  pl.BlockSpec((MLA_KV_LORA, VW), row),
                  pl.BlockSpec((1, MLA_QK_PAD), row),
                  pl.BlockSpec((1, MLA_QK_PAD), row),
                  tab, tab, tab],
        out_specs=[pl.BlockSpec((tt, QW), lambda i: (i, 0)),
                   pl.BlockSpec((tt, QW), lambda i: (i, 0)),
                   pl.BlockSpec((tt, VW), lambda i: (i, 0))],
        out_shape=[jax.ShapeDtypeStruct((T, QW), BF16),
                   jax.ShapeDtypeStruct((T, QW), BF16),
                   jax.ShapeDtypeStruct((T, VW), BF16)],
        compiler_params=_params("arbitrary"),
        name="mla_prep",
    )(p_mla, g_cq, g_ckv, wuq, wuk, wuv, gq, gk, cc, sa, sb)


ATTN_BLOCK = 512


ATTN_HEADS_PER_STEP = 4
ATTN_ROW_GROUP = 32


def _attn_kernel(q_ref, k_ref, v_ref, o_ref, s_scr, p_scr, m_scr, l_scr, a_scr, acc_scr, *, blk):
    i = pl.program_id(1)
    NH, RG = ATTN_HEADS_PER_STEP, ATTN_ROW_GROUP
    ntile = blk // LANE
    qk = [slice(t * MLA_QK_PAD, (t + 1) * MLA_QK_PAD) for t in range(NH)]
    vd = [slice(t * MLA_V_DIM, (t + 1) * MLA_V_DIM) for t in range(NH)]
    m_scr[...] = jnp.full(m_scr.shape, -jnp.inf, F32)
    l_scr[...] = jnp.zeros(l_scr.shape, F32)
    acc_scr[...] = jnp.zeros(acc_scr.shape, F32)

    def step(j, masked):
        start = pl.multiple_of(j * blk, blk)
        for t in range(NH):
            s_scr[t] = lax.dot_general(q_ref[:, qk[t]], k_ref[pl.ds(start, blk), qk[t]],
                                       (((1,), (1,)), ((), ())), preferred_element_type=F32)
        for t in range(NH):
            for g in range(blk // RG):
                r = slice(g * RG, (g + 1) * RG)
                s = s_scr[t, r, :]
                if masked:
                    row = g * RG + lax.broadcasted_iota(jnp.int32, (RG, blk), 0)
                    col = lax.broadcasted_iota(jnp.int32, (RG, blk), 1)
                    s = jnp.where(col <= row, s, -jnp.inf)
                    s_scr[t, r, :] = s
                m_old = m_scr[t, r, :]
                m_new = jnp.maximum(m_old, jnp.max(s, axis=-1, keepdims=True))
                a_scr[t, r, :] = jnp.exp2(m_old - m_new)
                m_scr[t, r, :] = m_new
            for g in range(blk // RG):
                r = slice(g * RG, (g + 1) * RG)
                m_new = m_scr[t, r, :]
                p = jnp.exp2(s_scr[t, r, :] - jnp.concatenate([m_new] * ntile, axis=1))
                psum = p[:, :LANE]
                for c in range(1, ntile):
                    psum = psum + p[:, c * LANE:(c + 1) * LANE]
                l_scr[t, r, :] = a_scr[t, r, :] * l_scr[t, r, :] + psum
                p_scr[t, r, :] = p.astype(BF16)
        for t in range(NH):
            acc_scr[t] = acc_scr[t] * a_scr[t] + jnp.dot(
                p_scr[t], v_ref[pl.ds(start, blk), vd[t]], preferred_element_type=F32)

    def body(j, carry):
        step(j, False)
        return carry

    lax.fori_loop(0, i, body, 0)
    step(i, True)
    for t in range(NH):
        l = jnp.sum(l_scr[t], axis=-1, keepdims=True)
        o_ref[:, vd[t]] = (acc_scr[t] / l).astype(o_ref.dtype)


def _causal_attention(q, k, v):
    T = q.shape[0]
    blk, NH = ATTN_BLOCK, ATTN_HEADS_PER_STEP
    assert MLA_V_DIM == LANE and MLA_HEADS % NH == 0
    stat = pltpu.VMEM((NH, blk, LANE), F32)
    return pl.pallas_call(
        functools.partial(_attn_kernel, blk=blk),
        grid=(MLA_HEADS // NH, T // blk),
        in_specs=[pl.BlockSpec((blk, NH * MLA_QK_PAD), lambda h, i: (i, h)),
                  pl.BlockSpec((T, NH * MLA_QK_PAD), lambda h, i: (0, h),
                               pipeline_mode=pl.Buffered(1)),
                  pl.BlockSpec((T, NH * MLA_V_DIM), lambda h, i: (0, h),
                               pipeline_mode=pl.Buffered(1))],
        out_specs=pl.BlockSpec((blk, NH * MLA_V_DIM), lambda h, i: (i, h)),
        out_shape=jax.ShapeDtypeStruct((T, MLA_HEADS * MLA_V_DIM), BF16),
        scratch_shapes=[pltpu.VMEM((NH, blk, blk), F32), pltpu.VMEM((NH, blk, blk), BF16),
                        stat, stat, stat, stat],
        compiler_params=_params("arbitrary", "arbitrary"),
        name="mla_attention",
    )(q, k, v)


_RET_LOG_GAMMA = np.log1p(-np.exp(np.linspace(math.log(1.0 / 32), math.log(1.0 / 512),
                                              RET_HEADS, dtype=np.float32))).astype(np.float32)


RET_GROUP = 4


def _ret_kernel(q_ref, k_ref, v_ref, g_ref, cc_ref, ss_ref, gn_ref, o_ref, st_ref):
    i = pl.program_id(0)
    C, Dh, H = RET_CHUNK, RET_HEAD_DIM, RET_HEADS

    @pl.when(i == 0)
    def _():
        st_ref[...] = jnp.zeros_like(st_ref)

    row = lax.broadcasted_iota(jnp.int32, (C, C), 0)
    col = lax.broadcasted_iota(jnp.int32, (C, C), 1)
    rel = (row - col).astype(F32)
    idx = lax.broadcasted_iota(jnp.int32, (C, 1), 0).astype(F32)
    lgs = [float(x) for x in _RET_LOG_GAMMA]
    decay_in = jnp.stack([jnp.where(rel >= 0, jnp.exp(lg * jnp.maximum(rel, 0.0)), 0.0)
                          for lg in lgs])
    q_dec = jnp.stack([jnp.exp(lg * (idx + 1.0)) for lg in lgs])
    k_dec = jnp.stack([jnp.exp(lg * (C - 1.0 - idx)) for lg in lgs])
    heads = [slice(h * Dh, (h + 1) * Dh) for h in range(H)]
    st = st_ref[...]
    for c in range(RET_GROUP):
        rows = slice(c * C, (c + 1) * C)
        cc, ss = cc_ref[rows, :], ss_ref[rows, :]

        def rope(ref):
            x = jnp.stack([ref[rows, sl] for sl in heads])
            return x * cc + pltpu.roll(x, Dh // 2, 2) * ss

        q = rope(q_ref)
        k = rope(k_ref) * (Dh ** -0.5)
        v = jnp.stack([v_ref[rows, sl] for sl in heads])
        inner = _bmm(q, k, _BNT) * decay_in
        o = _bmm(inner, v, _BNN) + _bmm(q * q_dec, st, _BNN)
        kv = _bmm(k * k_dec, v, _BTN)
        st = jnp.stack([math.exp(lg * C) * st[h] + kv[h] for h, lg in enumerate(lgs)])
        oc = o - jnp.mean(o, axis=-1, keepdims=True)
        on = oc * lax.rsqrt(jnp.mean(oc * oc, axis=-1, keepdims=True) + GN_EPS)
        for h, sl in enumerate(heads):
            g = g_ref[rows, sl]
            o_ref[rows, sl] = (g * _sigmoid(g) * (on[h] * gn_ref[:, sl])).astype(o_ref.dtype)
    st_ref[...] = st


def _retention_branch(p_ret, cc, ss, g_norm):
    T = p_ret.shape[0]
    C, W = RET_CHUNK * RET_GROUP, RET_WIDTH
    blk = lambda n: pl.BlockSpec((C, W), functools.partial(lambda i, n: (i, n), n=n))
    return pl.pallas_call(
        _ret_kernel,
        grid=(T // C,),
        in_specs=[blk(0), blk(1), blk(2), blk(3),
                  pl.BlockSpec((C, LANE), lambda i: (i, 0)),
                  pl.BlockSpec((C, LANE), lambda i: (i, 0)),
                  pl.BlockSpec((1, W), lambda i: (0, 0))],
        out_specs=pl.BlockSpec((C, W), lambda i: (i, 0)),
        out_shape=jax.ShapeDtypeStruct((T, W), BF16),
        scratch_shapes=[pltpu.VMEM((RET_HEADS, RET_HEAD_DIM, RET_HEAD_DIM), F32)],
        compiler_params=_params("arbitrary"),
        name="retention",
    )(p_ret, p_ret, p_ret, p_ret, cc, ss, g_norm)


RWKV_PAD_COLS = 3 * RWKV_WIDTH + 2 * LANE
assert RWKV_DECAY_LORA + RWKV_AAA_LORA == LANE and RWKV_GATE_LORA == LANE


def _split_dot(a, b_exact, terms):
    acc = None
    rem = a
    for _ in range(terms):
        piece = rem.astype(BF16)
        rem = rem - piece.astype(F32)
        d = jnp.dot(piece, b_exact, preferred_element_type=F32)
        acc = d if acc is None else acc + d
    return acc


def _rwkv_prep_kernel(p_ref, mu_ref, w0_ref, a0_ref, wup_ref, aup_ref, gup_ref, kk_ref, ka_ref,
                      rk_ref, ones_ref, tri_ref,
                      kp_ref, rp_ref, kn_ref, bn_ref, knp_ref, bnp_ref, v_ref, g_ref, bonus_ref,
                      pc_ref, buf):
    i = pl.program_id(0)
    tt = p_ref.shape[0]
    W = RWKV_WIDTH
    C = RWKV_CHUNK

    @pl.when(i == 0)
    def _():
        buf[0:SUBLANE, :] = jnp.zeros((SUBLANE, RWKV_PAD_COLS), F32)

    p = p_ref[...]
    buf[SUBLANE:SUBLANE + tt, :] = p
    prev = buf[pl.ds(SUBLANE - 1, tt), :]
    buf[0:SUBLANE, :] = p_ref[tt - SUBLANE:tt, :]
    xs = p + (prev - p) * mu_ref[...]
    r = xs[:, 0:W]
    k = xs[:, W:2 * W]
    v = xs[:, 2 * W:3 * W]
    lora = xs[:, 3 * W:3 * W + LANE]
    gd = xs[:, 3 * W + LANE:3 * W + 2 * LANE]

    w_log = -_softplus(-(w0_ref[...] + _fdot(jnp.tanh(lora), wup_ref[...]))) - 0.5
    logw = -jnp.exp(w_log)
    a = _sigmoid(a0_ref[...] + _bdot(lora, aup_ref[...]))
    g_ref[...] = _bdot(_sigmoid(gd), gup_ref[...])

    ones_bd = ones_ref[...]
    kk = k * kk_ref[...]
    kk = kk / jnp.maximum(jnp.sqrt(_split_dot(kk * kk, ones_bd, 2)), 1e-12)
    kt = k * (1.0 + (a - 1.0) * ka_ref[...])
    bonus_ref[...] = _split_dot(r * kt * rk_ref[...], ones_bd, 2) * v
    v_ref[...] = v.astype(v_ref.dtype)

    tri = tri_ref[...]
    cum = jnp.concatenate([_split_dot_lhs_exact(tri, logw[c * C:(c + 1) * C], 3)
                           for c in range(tt // C)], axis=0)
    tot = jnp.sum(logw.reshape(tt // C, C, W), axis=1)
    pc_ref[...] = jnp.exp(tot)
    to_end = jnp.exp((tot[:, None, :] - cum.reshape(tt // C, C, W)).reshape(tt, W))
    e_neg = jnp.exp(-cum)
    beta = kk * a
    kp_ref[...] = (kk * jnp.exp(cum - logw)).astype(kp_ref.dtype)
    rp_ref[...] = (r * jnp.exp(cum)).astype(rp_ref.dtype)
    kn_ref[...] = (kt * e_neg).astype(kn_ref.dtype)
    bn_ref[...] = (beta * e_neg).astype(bn_ref.dtype)
    knp_ref[...] = (kt * to_end).astype(knp_ref.dtype)
    bnp_ref[...] = (beta * to_end).astype(bnp_ref.dtype)


def _split_dot_lhs_exact(a_exact, b, terms):
    acc = None
    rem = b
    for _ in range(terms):
        piece = rem.astype(BF16)
        rem = rem - piece.astype(F32)
        d = jnp.dot(a_exact, piece, preferred_element_type=F32)
        acc = d if acc is None else acc + d
    return acc


def _rwkv_prep(p_rw, mu, w0, a0, wup, aup, gup, k_k, k_a, r_k):
    T = p_rw.shape[0]
    W, C = RWKV_WIDTH, RWKV_CHUNK
    tt = 512
    hid = np.arange(W) // RWKV_HEAD_DIM
    ones_bd = jnp.asarray((hid[:, None] == hid[None, :]).astype(np.float32), dtype=BF16)
    tid = np.arange(C)
    tri = jnp.asarray((tid[:, None] >= tid[None, :]).astype(np.float32), dtype=BF16)
    row = lambda i: (0, 0)
    vec = pl.BlockSpec((1, W), row)
    lora = pl.BlockSpec((LANE, W), row)
    out = pl.BlockSpec((tt, W), lambda i: (i, 0))
    lo = jax.ShapeDtypeStruct((T, W), BF16)
    hi = jax.ShapeDtypeStruct((T, W), F32)
    return pl.pallas_call(
        _rwkv_prep_kernel,
        grid=(T // tt,),
        in_specs=[pl.BlockSpec((tt, RWKV_PAD_COLS), lambda i: (i, 0)),
                  pl.BlockSpec((1, RWKV_PAD_COLS), row),
                  vec, vec, lora, lora, lora, vec, vec, vec,
                  pl.BlockSpec((W, W), row),
                  pl.BlockSpec((C, C), row)],
        out_specs=[out] * 9 + [pl.BlockSpec((tt // C, W), lambda i: (i, 0))],
        out_shape=[lo] * 7 + [hi, hi, jax.ShapeDtypeStruct((T // C, W), F32)],
        scratch_shapes=[pltpu.VMEM((tt + SUBLANE, RWKV_PAD_COLS), F32)],
        compiler_params=_params("arbitrary"),
        name="rwkv_prep",
    )(p_rw, mu, w0, a0, wup, aup, gup, k_k, k_a, r_k, ones_bd, tri)


def _bmm(a, b, dims):
    return lax.dot_general(a.astype(BF16), b.astype(BF16), dims, preferred_element_type=F32)


_BNN = (((2,), (1,)), ((0,), (0,)))
_BNT = (((2,), (2,)), ((0,), (0,)))
_BTN = (((1,), (1,)), ((0,), (0,)))
RWKV_GROUP = 4


def _rwkv_rec_kernel(kp_ref, rp_ref, kn_ref, bn_ref, knp_ref, bnp_ref, v_ref, pc_ref, gn_ref,
                     bonus_ref, g_ref, y_ref, s_ref, *, nchunk):
    C = RWKV_CHUNK
    P = 2 * C
    assert P == LANE

    @pl.when(pl.program_id(0) == 0)
    def _():
        s_ref[...] = jnp.zeros_like(s_ref)

    row = lax.broadcasted_iota(jnp.int32, (P, P), 0)
    col = lax.broadcasted_iota(jnp.int32, (P, P), 1)

    def blocks(n):
        return jnp.where((row // n) == (col // n), 1.0, 0.0)

    own = blocks(C)
    own_bf = own.astype(BF16)
    lower = jnp.where(row > col, 1.0, 0.0)
    m_strict = own * lower
    m_incl = own * jnp.where(row >= col, 1.0, 0.0)
    m_base = blocks(16) * lower
    m_l32 = blocks(32) * lower - m_base
    m_l64 = m_strict - m_base - m_l32
    eye = jnp.where(row == col, 1.0, 0.0)

    npair = RWKV_HEADS // 2
    lanes = [slice(p * LANE, (p + 1) * LANE) for p in range(npair)]
    G = RWKV_GROUP

    def group(gi, carry):
        rows = [pl.ds(pl.multiple_of((gi * G + c) * C, C), C) for c in range(G)]

        def load(ref):
            x = jnp.stack([ref[sl, ln] for sl in rows for ln in lanes])
            return jnp.concatenate([x, x], axis=1) * own_bf

        kp, rp, kn, bn, knp, bnp, v = (load(r) for r in (kp_ref, rp_ref, kn_ref, bn_ref,
                                                         knp_ref, bnp_ref, v_ref))
        q2 = jnp.concatenate([kp, rp], axis=1)
        a_kn = _bmm(q2, kn, _BNT)
        a_bn = _bmm(q2, bn, _BNT)
        a_k = a_kn[:, :P] * m_strict
        a_rk = a_kn[:, P:] * m_incl
        a_b = a_bn[:, :P]
        a_rb = a_bn[:, P:] * m_incl

        n1 = -(a_b * m_base)
        inv = eye + n1
        n2 = _bmm(n1, n1, _BNN)
        inv = inv + _bmm(inv, n2, _BNN)
        n4 = _bmm(n2, n2, _BNN)
        inv = inv + _bmm(inv, n4, _BNN)
        n8 = _bmm(n4, n4, _BNN)
        inv = inv + _bmm(inv, n8, _BNN)
        for msk in (m_l32, m_l64):
            inv = inv - _bmm(inv, _bmm(a_b * msk, inv, _BNN), _BNN)
        av = _bmm(jnp.concatenate([a_k, a_rk], axis=1), v, _BNN)
        k2 = jnp.concatenate([knp, bnp], axis=1)

        s = s_ref[...]
        inv_n = 1.0 / RWKV_HEAD_DIM
        for c in range(G):
            b = slice(c * npair, (c + 1) * npair)
            x1 = _bmm(q2[b], s, _BNT)
            u = _bmm(inv[b], x1[:, :P] + av[b, :P], _BNN)
            o = x1[:, P:] + av[b, P:] - _bmm(a_rb[b], u, _BNN)
            vu = jnp.concatenate([v[b], (-u).astype(BF16)], axis=1)
            pc = pc_ref[gi * G + c]
            pc = jnp.stack([pc[:, ln] for ln in lanes])
            s = s * pc + _bmm(vu, k2[b], _BTN)

            oc = (o - jnp.sum(o, axis=-1, keepdims=True) * inv_n) * own
            on = oc * lax.rsqrt(jnp.sum(oc * oc, axis=-1, keepdims=True) * inv_n + GN_EPS)
            on = on[:, :C] + on[:, C:]
            for p, ln in enumerate(lanes):
                y = (on[p] * gn_ref[:, ln] + bonus_ref[rows[c], ln]) * g_ref[rows[c], ln]
                y_ref[rows[c], ln] = y.astype(y_ref.dtype)
        s_ref[...] = s
        return carry

    lax.fori_loop(0, nchunk // G, group, 0)


def _rwkv_recurrence(kp, rp, kn, bn, knp, bnp, v, pc, gn, bonus, g):
    T, W = kp.shape
    C = RWKV_CHUNK
    tb = 512
    nchunk = tb // C
    blk = pl.BlockSpec((tb, W), lambda i: (i, 0))
    return pl.pallas_call(
        functools.partial(_rwkv_rec_kernel, nchunk=nchunk),
        grid=(T // tb,),
        in_specs=[blk] * 7 + [pl.BlockSpec((nchunk, 1, W), lambda i: (i, 0, 0)),
                              pl.BlockSpec((1, W), lambda i: (0, 0)), blk, blk],
        out_specs=blk,
        out_shape=jax.ShapeDtypeStruct((T, W), BF16),
        scratch_shapes=[pltpu.VMEM((RWKV_HEADS // 2, LANE, LANE), F32)],
        compiler_params=_params("arbitrary"),
        name="rwkv_recurrence",
    )(kp, rp, kn, bn, knp, bnp, v, pc, gn, bonus, g)


def _rwkv_branch(p_rw, mu, w0, a0, wup, aup, gup, k_k, k_a, r_k, g_norm):
    kp, rp, kn, bn, knp, bnp, v, g, bonus, pc = _rwkv_prep(p_rw, mu, w0, a0, wup, aup, gup,
                                                          k_k, k_a, r_k)
    pc = pc.reshape(pc.shape[0], 1, pc.shape[1])
    return _rwkv_recurrence(kp, rp, kn, bn, knp, bnp, v, pc, g_norm.reshape(1, -1), bonus, g)


def _block_diag(w):
    n, bi, bj = w.shape
    eye = jnp.eye(n, dtype=w.dtype)
    return (eye[:, None, :, None] * w[:, :, None, :]).reshape(n * bi, n * bj)


def _rope_tables(positions):
    pos = positions.astype(F32).reshape(-1, 1)
    T = pos.shape[0]

    def cs(dim):
        inv = 1.0 / (ROPE_BASE ** (jnp.arange(0, dim, 2, dtype=F32) / dim))
        ang = pos * inv
        return jnp.cos(ang), jnp.sin(ang)

    cm, sm = cs(MLA_ROPE_DIM)
    z32 = jnp.zeros((T, MLA_ROPE_DIM // 2), F32)
    z64 = jnp.zeros((T, LANE - MLA_ROPE_DIM), F32)
    mla = (jnp.concatenate([cm, cm, z64], axis=1),
           jnp.concatenate([-sm, z32, z64], axis=1),
           jnp.concatenate([z32, sm, z64], axis=1))
    cr, sr = cs(RET_HEAD_DIM)
    ret = (jnp.concatenate([cr, cr], axis=1), jnp.concatenate([-sr, sr], axis=1))
    return mla, ret


def _mla_weights(w_uq, w_ukv, g_qn, g_kn):
    wq = w_uq.reshape(MLA_Q_LORA, MLA_HEADS, MLA_QK_DIM)
    wq = jnp.pad(wq, ((0, 0), (0, 0), (0, MLA_QK_PAD - MLA_QK_DIM)))
    wq = wq.reshape(MLA_Q_LORA, MLA_HEADS * MLA_QK_PAD)
    wkv = w_ukv.reshape(MLA_KV_LORA, MLA_HEADS, MLA_NOPE_DIM + MLA_V_DIM)
    wk = wkv[:, :, :MLA_NOPE_DIM].reshape(MLA_KV_LORA, MLA_HEADS * MLA_NOPE_DIM)
    wv = wkv[:, :, MLA_NOPE_DIM:].reshape(MLA_KV_LORA, MLA_HEADS * MLA_V_DIM)
    pad = MLA_QK_PAD - MLA_QK_DIM
    gq = jnp.pad(g_qn, (0, pad)).reshape(1, MLA_QK_PAD)
    gk = jnp.pad(g_kn, (0, pad)).reshape(1, MLA_QK_PAD)
    return wq.astype(BF16), wk.astype(BF16), wv.astype(BF16), gq, gk


def kernel(x, c, positions, ada_w, ada_b, norm_mix, norm_ffn, w_in, conv_w, conv_b, lru_wr, lru_br, lru_wi, lru_bi, lru_lam, mla_g_cq, mla_g_ckv, mla_w_uq, mla_w_ukv, mla_g_qn, mla_g_kn, ret_g_norm, rwkv_mu, rwkv_w0, rwkv_w_up, rwkv_a0, rwkv_a_up, rwkv_g_up, rwkv_k_k, rwkv_k_a, rwkv_r_k, rwkv_g_norm, w_branch, w_out, ffn_w_in, ffn_w_out):
    B, T, D = x.shape
    assert B == 1 and D == D_MODEL
    depth = ada_w.shape[0]
    xt = x.reshape(T, D)
    mod_all = _modulation(c, ada_w, ada_b)
    (cc_m, sa_m, sb_m), (cc_r, ss_r) = _rope_tables(positions)
    w_in_t = jnp.swapaxes(w_in, 1, 2).astype(BF16)
    ffn_w_out_bf = ffn_w_out.astype(BF16)

    o_a = GATE_COLS
    o_cq = o_a + 2 * LRU_WIDTH
    o_ckv = o_cq + MLA_Q_LORA
    o_kr = o_ckv + MLA_KV_LORA
    o_ret = o_kr + MLA_ROPE_DIM
    o_rw = o_ret + 4 * RET_WIDTH

    for l in range(depth):
        mod = mod_all[l]
        dl, al = RWKV_DECAY_LORA, RWKV_AAA_LORA
        w_ret_t = w_in_t[l, o_ret:o_rw]
        w_rw_t = w_in_t[l, o_rw:]
        w_up_p = jnp.pad(rwkv_w_up[l], ((0, al), (0, 0)))
        a_up_p = jnp.pad(rwkv_a_up[l], ((dl, 0), (0, 0)))

        h = _mod_norm(xt, norm_mix[l].reshape(1, D), mod, 0)
        p_a = _ws_matmul_nt(h, w_in_t, l, o_a, 2 * LRU_WIDTH, 512, "in_proj_lru")
        p_mla = _ws_matmul_nt(h, w_in_t, l, o_cq, o_ret - o_cq + MLA_ROPE_DIM, 768, "in_proj_mla")
        p_ret = _matmul_nt(h, w_ret_t, "in_proj_ret")
        p_rw = _matmul_nt(h, w_rw_t, "in_proj_rwkv")

        y_a = _lru_branch(p_a, conv_w[l], conv_b[l].reshape(1, -1),
                          _block_diag(lru_wr[l]).astype(BF16), lru_br[l].reshape(1, -1),
                          _block_diag(lru_wi[l]).astype(BF16), lru_bi[l].reshape(1, -1),
                          lru_lam[l].reshape(1, -1))

        wq, wk, wv, gq, gk = _mla_weights(mla_w_uq[l], mla_w_ukv[l], mla_g_qn[l], mla_g_kn[l])
        q, k, v = _mla_prep(p_mla, mla_g_cq[l].reshape(1, -1), mla_g_ckv[l].reshape(1, -1),
                            wq, wk, wv, gq, gk, cc_m, sa_m, sb_m)
        y_b = _causal_attention(q, k, v)

        y_c = _retention_branch(p_ret, cc_r, ss_r, ret_g_norm[l].reshape(1, -1))

        y_d = _rwkv_branch(p_rw, rwkv_mu[l].reshape(1, -1), rwkv_w0[l].reshape(1, -1),
                           rwkv_a0[l].reshape(1, -1), w_up_p, a_up_p,
                           rwkv_g_up[l], rwkv_k_k[l].reshape(1, -1), rwkv_k_a[l].reshape(1, -1),
                           rwkv_r_k[l].reshape(1, -1), rwkv_g_norm[l])

        merged = _gated_merge(h, w_in_t, (y_a, y_b, y_c, y_d), w_branch, l)
        xt = _matmul_gated_residual(merged, w_out, l, xt, mod, 2, 1024, 1024, "out_proj")

        h2 = _mod_norm(xt, norm_ffn[l].reshape(1, D), mod, 3)
        act = _swiglu_in(h2, ffn_w_in, l)
        xt = _matmul_gated_residual(act, ffn_w_out_bf, l, xt, mod, 5, 512, 1024, "ffn_out")
    return xt.reshape(B, T, D)
```
